```python
import math
import jax, jax.numpy as jnp
from jax import lax
import numpy as np

D_MODEL = 2048
BATCH = 4
SEQ = 4096
DEPTH = 2

S5_WIDTH = D_MODEL // 2
S5_GROUP = 16
S5_GROUPS = S5_WIDTH // S5_GROUP
S5_STATE = 64
LRU_WIDTH = D_MODEL // 2
LRU_BLOCKS = 16
LRU_BLOCK = LRU_WIDTH // LRU_BLOCKS
CONV_WIDTH = 4
LRU_C = 8.0
HEAD_DIM = 128
N_HEADS = 8
ATTN_WIDTH = N_HEADS * HEAD_DIM
Q_BLOCK = 128
N_BRANCH = 3
D_FF = 4 * D_MODEL
IN_COLS = S5_WIDTH + LRU_WIDTH + 3 * ATTN_WIDTH + N_BRANCH * D_MODEL
EPS = 1e-6

kernel_name = "hybrid_s5_rglru_stickbreak_gated"


def rms_norm(x, g):
    x32 = x.astype(jnp.float32)
    y = x32 * lax.rsqrt(jnp.mean(x32 * x32, axis=-1, keepdims=True) + EPS)
    return (y * g.astype(jnp.float32)).astype(x.dtype)


def linear_scan(a, b):
    def combine(e1, e2):
        a1, b1 = e1
        a2, b2 = e2
        return (a1 * a2, a2 * b1 + b2)
    _, h = lax.associative_scan(combine, (a, b), axis=0)
    return h


def s5_branch(u, lam_re, lam_im, log_dt, b_re, b_im, c_re, c_im, d_skip, w_glu, b_glu):
    bsz, L, _ = u.shape
    u32 = u.astype(jnp.float32)
    dt = jnp.exp(log_dt.astype(jnp.float32))[:, None]
    lam = lax.complex(lam_re.astype(jnp.float32), lam_im.astype(jnp.float32))
    lam_bar = jnp.exp(lam * dt)
    b_mat = lax.complex(b_re.astype(jnp.float32), b_im.astype(jnp.float32))
    b_bar = ((lam_bar - 1.0) / lam)[..., None] * b_mat
    ug = u32.transpose(1, 0, 2).reshape(L, bsz, S5_GROUPS, S5_GROUP)
    bu = lax.complex(jnp.einsum('lbgc,gpc->lbgp', ug, jnp.real(b_bar)),
                     jnp.einsum('lbgc,gpc->lbgp', ug, jnp.imag(b_bar)))
    a = jnp.broadcast_to(lam_bar, (L, 1) + lam_bar.shape)
    h = linear_scan(a, bu)
    y = (jnp.einsum('lbgp,gcp->lbgc', jnp.real(h), c_re.astype(jnp.float32))
         - jnp.einsum('lbgp,gcp->lbgc', jnp.imag(h), c_im.astype(jnp.float32)))
    y = y.reshape(L, bsz, S5_WIDTH).transpose(1, 0, 2) + d_skip.astype(jnp.float32) * u32
    y = jax.nn.gelu(y)
    return y * jax.nn.sigmoid(y @ w_glu.astype(jnp.float32) + b_glu.astype(jnp.float32))


def causal_depthwise_conv(x, w, b):
    K = w.shape[0]
    L = x.shape[1]
    xp = jnp.pad(x, ((0, 0), (K - 1, 0), (0, 0)))
    out = b + xp[:, 0:L] * w[0]
    for kk in range(1, K):
        out = out + xp[:, kk:kk + L] * w[kk]
    return out


def rglru_branch(x, conv_w, conv_b, w_r, b_r, w_i, b_i, lam):
    bsz, L, _ = x.shape
    xc = causal_depthwise_conv(x, conv_w, conv_b).astype(jnp.float32)
    xb = xc.reshape(bsz, L, LRU_BLOCKS, LRU_BLOCK)
    r = jax.nn.sigmoid(jnp.einsum('blnj,njk->blnk', xb, w_r.astype(jnp.float32)) + b_r).reshape(bsz, L, LRU_WIDTH)
    i = jax.nn.sigmoid(jnp.einsum('blnj,njk->blnk', xb, w_i.astype(jnp.float32)) + b_i).reshape(bsz, L, LRU_WIDTH)
    log_a = -LRU_C * r * jax.nn.softplus(-lam.astype(jnp.float32))
    a = jnp.exp(log_a)
    gated = jnp.sqrt(-jnp.expm1(2.0 * log_a)) * (i * xc)
    h = linear_scan(a.transpose(1, 0, 2), gated.transpose(1, 0, 2))
    return h.transpose(1, 0, 2)


def stick_breaking_attention(q, k, v):
    bsz, L, H, Dh = q.shape
    nb = L // Q_BLOCK
    scale = Dh ** -0.5
    q_blocks = q.astype(jnp.float32).reshape(bsz, nb, Q_BLOCK, H, Dh).transpose(1, 0, 3, 2, 4)
    kh = k.astype(jnp.float32).transpose(0, 2, 1, 3)
    vh = v.astype(jnp.float32).transpose(0, 2, 1, 3)
    key_pos = jnp.arange(L)

    def one_block(args):
        qb, bi = args
        q_pos = bi * Q_BLOCK + jnp.arange(Q_BLOCK)
        z = jnp.einsum('bhqd,bhkd->bhqk', qb, kh) * scale
        causal = key_pos[None, :] < q_pos[:, None]
        log_keep = jnp.where(causal, jax.nn.log_sigmoid(-z), 0.0)
        log_stick = lax.cumsum(log_keep, axis=3, reverse=True) - log_keep
        w = jnp.where(causal, jnp.exp(jax.nn.log_sigmoid(z) + log_stick), 0.0)
        return jnp.einsum('bhqk,bhkd->bhqd', w, vh)

    out = lax.map(one_block, (q_blocks, jnp.arange(nb)))
    return out.transpose(1, 0, 3, 2, 4).reshape(bsz, L, H * Dh)


def hybrid_mixer(xn, w_in, b_gate,
                 s5_lam_re, s5_lam_im, s5_log_dt, s5_b_re, s5_b_im, s5_c_re, s5_c_im, s5_d, s5_w_glu, s5_b_glu,
                 lru_conv_w, lru_conv_b, lru_w_r, lru_b_r, lru_w_i, lru_b_i, lru_lambda,
                 w_br_s5, w_br_lru, w_br_attn, w_out):
    bsz, L, _ = xn.shape
    proj = xn @ w_in
    offs = [S5_WIDTH, S5_WIDTH + LRU_WIDTH, S5_WIDTH + LRU_WIDTH + ATTN_WIDTH,
            S5_WIDTH + LRU_WIDTH + 2 * ATTN_WIDTH, S5_WIDTH + LRU_WIDTH + 3 * ATTN_WIDTH]
    u_s5, u_lru, q, k, v, gate_logits = jnp.split(proj, offs, axis=-1)
    gates = jax.nn.sigmoid(gate_logits + b_gate)
    g_s5, g_lru, g_attn = jnp.split(gates, [D_MODEL, 2 * D_MODEL], axis=-1)

    y_s5 = s5_branch(u_s5, s5_lam_re, s5_lam_im, s5_log_dt, s5_b_re, s5_b_im, s5_c_re, s5_c_im,
                     s5_d, s5_w_glu, s5_b_glu)
    y_lru = rglru_branch(u_lru, lru_conv_w, lru_conv_b, lru_w_r, lru_b_r, lru_w_i, lru_b_i, lru_lambda)
    y_attn = stick_breaking_attention(q.reshape(bsz, L, N_HEADS, HEAD_DIM),
                                      k.reshape(bsz, L, N_HEADS, HEAD_DIM),
                                      v.reshape(bsz, L, N_HEADS, HEAD_DIM))
    merged = (g_s5 * (y_s5.astype(xn.dtype) @ w_br_s5)
              + g_lru * (y_lru.astype(xn.dtype) @ w_br_lru)
              + g_attn * (y_attn.astype(xn.dtype) @ w_br_attn))
    return merged @ w_out


def squared_relu_mlp(x, w_up, w_down):
    hdn = jax.nn.relu(x @ w_up)
    return (hdn * hdn) @ w_down


def setup_inputs(seed: int = 0) -> dict:
    key = jax.random.key(seed)
    ks = jax.random.split(key, 32)
    f32 = jnp.float32

    def nrm(k, shape, scale):
        return jax.random.normal(k, shape, f32) * scale

    x = jax.random.normal(ks[0], (BATCH, SEQ, D_MODEL), f32)
    norm_mix_g = 1.0 + nrm(ks[1], (DEPTH, D_MODEL), 0.01)
    w_in = nrm(ks[2], (DEPTH, D_MODEL, IN_COLS), D_MODEL ** -0.5)
    b_gate = nrm(ks[3], (DEPTH, N_BRANCH * D_MODEL), 0.01)
    n_idx = jnp.arange(S5_STATE, dtype=f32)
    s5_lam_re = -0.5 + nrm(ks[4], (DEPTH, S5_GROUPS, S5_STATE), 0.01)
    s5_lam_im = math.pi * n_idx[None, None, :] + nrm(ks[5], (DEPTH, S5_GROUPS, S5_STATE), 0.01)
    s5_log_dt = jax.random.uniform(ks[6], (DEPTH, S5_GROUPS), f32, math.log(1e-3), math.log(1e-1))
    s5_b_re = nrm(ks[7], (DEPTH, S5_GROUPS, S5_STATE, S5_GROUP), (2 * S5_GROUP) ** -0.5)
    s5_b_im = nrm(ks[8], (DEPTH, S5_GROUPS, S5_STATE, S5_GROUP), (2 * S5_GROUP) ** -0.5)
    s5_c_re = nrm(ks[9], (DEPTH, S5_GROUPS, S5_GROUP, S5_STATE), S5_STATE ** -0.5)
    s5_c_im = nrm(ks[10], (DEPTH, S5_GROUPS, S5_GROUP, S5_STATE), S5_STATE ** -0.5)
    s5_d = nrm(ks[11], (DEPTH, S5_WIDTH), 1.0)
    s5_w_glu = nrm(ks[12], (DEPTH, S5_WIDTH, S5_WIDTH), S5_WIDTH ** -0.5)
    s5_b_glu = nrm(ks[13], (DEPTH, S5_WIDTH), 0.01)
    lru_conv_w = nrm(ks[14], (DEPTH, CONV_WIDTH, LRU_WIDTH), CONV_WIDTH ** -0.5)
    lru_conv_b = nrm(ks[15], (DEPTH, LRU_WIDTH), 0.01)
    lru_w_r = nrm(ks[16], (DEPTH, LRU_BLOCKS, LRU_BLOCK, LRU_BLOCK), LRU_BLOCK ** -0.5)
    lru_b_r = nrm(ks[17], (DEPTH, LRU_BLOCKS, LRU_BLOCK), 0.01)
    lru_w_i = nrm(ks[18], (DEPTH, LRU_BLOCKS, LRU_BLOCK, LRU_BLOCK), LRU_BLOCK ** -0.5)
    lru_b_i = nrm(ks[19], (DEPTH, LRU_BLOCKS, LRU_BLOCK), 0.01)
    a_pow = jax.random.uniform(ks[20], (DEPTH, LRU_WIDTH), f32, 0.9, 0.999)
    a0 = a_pow ** (1.0 / LRU_C)
    lru_lambda = jnp.log(a0) - jnp.log1p(-a0)
    w_br_s5 = nrm(ks[21], (DEPTH, S5_WIDTH, D_MODEL), S5_WIDTH ** -0.5)
    w_br_lru = nrm(ks[22], (DEPTH, LRU_WIDTH, D_MODEL), LRU_WIDTH ** -0.5)
    w_br_attn = nrm(ks[23], (DEPTH, ATTN_WIDTH, D_MODEL), ATTN_WIDTH ** -0.5)
    w_out = nrm(ks[24], (DEPTH, D_MODEL, D_MODEL), D_MODEL ** -0.5)
    norm_mlp_g = 1.0 + nrm(ks[25], (DEPTH, D_MODEL), 0.01)
    w_up = nrm(ks[26], (DEPTH, D_MODEL, D_FF), D_MODEL ** -0.5)
    w_down = nrm(ks[27], (DEPTH, D_FF, D_MODEL), D_FF ** -0.5)
    final_norm_g = 1.0 + nrm(ks[28], (D_MODEL,), 0.01)
    return {"x": x, "norm_mix_g": norm_mix_g, "w_in": w_in, "b_gate": b_gate,
            "s5_lam_re": s5_lam_re, "s5_lam_im": s5_lam_im, "s5_log_dt": s5_log_dt,
            "s5_b_re": s5_b_re, "s5_b_im": s5_b_im, "s5_c_re": s5_c_re, "s5_c_im": s5_c_im,
            "s5_d": s5_d, "s5_w_glu": s5_w_glu, "s5_b_glu": s5_b_glu,
            "lru_conv_w": lru_conv_w, "lru_conv_b": lru_conv_b, "lru_w_r": lru_w_r, "lru_b_r": lru_b_r,
            "lru_w_i": lru_w_i, "lru_b_i": lru_b_i, "lru_lambda": lru_lambda,
            "w_br_s5": w_br_s5, "w_br_lru": w_br_lru, "w_br_attn": w_br_attn, "w_out": w_out,
            "norm_mlp_g": norm_mlp_g, "w_up": w_up, "w_down": w_down, "final_norm_g": final_norm_g}


def reference(x, norm_mix_g, w_in, b_gate,
              s5_lam_re, s5_lam_im, s5_log_dt, s5_b_re, s5_b_im, s5_c_re, s5_c_im,
              s5_d, s5_w_glu, s5_b_glu,
              lru_conv_w, lru_conv_b, lru_w_r, lru_b_r, lru_w_i, lru_b_i, lru_lambda,
              w_br_s5, w_br_lru, w_br_attn, w_out,
              norm_mlp_g, w_up, w_down, final_norm_g):
    h = x
    for l in range(DEPTH):
        xn = rms_norm(h, norm_mix_g[l])
        mixed = hybrid_mixer(xn, w_in[l], b_gate[l],
                             s5_lam_re[l], s5_lam_im[l], s5_log_dt[l], s5_b_re[l], s5_b_im[l],
                             s5_c_re[l], s5_c_im[l], s5_d[l], s5_w_glu[l], s5_b_glu[l],
                             lru_conv_w[l], lru_conv_b[l], lru_w_r[l], lru_b_r[l], lru_w_i[l],
                             lru_b_i[l], lru_lambda[l],
                             w_br_s5[l], w_br_lru[l], w_br_attn[l], w_out[l])
        h = h + mixed.astype(h.dtype)
        hn = rms_norm(h, norm_mlp_g[l])
        h = h + squared_relu_mlp(hn, w_up[l], w_down[l]).astype(h.dtype)
    return rms_norm(h, final_norm_g)
```

```python
import functools
import math

import jax
import jax.numpy as jnp
from jax import lax
from jax.experimental import pallas as pl
from jax.experimental.pallas import tpu as pltpu

F32 = jnp.float32
BF16 = jnp.bfloat16

EPS = 1e-6
HEAD_DIM = 128
S5_GROUP = 16
S5_CHUNK = 16
LRU_BLOCK = 64
LRU_SUPER = 256
LRU_C = 8.0
N_BRANCH = 3
VMEM_LIMIT = 56 * 1024 * 1024
HIGHEST = lax.Precision.HIGHEST


def _cparams(sem):
    return pltpu.CompilerParams(dimension_semantics=sem, vmem_limit_bytes=VMEM_LIMIT)


def _softplus(z):
    return jnp.maximum(z, 0.0) + jnp.log(1.0 + jnp.exp(-jnp.abs(z)))


def _sigmoid(z):
    return 1.0 / (1.0 + jnp.exp(-z))


def _rms_rows(x, g):
    ms = jnp.mean(x * x, axis=-1, keepdims=True)
    return (x * lax.rsqrt(ms + EPS)) * g


def _inproj_kernel(x_ref, g_ref, w_ref, o_ref, xn_ref, *, row_chunk):
    @pl.when(pl.program_id(1) == 0)
    def _():
        g = g_ref[...]

        def body(c, carry):
            r = pl.multiple_of(c * row_chunk, row_chunk)
            xn_ref[pl.ds(r, row_chunk), :] = _rms_rows(x_ref[pl.ds(r, row_chunk), :], g).astype(BF16)
            return carry

        lax.fori_loop(0, x_ref.shape[0] // row_chunk, body, 0)

    o_ref[...] = jnp.dot(xn_ref[...], w_ref[...], preferred_element_type=F32).astype(o_ref.dtype)


def _inproj(h, g, w, *, tm, tn):
    t, d = h.shape
    n = w.shape[1]
    return pl.pallas_call(
        functools.partial(_inproj_kernel, row_chunk=32),
        out_shape=jax.ShapeDtypeStruct((t, n), BF16),
        grid=(t // tm, n // tn),
        in_specs=[pl.BlockSpec((tm, d), lambda i, j: (i, 0)),
                  pl.BlockSpec((1, d), lambda i, j: (0, 0)),
                  pl.BlockSpec((d, tn), lambda i, j: (0, j))],
        out_specs=pl.BlockSpec((tm, tn), lambda i, j: (i, j)),
        scratch_shapes=[pltpu.VMEM((tm, d), BF16)],
        compiler_params=_cparams(("parallel", "arbitrary")),
        name="inproj",
    )(h, g.reshape(1, d), w)


def _s5_kernel(u_ref, m_ref, sre_ref, sim_ref, rre_ref, rim_ref, lr_ref, li_ref, o_ref,
               xre_ref, xim_ref, hre_ref, him_ref, *, nb):
    u = u_ref[...]
    xre_ref[...] = jnp.dot(u, sre_ref[...], preferred_element_type=F32)
    xim_ref[...] = jnp.dot(u, sim_ref[...], preferred_element_type=F32)
    lr = lr_ref[...]
    li = li_ref[...]
    rows = u.shape[0]
    per_tile = 8 // nb

    def body(k, carry):
        hr, hi = carry
        r = pl.multiple_of(k * 8, 8)
        xr = xre_ref[pl.ds(r, 8), :]
        xi = xim_ref[pl.ds(r, 8), :]
        srs, sis = [], []
        for c in range(per_tile):
            srs.append(hr)
            sis.append(hi)
            nr = lr * hr - li * hi + xr[c * nb:(c + 1) * nb, :]
            ni = lr * hi + li * hr + xi[c * nb:(c + 1) * nb, :]
            hr, hi = nr, ni
        hre_ref[pl.ds(r, 8), :] = jnp.concatenate(srs, axis=0)
        him_ref[pl.ds(r, 8), :] = jnp.concatenate(sis, axis=0)
        return hr, hi

    p = lr.shape[1]
    zero = jnp.zeros((nb, p), F32)
    lax.fori_loop(0, rows // 8, body, (zero, zero))

    y = jnp.dot(u, m_ref[...], preferred_element_type=F32)
    y = y + jnp.dot(hre_ref[...].astype(BF16), rre_ref[...], preferred_element_type=F32)
    y = y + jnp.dot(him_ref[...].astype(BF16), rim_ref[...], preferred_element_type=F32)
    o_ref[...] = y


def _s5_tables(lam_re, lam_im, log_dt, b_re, b_im, c_re, c_im):
    tc = S5_CHUNK
    g, p = lam_re.shape
    c = b_re.shape[-1]
    lr = lam_re.astype(F32)
    li = lam_im.astype(F32)
    dt = jnp.exp(log_dt.astype(F32))[:, None]
    ar, ai = lr * dt, li * dt
    ks = jnp.arange(tc + 1, dtype=F32)[None, :, None]
    mag = jnp.exp(ks * ar[:, None, :])
    pr = mag * jnp.cos(ks * ai[:, None, :])
    pi = mag * jnp.sin(ks * ai[:, None, :])
    nr = jnp.expm1(ar) * jnp.cos(ai) - 2.0 * jnp.sin(0.5 * ai) ** 2
    ni = jnp.exp(ar) * jnp.sin(ai)
    den = lr * lr + li * li
    fr = (nr * lr + ni * li) / den
    fi = (ni * lr - nr * li) / den
    bbr = fr[..., None] * b_re - fi[..., None] * b_im
    bbi = fr[..., None] * b_im + fi[..., None] * b_re
    er = pr[..., None] * bbr[:, None] - pi[..., None] * bbi[:, None]
    ei = pr[..., None] * bbi[:, None] + pi[..., None] * bbr[:, None]
    kk = (jnp.einsum('gop,gkpi->gkoi', c_re, er[:, :tc], precision=HIGHEST)
          - jnp.einsum('gop,gkpi->gkoi', c_im, ei[:, :tc], precision=HIGHEST))
    jj = jnp.arange(tc)[:, None]
    ii = jnp.arange(tc)[None, :]
    lag = jnp.clip(ii - jj, 0, tc - 1)
    m = jnp.where((ii >= jj)[None, :, :, None, None], kk[:, lag], 0.0)
    m = m.transpose(0, 1, 4, 2, 3).reshape(g, tc * c, tc * c)
    rev = (tc - 1) - jnp.arange(tc)
    s_re = er[:, rev].transpose(0, 1, 3, 2).reshape(g, tc * c, p)
    s_im = ei[:, rev].transpose(0, 1, 3, 2).reshape(g, tc * c, p)
    pr1, pi1 = pr[:, 1:, None, :], pi[:, 1:, None, :]
    qr = c_re[:, None] * pr1 - c_im[:, None] * pi1
    qi = c_re[:, None] * pi1 + c_im[:, None] * pr1
    r_re = qr.transpose(0, 3, 1, 2).reshape(g, p, tc * c)
    r_im = (-qi).transpose(0, 3, 1, 2).reshape(g, p, tc * c)
    return (m.astype(BF16), s_re.astype(BF16), s_im.astype(BF16), r_re.astype(BF16), r_im.astype(BF16),
            pr[:, tc].reshape(g, 1, p), pi[:, tc].reshape(g, 1, p))


def _s5_scan(u_cm, tables, *, nb):
    g, rows, w = u_cm.shape
    m, s_re, s_im, r_re, r_im, lr, li = tables
    p = lr.shape[-1]
    grp = lambda i: (i, 0, 0)
    return pl.pallas_call(
        functools.partial(_s5_kernel, nb=nb),
        out_shape=jax.ShapeDtypeStruct((g, rows, w), F32),
        grid=(g,),
        in_specs=[pl.BlockSpec((None, rows, w), grp),
                  pl.BlockSpec((None, w, w), grp),
                  pl.BlockSpec((None, w, p), grp),
                  pl.BlockSpec((None, w, p), grp),
                  pl.BlockSpec((None, p, w), grp),
                  pl.BlockSpec((None, p, w), grp),
                  pl.BlockSpec((None, 1, p), grp),
                  pl.BlockSpec((None, 1, p), grp)],
        out_specs=pl.BlockSpec((None, rows, w), grp),
        scratch_shapes=[pltpu.VMEM((rows, p), F32)] * 4,
        compiler_params=_cparams(("parallel",)),
        name="s5_scan",
    )(u_cm, m, s_re, s_im, r_re, r_im, lr, li)


def _s5_post_kernel(y_ref, u_ref, d_ref, w_ref, b_ref, o_ref):
    y = y_ref[...] + d_ref[...] * u_ref[...].astype(F32)
    c0 = math.sqrt(2.0 / math.pi)
    y = 0.5 * y * (1.0 + jnp.tanh(c0 * (y + 0.044715 * (y * y * y))))
    gate = jnp.dot(y.astype(BF16), w_ref[...], preferred_element_type=F32) + b_ref[...]
    o_ref[...] = (y * _sigmoid(gate)).astype(o_ref.dtype)


def _s5_post(y, proj, d_skip, w_glu, b_glu, *, tm):
    t, w = y.shape
    return pl.pallas_call(
        _s5_post_kernel,
        out_shape=jax.ShapeDtypeStruct((t, w), BF16),
        grid=(t // tm,),
        in_specs=[pl.BlockSpec((tm, w), lambda i: (i, 0)),
                  pl.BlockSpec((tm, w), lambda i: (i, 0)),
                  pl.BlockSpec((1, w), lambda i: (0, 0)),
                  pl.BlockSpec((w, w), lambda i: (0, 0)),
                  pl.BlockSpec((1, w), lambda i: (0, 0))],
        out_specs=pl.BlockSpec((tm, w), lambda i: (i, 0)),
        compiler_params=_cparams(("parallel",)),
        name="s5_post",
    )(y, proj, d_skip.reshape(1, w), w_glu, b_glu.reshape(1, w))


def _lru_kernel(x_ref, cw_ref, cb_ref, wr_ref, br_ref, wi_ref, bi_ref, lam_ref, o_ref,
                xbuf_ref, a_ref, g_ref, h_ref, *, tl, kw):
    pad = 8
    w = x_ref.shape[1]

    @pl.when(pl.program_id(1) == 0)
    def _():
        xbuf_ref[pl.ds(0, pad), :] = jnp.zeros((pad, w), F32)
        h_ref[...] = jnp.zeros_like(h_ref)

    @pl.when(pl.program_id(1) > 0)
    def _():
        xbuf_ref[pl.ds(0, pad), :] = xbuf_ref[pl.ds(tl, pad), :]

    xbuf_ref[pl.ds(pad, tl), :] = x_ref[...].astype(F32)
    cw = cw_ref[...]
    xc = cb_ref[...] + xbuf_ref[pl.ds(pad - (kw - 1), tl), :] * cw[0:1, :]
    for k in range(1, kw):
        xc = xc + xbuf_ref[pl.ds(pad - (kw - 1) + k, tl), :] * cw[k:k + 1, :]
    xcb = xc.astype(BF16)
    nsup = w // LRU_SUPER
    rs, is_ = [], []
    for s in range(nsup):
        xs = xcb[:, s * LRU_SUPER:(s + 1) * LRU_SUPER]
        rs.append(jnp.dot(xs, wr_ref[s], preferred_element_type=F32))
        is_.append(jnp.dot(xs, wi_ref[s], preferred_element_type=F32))
    r = _sigmoid(jnp.concatenate(rs, axis=1) + br_ref[...])
    i = _sigmoid(jnp.concatenate(is_, axis=1) + bi_ref[...])
    log_a = (-LRU_C) * r * _softplus(-lam_ref[...])
    a_ref[...] = jnp.exp(log_a)
    th = jnp.tanh(log_a)
    g_ref[...] = jnp.sqrt((-2.0 * th) / (1.0 - th)) * (i * xc)

    def body(t8, h):
        base = pl.multiple_of(t8 * 8, 8)
        for r_ in range(8):
            h = a_ref[pl.ds(base + r_, 1), :] * h + g_ref[pl.ds(base + r_, 1), :]
            g_ref[pl.ds(base + r_, 1), :] = h
        return h

    h_ref[...] = lax.fori_loop(0, tl // 8, body, h_ref[...])
    o_ref[...] = g_ref[...].astype(o_ref.dtype)


def _block_diag_super(wblk):
    n, k, _ = wblk.shape
    per = LRU_SUPER // k
    wb = wblk.reshape(n // per, per, k, k)
    eye = jnp.eye(per, dtype=wblk.dtype)
    sup = wb[:, :, :, None, :] * eye[None, :, None, :, None]
    return sup.reshape(n // per, LRU_SUPER, LRU_SUPER)


def _lru(proj3, col_block, conv_w, conv_b, w_r, b_r, w_i, b_i, lam, *, tl):
    b, l, _ = proj3.shape
    w = conv_w.shape[1]
    kw = conv_w.shape[0]
    wr = _block_diag_super(w_r).astype(BF16)
    wi = _block_diag_super(w_i).astype(BF16)
    vec = lambda bi, ti: (0, 0)
    full3 = lambda bi, ti: (0, 0, 0)
    return pl.pallas_call(
        functools.partial(_lru_kernel, tl=tl, kw=kw),
        out_shape=jax.ShapeDtypeStruct((b, l, w), BF16),
        grid=(b, l // tl),
        in_specs=[pl.BlockSpec((None, tl, w), lambda bi, ti: (bi, ti, col_block)),
                  pl.BlockSpec((kw, w), vec),
                  pl.BlockSpec((1, w), vec),
                  pl.BlockSpec(wr.shape, full3),
                  pl.BlockSpec((1, w), vec),
                  pl.BlockSpec(wi.shape, full3),
                  pl.BlockSpec((1, w), vec),
                  pl.BlockSpec((1, w), vec)],
        out_specs=pl.BlockSpec((None, tl, w), lambda bi, ti: (bi, ti, 0)),
        scratch_shapes=[pltpu.VMEM((tl + 8, w), F32), pltpu.VMEM((tl, w), F32),
                        pltpu.VMEM((tl, w), F32), pltpu.VMEM((1, w), F32)],
        compiler_params=_cparams(("parallel", "arbitrary")),
        name="rglru",
    )(proj3, conv_w, conv_b.reshape(1, w), wr, b_r.reshape(1, w), wi, b_i.reshape(1, w), lam.reshape(1, w))


def _attn_kernel(q_ref, k_ref, v_ref, o_ref, *, tq, scale):
    qi = pl.program_id(2)
    q = q_ref[...]
    row = lax.broadcasted_iota(jnp.int32, (tq, tq), 0)
    col = lax.broadcasted_iota(jnp.int32, (tq, tq), 1)
    suffix = jnp.where(row > col, 1.0, 0.0).astype(BF16)
    causal = col < row

    def block(j, carry, acc, masked):
        ks = pl.multiple_of(j * tq, tq)
        k = k_ref[pl.ds(ks, tq), :]
        v = v_ref[pl.ds(ks, tq), :]
        z = lax.dot_general(q, k, (((1,), (1,)), ((), ())), preferred_element_type=F32) * scale
        lk = -_softplus(z)
        if masked:
            lk = jnp.where(causal, lk, 0.0)
        hi = lk.astype(BF16)
        lo = (lk - hi.astype(F32)).astype(BF16)
        ls = (jnp.dot(hi, suffix, preferred_element_type=F32)
              + jnp.dot(lo, suffix, preferred_element_type=F32))
        w = jnp.exp(z + lk + ls + carry)
        if masked:
            w = jnp.where(causal, w, 0.0)
        acc = acc + jnp.dot(w.astype(BF16), v, preferred_element_type=F32)
        carry = carry + jnp.sum(lk, axis=-1, keepdims=True)
        return carry, acc

    carry = jnp.zeros((tq, 1), F32)
    acc = jnp.zeros((tq, q.shape[1]), F32)
    carry, acc = block(qi, carry, acc, True)

    def body(jj, c):
        return block(qi - 1 - jj, c[0], c[1], False)

    carry, acc = lax.fori_loop(0, qi, body, (carry, acc))
    o_ref[...] = acc.astype(o_ref.dtype)


def _attention(proj3, q_blk, k_blk, v_blk, n_heads, *, tq):
    b, l, _ = proj3.shape
    d = HEAD_DIM
    return pl.pallas_call(
        functools.partial(_attn_kernel, tq=tq, scale=d ** -0.5),
        out_shape=jax.ShapeDtypeStruct((b, l, n_heads * d), BF16),
        grid=(b, n_heads, l // tq),
        in_specs=[pl.BlockSpec((None, tq, d), lambda bi, hi, qi: (bi, qi, q_blk + hi)),
                  pl.BlockSpec((None, l, d), lambda bi, hi, qi: (bi, 0, k_blk + hi)),
                  pl.BlockSpec((None, l, d), lambda bi, hi, qi: (bi, 0, v_blk + hi))],
        out_specs=pl.BlockSpec((None, tq, d), lambda bi, hi, qi: (bi, qi, hi)),
        compiler_params=_cparams(("parallel", "parallel", "arbitrary")),
        name="stick_attn",
    )(proj3, proj3, proj3)


def _merge_kernel(ya_ref, yb_ref, yc_ref, wa_ref, wb_ref, wc_ref, ga_ref, gb_ref, gc_ref, bg_ref, o_ref):
    bg = bg_ref[...]
    out = None
    for n, (y_ref, w_ref, gl_ref) in enumerate(((ya_ref, wa_ref, ga_ref), (yb_ref, wb_ref, gb_ref),
                                                (yc_ref, wc_ref, gc_ref))):
        gate = _sigmoid(gl_ref[...].astype(F32) + bg[n:n + 1, :])
        term = gate * jnp.dot(y_ref[...], w_ref[...], preferred_element_type=F32)
        out = term if out is None else out + term
    o_ref[...] = out.astype(o_ref.dtype)


def _merge(ys, ws, proj, gate_col, b_gate, *, tm, tn):
    t, w = ys[0].shape
    d = ws[0].shape[1]
    gblk = gate_col // tn
    per = d // tn
    y_spec = pl.BlockSpec((tm, w), lambda i, j: (i, 0))
    w_spec = pl.BlockSpec((w, tn), lambda i, j: (0, j))
    gate_specs = [pl.BlockSpec((tm, tn), functools.partial(lambda i, j, n: (i, gblk + n * per + j), n=n))
                  for n in range(N_BRANCH)]
    return pl.pallas_call(
        _merge_kernel,
        out_shape=jax.ShapeDtypeStruct((t, d), BF16),
        grid=(t // tm, d // tn),
        in_specs=[y_spec] * 3 + [w_spec] * 3 + gate_specs
                 + [pl.BlockSpec((N_BRANCH, tn), lambda i, j: (0, j))],
        out_specs=pl.BlockSpec((tm, tn), lambda i, j: (i, j)),
        compiler_params=_cparams(("parallel", "arbitrary")),
        name="merge",
    )(*ys, *ws, proj, proj, proj, b_gate.reshape(N_BRANCH, d))


def _outproj_kernel(m_ref, w_ref, h_ref, g_ref, ho_ref, hn_ref):
    h = h_ref[...] + jnp.dot(m_ref[...], w_ref[...], preferred_element_type=F32)
    ho_ref[...] = h
    hn_ref[...] = _rms_rows(h, g_ref[...]).astype(hn_ref.dtype)


def _outproj(merged, w_out, h, g, *, tm):
    t, d = h.shape
    row = lambda i: (i, 0)
    return pl.pallas_call(
        _outproj_kernel,
        out_shape=(jax.ShapeDtypeStruct((t, d), F32), jax.ShapeDtypeStruct((t, d), BF16)),
        grid=(t // tm,),
        in_specs=[pl.BlockSpec((tm, d), row),
                  pl.BlockSpec((d, d), lambda i: (0, 0)),
                  pl.BlockSpec((tm, d), row),
                  pl.BlockSpec((1, d), lambda i: (0, 0))],
        out_specs=(pl.BlockSpec((tm, d), row), pl.BlockSpec((tm, d), row)),
        compiler_params=_cparams(("parallel",)),
        name="outproj",
    )(merged, w_out, h, g.reshape(1, d))


def _mlp_kernel(x_ref, wu_ref, wd_ref, h_ref, g_ref, o_ref, acc_ref, *, final_norm):
    f = pl.program_id(1)

    @pl.when(f == 0)
    def _():
        acc_ref[...] = h_ref[...]

    hid = jnp.maximum(jnp.dot(x_ref[...], wu_ref[...], preferred_element_type=F32), 0.0)
    acc_ref[...] += jnp.dot((hid * hid).astype(BF16), wd_ref[...], preferred_element_type=F32)

    @pl.when(f == pl.num_programs(1) - 1)
    def _():
        h = acc_ref[...]
        o_ref[...] = _rms_rows(h, g_ref[...]) if final_norm else h


def _mlp(hn, w_up, w_down, h, g, *, tm, tf, final_norm):
    t, d = h.shape
    ff = w_up.shape[1]
    row = lambda i, f: (i, 0)
    return pl.pallas_call(
        functools.partial(_mlp_kernel, final_norm=final_norm),
        out_shape=jax.ShapeDtypeStruct((t, d), F32),
        grid=(t // tm, ff // tf),
        in_specs=[pl.BlockSpec((tm, d), row),
                  pl.BlockSpec((d, tf), lambda i, f: (0, f)),
                  pl.BlockSpec((tf, d), lambda i, f: (f, 0)),
                  pl.BlockSpec((tm, d), row),
                  pl.BlockSpec((1, d), lambda i, f: (0, 0))],
        out_specs=pl.BlockSpec((tm, d), row),
        scratch_shapes=[pltpu.VMEM((tm, d), F32)],
        compiler_params=_cparams(("parallel", "arbitrary")),
        name="mlp",
    )(hn, w_up, w_down, h, g.reshape(1, d))


def _layer(h, bsz, seq, p, *, final_g):
    t, d = h.shape
    s5_w = p["s5_d"].shape[0]
    lru_w = p["lru_lambda"].shape[0]
    attn_w = p["w_br_attn"].shape[0]
    n_heads = attn_w // HEAD_DIM
    off_lru = s5_w
    off_q = off_lru + lru_w
    off_k = off_q + attn_w
    off_v = off_k + attn_w
    off_gate = off_v + attn_w

    proj = _inproj(h, p["norm_mix_g"], p["w_in"].astype(BF16), tm=1024, tn=1024)
    proj3 = proj.reshape(bsz, seq, proj.shape[1])

    groups = s5_w // S5_GROUP
    nchunk = seq // S5_CHUNK
    u_cm = (proj[:, :s5_w].reshape(bsz, nchunk, S5_CHUNK, groups, S5_GROUP)
            .transpose(3, 1, 0, 2, 4).reshape(groups, nchunk * bsz, S5_CHUNK * S5_GROUP))
    tables = _s5_tables(p["s5_lam_re"], p["s5_lam_im"], p["s5_log_dt"], p["s5_b_re"], p["s5_b_im"],
                        p["s5_c_re"], p["s5_c_im"])
    y_cm = _s5_scan(u_cm, tables, nb=bsz)
    y_tm = (y_cm.reshape(groups, nchunk, bsz, S5_CHUNK, S5_GROUP)
            .transpose(2, 1, 3, 0, 4).reshape(t, s5_w))
    y_s5 = _s5_post(y_tm, proj, p["s5_d"], p["s5_w_glu"].astype(BF16), p["s5_b_glu"], tm=1024)

    y_lru = _lru(proj3, off_lru // lru_w, p["lru_conv_w"], p["lru_conv_b"], p["lru_w_r"], p["lru_b_r"],
                 p["lru_w_i"], p["lru_b_i"], p["lru_lambda"], tl=512).reshape(t, lru_w)

    y_attn = _attention(proj3, off_q // HEAD_DIM, off_k // HEAD_DIM, off_v // HEAD_DIM, n_heads,
                        tq=256).reshape(t, attn_w)

    merged = _merge((y_s5, y_lru, y_attn),
                    (p["w_br_s5"].astype(BF16), p["w_br_lru"].astype(BF16), p["w_br_attn"].astype(BF16)),
                    proj, off_gate, p["b_gate"], tm=1024, tn=512)
    h, hn = _outproj(merged, p["w_out"].astype(BF16), h, p["norm_mlp_g"], tm=256)
    g_last = p["norm_mlp_g"] if final_g is None else final_g
    return _mlp(hn, p["w_up"].astype(BF16), p["w_down"].astype(BF16), h, g_last,
                tm=512, tf=512, final_norm=final_g is not None)


_LAYER_PARAMS = ("norm_mix_g", "w_in", "b_gate", "s5_lam_re", "s5_lam_im", "s5_log_dt", "s5_b_re", "s5_b_im",
                 "s5_c_re", "s5_c_im", "s5_d", "s5_w_glu", "s5_b_glu", "lru_conv_w", "lru_conv_b", "lru_w_r",
                 "lru_b_r", "lru_w_i", "lru_b_i", "lru_lambda", "w_br_s5", "w_br_lru", "w_br_attn", "w_out",
                 "norm_mlp_g", "w_up", "w_down")


def kernel(x, norm_mix_g, w_in, b_gate, s5_lam_re, s5_lam_im, s5_log_dt, s5_b_re, s5_b_im, s5_c_re, s5_c_im,
           s5_d, s5_w_glu, s5_b_glu, lru_conv_w, lru_conv_b, lru_w_r, lru_b_r, lru_w_i, lru_b_i, lru_lambda,
           w_br_s5, w_br_lru, w_br_attn, w_out, norm_mlp_g, w_up, w_down, final_norm_g):
    stacked = dict(zip(_LAYER_PARAMS, (norm_mix_g, w_in, b_gate, s5_lam_re, s5_lam_im, s5_log_dt, s5_b_re,
                                       s5_b_im, s5_c_re, s5_c_im, s5_d, s5_w_glu, s5_b_glu, lru_conv_w,
                                       lru_conv_b, lru_w_r, lru_b_r, lru_w_i, lru_b_i, lru_lambda, w_br_s5,
                                       w_br_lru, w_br_attn, w_out, norm_mlp_g, w_up, w_down)))
    bsz, seq, d = x.shape
    depth = w_in.shape[0]
    h = x.reshape(bsz * seq, d).astype(F32)
    for layer in range(depth):
        p = {k: v[layer] for k, v in stacked.items()}
        h = _layer(h, bsz, seq, p, final_g=final_norm_g if layer == depth - 1 else None)
    return h.reshape(bsz, seq, d).astype(x.dtype)
```

```python
import functools
import math

import jax
import jax.numpy as jnp
from jax import lax
from jax.experimental import pallas as pl
from jax.experimental.pallas import tpu as pltpu

F32 = jnp.float32
BF16 = jnp.bfloat16

EPS = 1e-6
HEAD_DIM = 128
S5_GROUP = 16
S5_CHUNK = 16
LRU_BLOCK = 64
LRU_SUPER = 256
LRU_C = 8.0
N_BRANCH = 3
VMEM_LIMIT = 56 * 1024 * 1024
HIGHEST = lax.Precision.HIGHEST


def _cparams(sem):
    return pltpu.CompilerParams(dimension_semantics=sem, vmem_limit_bytes=VMEM_LIMIT)


def _softplus(z):
    return jnp.maximum(z, 0.0) + jnp.log(1.0 + jnp.exp(-jnp.abs(z)))


def _sigmoid(z):
    return 1.0 / (1.0 + jnp.exp(-z))


def _rms_rows(x, g):
    ms = jnp.mean(x * x, axis=-1, keepdims=True)
    return (x * lax.rsqrt(ms + EPS)) * g


def _inproj_kernel(x_ref, g_ref, w_ref, o_ref, xn_ref, *, row_chunk):
    @pl.when(pl.program_id(1) == 0)
    def _():
        g = g_ref[...]

        def body(c, carry):
            r = pl.multiple_of(c * row_chunk, row_chunk)
            xn_ref[pl.ds(r, row_chunk), :] = _rms_rows(x_ref[pl.ds(r, row_chunk), :], g).astype(BF16)
            return carry

        lax.fori_loop(0, x_ref.shape[0] // row_chunk, body, 0)

    o_ref[...] = jnp.dot(xn_ref[...], w_ref[...], preferred_element_type=F32).astype(o_ref.dtype)


def _inproj(h, g, w, *, tm, tn):
    t, d = h.shape
    n = w.shape[1]
    return pl.pallas_call(
        functools.partial(_inproj_kernel, row_chunk=32),
        out_shape=jax.ShapeDtypeStruct((t, n), BF16),
        grid=(t // tm, n // tn),
        in_specs=[pl.BlockSpec((tm, d), lambda i, j: (i, 0)),
                  pl.BlockSpec((1, d), lambda i, j: (0, 0)),
                  pl.BlockSpec((d, tn), lambda i, j: (0, j))],
        out_specs=pl.BlockSpec((tm, tn), lambda i, j: (i, j)),
        scratch_shapes=[pltpu.VMEM((tm, d), BF16)],
        compiler_params=_cparams(("parallel", "arbitrary")),
        name="inproj",
    )(h, g.reshape(1, d), w)


def _s5_kernel(u_ref, m_ref, sre_ref, sim_ref, rre_ref, rim_ref, lr_ref, li_ref, o_ref,
               xre_ref, xim_ref, hre_ref, him_ref, *, nb):
    u = u_ref[...]
    xre_ref[...] = jnp.dot(u, sre_ref[...], preferred_element_type=F32)
    xim_ref[...] = jnp.dot(u, sim_ref[...], preferred_element_type=F32)
    lr = lr_ref[...]
    li = li_ref[...]
    rows = u.shape[0]
    per_tile = 8 // nb

    def body(k, carry):
        hr, hi = carry
        r = pl.multiple_of(k * 8, 8)
        xr = xre_ref[pl.ds(r, 8), :]
        xi = xim_ref[pl.ds(r, 8), :]
        srs, sis = [], []
        for c in range(per_tile):
            srs.append(hr)
            sis.append(hi)
            nr = lr * hr - li * hi + xr[c * nb:(c + 1) * nb, :]
            ni = lr * hi + li * hr + xi[c * nb:(c + 1) * nb, :]
            hr, hi = nr, ni
        hre_ref[pl.ds(r, 8), :] = jnp.concatenate(srs, axis=0)
        him_ref[pl.ds(r, 8), :] = jnp.concatenate(sis, axis=0)
        return hr, hi

    p = lr.shape[1]
    zero = jnp.zeros((nb, p), F32)
    lax.fori_loop(0, rows // 8, body, (zero, zero))

    y = jnp.dot(u, m_ref[...], preferred_element_type=F32)
    y = y + jnp.dot(hre_ref[...].astype(BF16), rre_ref[...], preferred_element_type=F32)
    y = y + jnp.dot(him_ref[...].astype(BF16), rim_ref[...], preferred_element_type=F32)
    o_ref[...] = y


def _s5_tables(lam_re, lam_im, log_dt, b_re, b_im, c_re, c_im):
    tc = S5_CHUNK
    g, p = lam_re.shape
    c = b_re.shape[-1]
    lr = lam_re.astype(F32)
    li = lam_im.astype(F32)
    dt = jnp.exp(log_dt.astype(F32))[:, None]
    ar, ai = lr * dt, li * dt
    ks = jnp.arange(tc + 1, dtype=F32)[None, :, None]
    mag = jnp.exp(ks * ar[:, None, :])
    pr = mag * jnp.cos(ks * ai[:, None, :])
    pi = mag * jnp.sin(ks * ai[:, None, :])
    nr = jnp.expm1(ar) * jnp.cos(ai) - 2.0 * jnp.sin(0.5 * ai) ** 2
    ni = jnp.exp(ar) * jnp.sin(ai)
    den = lr * lr + li * li
    fr = (nr * lr + ni * li) / den
    fi = (ni * lr - nr * li) / den
    bbr = fr[..., None] * b_re - fi[..., None] * b_im
    bbi = fr[..., None] * b_im + fi[..., None] * b_re
    er = pr[..., None] * bbr[:, None] - pi[..., None] * bbi[:, None]
    ei = pr[..., None] * bbi[:, None] + pi[..., None] * bbr[:, None]
    kk = (jnp.einsum('gop,gkpi->gkoi', c_re, er[:, :tc], precision=HIGHEST)
          - jnp.einsum('gop,gkpi->gkoi', c_im, ei[:, :tc], precision=HIGHEST))
    jj = jnp.arange(tc)[:, None]
    ii = jnp.arange(tc)[None, :]
    lag = jnp.clip(ii - jj, 0, tc - 1)
    m = jnp.where((ii >= jj)[None, :, :, None, None], kk[:, lag], 0.0)
    m = m.transpose(0, 1, 4, 2, 3).reshape(g, tc * c, tc * c)
    rev = (tc - 1) - jnp.arange(tc)
    s_re = er[:, rev].transpose(0, 1, 3, 2).reshape(g, tc * c, p)
    s_im = ei[:, rev].transpose(0, 1, 3, 2).reshape(g, tc * c, p)
    pr1, pi1 = pr[:, 1:, None, :], pi[:, 1:, None, :]
    qr = c_re[:, None] * pr1 - c_im[:, None] * pi1
    qi = c_re[:, None] * pi1 + c_im[:, None] * pr1
    r_re = qr.transpose(0, 3, 1, 2).reshape(g, p, tc * c)
    r_im = (-qi).transpose(0, 3, 1, 2).reshape(g, p, tc * c)
    return (m.astype(BF16), s_re.astype(BF16), s_im.astype(BF16), r_re.astype(BF16), r_im.astype(BF16),
            pr[:, tc].reshape(g, 1, p), pi[:, tc].reshape(g, 1, p))


def _s5_scan(u_cm, tables, *, nb):
    g, rows, w = u_cm.shape
    m, s_re, s_im, r_re, r_im, lr, li = tables
    p = lr.shape[-1]
    grp = lambda i: (i, 0, 0)
    return pl.pallas_call(
        functools.partial(_s5_kernel, nb=nb),
        out_shape=jax.ShapeDtypeStruct((g, rows, w), F32),
        grid=(g,),
        in_specs=[pl.BlockSpec((None, rows, w), grp),
                  pl.BlockSpec((None, w, w), grp),
                  pl.BlockSpec((None, w, p), grp),
                  pl.BlockSpec((None, w, p), grp),
                  pl.BlockSpec((None, p, w), grp),
                  pl.BlockSpec((None, p, w), grp),
                  pl.BlockSpec((None, 1, p), grp),
                  pl.BlockSpec((None, 1, p), grp)],
        out_specs=pl.BlockSpec((None, rows, w), grp),
        scratch_shapes=[pltpu.VMEM((rows, p), F32)] * 4,
        compiler_params=_cparams(("parallel",)),
        name="s5_scan",
    )(u_cm, m, s_re, s_im, r_re, r_im, lr, li)


def _s5_post_kernel(y_ref, u_ref, d_ref, w_ref, b_ref, o_ref):
    y = y_ref[...] + d_ref[...] * u_ref[...].astype(F32)
    c0 = math.sqrt(2.0 / math.pi)
    y = 0.5 * y * (1.0 + jnp.tanh(c0 * (y + 0.044715 * (y * y * y))))
    gate = jnp.dot(y.astype(BF16), w_ref[...], preferred_element_type=F32) + b_ref[...]
    o_ref[...] = (y * _sigmoid(gate)).astype(o_ref.dtype)


def _s5_post(y, proj, d_skip, w_glu, b_glu, *, tm):
    t, w = y.shape
    return pl.pallas_call(
        _s5_post_kernel,
        out_shape=jax.ShapeDtypeStruct((t, w), BF16),
        grid=(t // tm,),
        in_specs=[pl.BlockSpec((tm, w), lambda i: (i, 0)),
                  pl.BlockSpec((tm, w), lambda i: (i, 0)),
                  pl.BlockSpec((1, w), lambda i: (0, 0)),
                  pl.BlockSpec((w, w), lambda i: (0, 0)),
                  pl.BlockSpec((1, w), lambda i: (0, 0))],
        out_specs=pl.BlockSpec((tm, w), lambda i: (i, 0)),
        compiler_params=_cparams(("parallel",)),
        name="s5_post",
    )(y, proj, d_skip.reshape(1, w), w_glu, b_glu.reshape(1, w))


def _lru_kernel(x_ref, cw_ref, cb_ref, wr_ref, br_ref, wi_ref, bi_ref, lam_ref, o_ref,
                xbuf_ref, a_ref, g_ref, h_ref, *, tl, kw):
    pad = 8
    w = x_ref.shape[1]

    @pl.when(pl.program_id(1) == 0)
    def _():
        xbuf_ref[pl.ds(0, pad), :] = jnp.zeros((pad, w), F32)
        h_ref[...] = jnp.zeros_like(h_ref)

    @pl.when(pl.program_id(1) > 0)
    def _():
        xbuf_ref[pl.ds(0, pad), :] = xbuf_ref[pl.ds(tl, pad), :]

    xbuf_ref[pl.ds(pad, tl), :] = x_ref[...].astype(F32)
    cw = cw_ref[...]
    xc = cb_ref[...] + xbuf_ref[pl.ds(pad - (kw - 1), tl), :] * cw[0:1, :]
    for k in range(1, kw):
        xc = xc + xbuf_ref[pl.ds(pad - (kw - 1) + k, tl), :] * cw[k:k + 1, :]
    xcb = xc.astype(BF16)
    nsup = w // LRU_SUPER
    rs, is_ = [], []
    for s in range(nsup):
        xs = xcb[:, s * LRU_SUPER:(s + 1) * LRU_SUPER]
        rs.append(jnp.dot(xs, wr_ref[s], preferred_element_type=F32))
        is_.append(jnp.dot(xs, wi_ref[s], preferred_element_type=F32))
    r = _sigmoid(jnp.concatenate(rs, axis=1) + br_ref[...])
    i = _sigmoid(jnp.concatenate(is_, axis=1) + bi_ref[...])
    log_a = (-LRU_C) * r * _softplus(-lam_ref[...])
    a_ref[...] = jnp.exp(log_a)
    th = jnp.tanh(log_a)
    g_ref[...] = jnp.sqrt((-2.0 * th) / (1.0 - th)) * (i * xc)

    def body(t8, h):
        base = pl.multiple_of(t8 * 8, 8)
        for r_ in range(8):
            h = a_ref[pl.ds(base + r_, 1), :] * h + g_ref[pl.ds(base + r_, 1), :]
            g_ref[pl.ds(base + r_, 1), :] = h
        return h

    h_ref[...] = lax.fori_loop(0, tl // 8, body, h_ref[...])
    o_ref[...] = g_ref[...].astype(o_ref.dtype)


def _block_diag_super(wblk):
    n, k, _ = wblk.shape
    per = LRU_SUPER // k
    wb = wblk.reshape(n // per, per, k, k)
    eye = jnp.eye(per, dtype=wblk.dtype)
    sup = wb[:, :, :, None, :] * eye[None, :, None, :, None]
    return sup.reshape(n // per, LRU_SUPER, LRU_SUPER)


def _lru(proj3, col_block, conv_w, conv_b, w_r, b_r, w_i, b_i, lam, *, tl):
    b, l, _ = proj3.shape
    w = conv_w.shape[1]
    kw = conv_w.shape[0]
    wr = _block_diag_super(w_r).astype(BF16)
    wi = _block_diag_super(w_i).astype(BF16)
    vec = lambda bi, ti: (0, 0)
    full3 = lambda bi, ti: (0, 0, 0)
    return pl.pallas_call(
        functools.partial(_lru_kernel, tl=tl, kw=kw),
        out_shape=jax.ShapeDtypeStruct((b, l, w), BF16),
        grid=(b, l // tl),
        in_specs=[pl.BlockSpec((None, tl, w), lambda bi, ti: (bi, ti, col_block)),
                  pl.BlockSpec((kw, w), vec),
                  pl.BlockSpec((1, w), vec),
                  pl.BlockSpec(wr.shape, full3),
                  pl.BlockSpec((1, w), vec),
                  pl.BlockSpec(wi.shape, full3),
                  pl.BlockSpec((1, w), vec),
                  pl.BlockSpec((1, w), vec)],
        out_specs=pl.BlockSpec((None, tl, w), lambda bi, ti: (bi, ti, 0)),
        scratch_shapes=[pltpu.VMEM((tl + 8, w), F32), pltpu.VMEM((tl, w), F32),
                        pltpu.VMEM((tl, w), F32), pltpu.VMEM((1, w), F32)],
        compiler_params=_cparams(("parallel", "arbitrary")),
        name="rglru",
    )(proj3, conv_w, conv_b.reshape(1, w), wr, b_r.reshape(1, w), wi, b_i.reshape(1, w), lam.reshape(1, w))


def _attn_kernel(q_ref, k_ref, v_ref, o_ref, *, tq, hp):
    qi = pl.program_id(2)
    d = HEAD_DIM
    row = lax.broadcasted_iota(jnp.int32, (tq, tq), 0)
    col = lax.broadcasted_iota(jnp.int32, (tq, tq), 1)
    suffix = jnp.where(row > col, 1.0, 0.0).astype(BF16)
    causal = col < row
    qs = [q_ref[:, h * d:(h + 1) * d] for h in range(hp)]

    def blocks(j, state, masked):
        ks = pl.multiple_of(j * tq, tq)
        heads = range(hp)
        zs = [lax.dot_general(qs[h], k_ref[pl.ds(ks, tq), h * d:(h + 1) * d], (((1,), (1,)), ((), ())),
                              preferred_element_type=F32) for h in heads]
        lks = []
        for h in heads:
            lk = -_softplus(zs[h])
            lks.append(jnp.where(causal, lk, 0.0) if masked else lk)
        lss = []
        for h in heads:
            hi = lks[h].astype(BF16)
            lo = (lks[h] - hi.astype(F32)).astype(BF16)
            lss.append(jnp.dot(hi, suffix, preferred_element_type=F32)
                       + jnp.dot(lo, suffix, preferred_element_type=F32))
        out = []
        for h in heads:
            carry, acc = state[2 * h], state[2 * h + 1]
            w = jnp.exp(zs[h] + lks[h] + lss[h] + carry)
            if masked:
                w = jnp.where(causal, w, 0.0)
            acc = acc + jnp.dot(w.astype(BF16), v_ref[pl.ds(ks, tq), h * d:(h + 1) * d],
                                preferred_element_type=F32)
            out.extend((carry + jnp.sum(lks[h], axis=-1, keepdims=True), acc))
        return tuple(out)

    state = blocks(qi, (jnp.zeros((tq, 1), F32), jnp.zeros((tq, d), F32)) * hp, True)
    state = lax.fori_loop(0, qi, lambda jj, c: blocks(qi - 1 - jj, c, False), state)
    for h in range(hp):
        o_ref[:, h * d:(h + 1) * d] = state[2 * h + 1].astype(o_ref.dtype)


def _attention(proj3, q_blk, k_blk, v_blk, n_heads, *, tq, hp):
    b, l, _ = proj3.shape
    d = HEAD_DIM
    return pl.pallas_call(
        functools.partial(_attn_kernel, tq=tq, hp=hp),
        out_shape=jax.ShapeDtypeStruct((b, l, n_heads * d), BF16),
        grid=(b, n_heads // hp, l // tq),
        in_specs=[pl.BlockSpec((None, tq, hp * d), lambda bi, hi, qi: (bi, qi, q_blk // hp + hi)),
                  pl.BlockSpec((None, l, hp * d), lambda bi, hi, qi: (bi, 0, k_blk // hp + hi)),
                  pl.BlockSpec((None, l, hp * d), lambda bi, hi, qi: (bi, 0, v_blk // hp + hi))],
        out_specs=pl.BlockSpec((None, tq, hp * d), lambda bi, hi, qi: (bi, qi, hi)),
        compiler_params=_cparams(("parallel", "parallel", "arbitrary")),
        name="stick_attn",
    )(proj3, proj3, proj3)


def _merge_kernel(ya_ref, yb_ref, yc_ref, wa_ref, wb_ref, wc_ref, ga_ref, gb_ref, gc_ref, bg_ref, o_ref):
    bg = bg_ref[...]
    out = None
    for n, (y_ref, w_ref, gl_ref) in enumerate(((ya_ref, wa_ref, ga_ref), (yb_ref, wb_ref, gb_ref),
                                                (yc_ref, wc_ref, gc_ref))):
        gate = _sigmoid(gl_ref[...].astype(F32) + bg[n:n + 1, :])
        term = gate * jnp.dot(y_ref[...], w_ref[...], preferred_element_type=F32)
        out = term if out is None else out + term
    o_ref[...] = out.astype(o_ref.dtype)


def _merge(ys, ws, proj, gate_col, b_gate, *, tm, tn):
    t, w = ys[0].shape
    d = ws[0].shape[1]
    gblk = gate_col // tn
    per = d // tn
    y_spec = pl.BlockSpec((tm, w), lambda i, j: (i, 0))
    w_spec = pl.BlockSpec((w, tn), lambda i, j: (0, j))
    gate_specs = [pl.BlockSpec((tm, tn), functools.partial(lambda i, j, n: (i, gblk + n * per + j), n=n))
                  for n in range(N_BRANCH)]
    return pl.pallas_call(
        _merge_kernel,
        out_shape=jax.ShapeDtypeStruct((t, d), BF16),
        grid=(t // tm, d // tn),
        in_specs=[y_spec] * 3 + [w_spec] * 3 + gate_specs
                 + [pl.BlockSpec((N_BRANCH, tn), lambda i, j: (0, j))],
        out_specs=pl.BlockSpec((tm, tn), lambda i, j: (i, j)),
        compiler_params=_cparams(("parallel", "arbitrary")),
        name="merge",
    )(*ys, *ws, proj, proj, proj, b_gate.reshape(N_BRANCH, d))


def _outproj_kernel(m_ref, w_ref, h_ref, g_ref, ho_ref, hn_ref):
    h = h_ref[...] + jnp.dot(m_ref[...], w_ref[...], preferred_element_type=F32)
    ho_ref[...] = h
    hn_ref[...] = _rms_rows(h, g_ref[...]).astype(hn_ref.dtype)


def _outproj(merged, w_out, h, g, *, tm):
    t, d = h.shape
    row = lambda i: (i, 0)
    return pl.pallas_call(
        _outproj_kernel,
        out_shape=(jax.ShapeDtypeStruct((t, d), F32), jax.ShapeDtypeStruct((t, d), BF16)),
        grid=(t // tm,),
        in_specs=[pl.BlockSpec((tm, d), row),
                  pl.BlockSpec((d, d), lambda i: (0, 0)),
                  pl.BlockSpec((tm, d), row),
                  pl.BlockSpec((1, d), lambda i: (0, 0))],
        out_specs=(pl.BlockSpec((tm, d), row), pl.BlockSpec((tm, d), row)),
        compiler_params=_cparams(("parallel",)),
        name="outproj",
    )(merged, w_out, h, g.reshape(1, d))


def _mlp_kernel(x_ref, wu_ref, wd_ref, h_ref, g_ref, o_ref, acc_ref, *, final_norm):
    f = pl.program_id(1)

    @pl.when(f == 0)
    def _():
        acc_ref[...] = h_ref[...]

    hid = jnp.maximum(jnp.dot(x_ref[...], wu_ref[...], preferred_element_type=F32), 0.0)
    acc_ref[...] += jnp.dot((hid * hid).astype(BF16), wd_ref[...], preferred_element_type=F32)

    @pl.when(f == pl.num_programs(1) - 1)
    def _():
        h = acc_ref[...]
        o_ref[...] = _rms_rows(h, g_ref[...]) if final_norm else h


def _mlp(hn, w_up, w_down, h, g, *, tm, tf, final_norm):
    t, d = h.shape
    ff = w_up.shape[1]
    row = lambda i, f: (i, 0)
    return pl.pallas_call(
        functools.partial(_mlp_kernel, final_norm=final_norm),
        out_shape=jax.ShapeDtypeStruct((t, d), F32),
        grid=(t // tm, ff // tf),
        in_specs=[pl.BlockSpec((tm, d), row),
                  pl.BlockSpec((d, tf), lambda i, f: (0, f)),
                  pl.BlockSpec((tf, d), lambda i, f: (f, 0)),
                  pl.BlockSpec((tm, d), row),
                  pl.BlockSpec((1, d), lambda i, f: (0, 0))],
        out_specs=pl.BlockSpec((tm, d), row),
        scratch_shapes=[pltpu.VMEM((tm, d), F32)],
        compiler_params=_cparams(("parallel", "arbitrary")),
        name="mlp",
    )(hn, w_up, w_down, h, g.reshape(1, d))


def _layer(h, bsz, seq, p, *, final_g):
    t, d = h.shape
    s5_w = p["s5_d"].shape[0]
    lru_w = p["lru_lambda"].shape[0]
    attn_w = p["w_br_attn"].shape[0]
    n_heads = attn_w // HEAD_DIM
    off_lru = s5_w
    off_q = off_lru + lru_w
    off_k = off_q + attn_w
    off_v = off_k + attn_w
    off_gate = off_v + attn_w

    col = jnp.arange(p["w_in"].shape[1])
    q_scale = jnp.where((col >= off_q) & (col < off_k), HEAD_DIM ** -0.5, 1.0).astype(F32)
    proj = _inproj(h, p["norm_mix_g"], (p["w_in"] * q_scale[None, :]).astype(BF16), tm=1024, tn=1024)
    proj3 = proj.reshape(bsz, seq, proj.shape[1])

    groups = s5_w // S5_GROUP
    nchunk = seq // S5_CHUNK
    u_cm = (proj[:, :s5_w].reshape(bsz, nchunk, S5_CHUNK, groups, S5_GROUP)
            .transpose(3, 1, 0, 2, 4).reshape(groups, nchunk * bsz, S5_CHUNK * S5_GROUP))
    tables = _s5_tables(p["s5_lam_re"], p["s5_lam_im"], p["s5_log_dt"], p["s5_b_re"], p["s5_b_im"],
                        p["s5_c_re"], p["s5_c_im"])
    y_cm = _s5_scan(u_cm, tables, nb=bsz)
    y_tm = (y_cm.reshape(groups, nchunk, bsz, S5_CHUNK, S5_GROUP)
            .transpose(2, 1, 3, 0, 4).reshape(t, s5_w))
    y_s5 = _s5_post(y_tm, proj, p["s5_d"], p["s5_w_glu"].astype(BF16), p["s5_b_glu"], tm=1024)

    y_lru = _lru(proj3, off_lru // lru_w, p["lru_conv_w"], p["lru_conv_b"], p["lru_w_r"], p["lru_b_r"],
                 p["lru_w_i"], p["lru_b_i"], p["lru_lambda"], tl=512).reshape(t, lru_w)

    y_attn = _attention(proj3, off_q // HEAD_DIM, off_k // HEAD_DIM, off_v // HEAD_DIM, n_heads,
                        tq=256, hp=4).reshape(t, attn_w)

    merged = _merge((y_s5, y_lru, y_attn),
                    (p["w_br_s5"].astype(BF16), p["w_br_lru"].astype(BF16), p["w_br_attn"].astype(BF16)),
                    proj, off_gate, p["b_gate"], tm=1024, tn=512)
    h, hn = _outproj(merged, p["w_out"].astype(BF16), h, p["norm_mlp_g"], tm=256)
    g_last = p["norm_mlp_g"] if final_g is None else final_g
    return _mlp(hn, p["w_up"].astype(BF16), p["w_down"].astype(BF16), h, g_last,
                tm=512, tf=512, final_norm=final_g is not None)


_LAYER_PARAMS = ("norm_mix_g", "w_in", "b_gate", "s5_lam_re", "s5_lam_im", "s5_log_dt", "s5_b_re", "s5_b_im",
                 "s5_c_re", "s5_c_im", "s5_d", "s5_w_glu", "s5_b_glu", "lru_conv_w", "lru_conv_b", "lru_w_r",
                 "lru_b_r", "lru_w_i", "lru_b_i", "lru_lambda", "w_br_s5", "w_br_lru", "w_br_attn", "w_out",
                 "norm_mlp_g", "w_up", "w_down")


def kernel(x, norm_mix_g, w_in, b_gate, s5_lam_re, s5_lam_im, s5_log_dt, s5_b_re, s5_b_im, s5_c_re, s5_c_im,
           s5_d, s5_w_glu, s5_b_glu, lru_conv_w, lru_conv_b, lru_w_r, lru_b_r, lru_w_i, lru_b_i, lru_lambda,
           w_br_s5, w_br_lru, w_br_attn, w_out, norm_mlp_g, w_up, w_down, final_norm_g):
    stacked = dict(zip(_LAYER_PARAMS, (norm_mix_g, w_in, b_gate, s5_lam_re, s5_lam_im, s5_log_dt, s5_b_re,
                                       s5_b_im, s5_c_re, s5_c_im, s5_d, s5_w_glu, s5_b_glu, lru_conv_w,
                                       lru_conv_b, lru_w_r, lru_b_r, lru_w_i, lru_b_i, lru_lambda, w_br_s5,
                                       w_br_lru, w_br_attn, w_out, norm_mlp_g, w_up, w_down)))
    bsz, seq, d = x.shape
    depth = w_in.shape[0]
    h = x.reshape(bsz * seq, d).astype(F32)
    for layer in range(depth):
        p = {k: v[layer] for k, v in stacked.items()}
        h = _layer(h, bsz, seq, p, final_g=final_norm_g if layer == depth - 1 else None)
    return h.reshape(bsz, seq, d).astype(x.dtype)
```

```python
import functools
import math

import jax
import jax.numpy as jnp
from jax import lax
from jax.experimental import pallas as pl
from jax.experimental.pallas import tpu as pltpu

F32 = jnp.float32
BF16 = jnp.bfloat16

EPS = 1e-6
LANES = 128
HEAD_DIM = 128
S5_GROUP = 16
S5_CHUNK = 16
LRU_BLOCK = 64
LRU_SUPER = 256
LRU_C = 8.0
N_BRANCH = 3
LOG2E = 1.4426950408889634
VMEM_LIMIT = 56 * 1024 * 1024
HIGHEST = lax.Precision.HIGHEST


def _cparams(sem):
    return pltpu.CompilerParams(dimension_semantics=sem, vmem_limit_bytes=VMEM_LIMIT)


def _softplus(z):
    return jnp.maximum(z, 0.0) + jnp.log(1.0 + jnp.exp2(jnp.abs(z) * (-LOG2E)))


def _sigmoid(z):
    return 1.0 / (1.0 + jnp.exp(-z))


def _rms_rows(x, g):
    ms = jnp.mean(x * x, axis=-1, keepdims=True)
    return (x * lax.rsqrt(ms + EPS)) * g


def _inproj_kernel(x_ref, g_ref, w_ref, o_ref, o32_ref, xn_ref, *, row_chunk):
    j = pl.program_id(1)

    @pl.when(j == 0)
    def _():
        g = g_ref[...]

        def body(c, carry):
            r = pl.multiple_of(c * row_chunk, row_chunk)
            xn_ref[pl.ds(r, row_chunk), :] = _rms_rows(x_ref[pl.ds(r, row_chunk), :], g).astype(BF16)
            return carry

        lax.fori_loop(0, x_ref.shape[0] // row_chunk, body, 0)

    acc = jnp.dot(xn_ref[...], w_ref[...], preferred_element_type=F32)
    o_ref[...] = acc.astype(o_ref.dtype)

    @pl.when(j == 0)
    def _():
        o32_ref[...] = acc


def _inproj(h, g, w, *, tm, tn):
    t, d = h.shape
    n = w.shape[1]
    return pl.pallas_call(
        functools.partial(_inproj_kernel, row_chunk=32),
        out_shape=(jax.ShapeDtypeStruct((t, n), BF16), jax.ShapeDtypeStruct((t, tn), F32)),
        grid=(t // tm, n // tn),
        in_specs=[pl.BlockSpec((tm, d), lambda i, j: (i, 0)),
                  pl.BlockSpec((1, d), lambda i, j: (0, 0)),
                  pl.BlockSpec((d, tn), lambda i, j: (0, j))],
        out_specs=(pl.BlockSpec((tm, tn), lambda i, j: (i, j)),
                   pl.BlockSpec((tm, tn), lambda i, j: (i, 0))),
        scratch_shapes=[pltpu.VMEM((tm, d), BF16)],
        compiler_params=_cparams(("parallel", "arbitrary")),
        name="inproj",
    )(h, g.reshape(1, d), w)


def _s5_kernel(u_ref, g2_ref, wsr_ref, wsi_ref, wrr_ref, wri_ref, lr_ref, li_ref, o_ref,
               xre_ref, xim_ref, hre_ref, him_ref, *, tc):
    rows = u_ref.shape[0] // tc
    lanes = u_ref.shape[1]
    xc = jnp.concatenate([u_ref[pl.ds(t, rows, stride=tc), :].astype(BF16) for t in range(tc)], axis=1)
    xre_ref[...] = jnp.dot(xc, wsr_ref[...], preferred_element_type=F32)
    xim_ref[...] = jnp.dot(xc, wsi_ref[...], preferred_element_type=F32)
    lr = lr_ref[...]
    li = li_ref[...]

    def body(k, carry):
        hr, hi = carry
        base = pl.multiple_of(k * 8, 8)
        for r in range(8):
            hre_ref[pl.ds(base + r, 1), :] = hr
            him_ref[pl.ds(base + r, 1), :] = hi
            xr = xre_ref[pl.ds(base + r, 1), :]
            xi = xim_ref[pl.ds(base + r, 1), :]
            hr, hi = lr * hr - li * hi + xr, lr * hi + li * hr + xi
        return hr, hi

    zero = jnp.zeros(lr.shape, F32)
    lax.fori_loop(0, rows // 8, body, (zero, zero))

    yr = (jnp.dot(hre_ref[...].astype(BF16), wrr_ref[...], preferred_element_type=F32)
          + jnp.dot(him_ref[...].astype(BF16), wri_ref[...], preferred_element_type=F32))
    nblk = g2_ref.shape[0] // lanes
    for m in range(tc // 2):
        kdim = (2 * m + 2) * lanes
        y = jnp.dot(xc[:, :kdim], g2_ref[(nblk * lanes - kdim):, :], preferred_element_type=F32)
        y = y + yr[:, 2 * m * lanes:(2 * m + 2) * lanes]
        o_ref[pl.ds(2 * m, rows, stride=tc), :] = y[:, :lanes]
        o_ref[pl.ds(2 * m + 1, rows, stride=tc), :] = y[:, lanes:]


def _s5_tables(lam_re, lam_im, log_dt, b_re, b_im, c_re, c_im):
    tc = S5_CHUNK
    g, p = lam_re.shape
    c = b_re.shape[-1]
    lr = lam_re.astype(F32)
    li = lam_im.astype(F32)
    dt = jnp.exp(log_dt.astype(F32))[:, None]
    ar, ai = lr * dt, li * dt
    ks = jnp.arange(tc + 1, dtype=F32)[None, :, None]
    mag = jnp.exp(ks * ar[:, None, :])
    pr = mag * jnp.cos(ks * ai[:, None, :])
    pi = mag * jnp.sin(ks * ai[:, None, :])
    nr = jnp.expm1(ar) * jnp.cos(ai) - 2.0 * jnp.sin(0.5 * ai) ** 2
    ni = jnp.exp(ar) * jnp.sin(ai)
    den = lr * lr + li * li
    fr = (nr * lr + ni * li) / den
    fi = (ni * lr - nr * li) / den
    bbr = fr[..., None] * b_re - fi[..., None] * b_im
    bbi = fr[..., None] * b_im + fi[..., None] * b_re
    er = pr[..., None] * bbr[:, None] - pi[..., None] * bbi[:, None]
    ei = pr[..., None] * bbi[:, None] + pi[..., None] * bbr[:, None]
    kk = (jnp.einsum('gop,gkpi->gkoi', c_re, er[:, :tc], precision=HIGHEST)
          - jnp.einsum('gop,gkpi->gkoi', c_im, ei[:, :tc], precision=HIGHEST))
    gt = LANES // c
    nt = g // gt
    eye = jnp.eye(gt, dtype=F32)

    def tile_diag(x, rows_sub, cols_sub):
        a = x.shape[1]
        xt = x.reshape(nt, gt, a, rows_sub, cols_sub)
        out = xt[:, :, :, :, None, :] * eye[None, :, None, None, :, None]
        return out.transpose(0, 2, 1, 3, 4, 5).reshape(nt, a, gt * rows_sub, gt * cols_sub)

    bd = tile_diag(kk.transpose(0, 1, 3, 2), c, c)
    zblk = jnp.zeros_like(bd[:, :1])
    left = jnp.concatenate([bd[:, ::-1], zblk], axis=1)
    right = jnp.concatenate([zblk, bd[:, ::-1]], axis=1)
    g2 = jnp.concatenate([left, right], axis=-1).reshape(nt, (tc + 1) * LANES, 2 * LANES)
    rev = (tc - 1) - jnp.arange(tc)
    ws_re = tile_diag(er[:, rev].transpose(0, 1, 3, 2), c, p).reshape(nt, tc * LANES, gt * p)
    ws_im = tile_diag(ei[:, rev].transpose(0, 1, 3, 2), c, p).reshape(nt, tc * LANES, gt * p)
    pr1, pi1 = pr[:, 1:, None, :], pi[:, 1:, None, :]
    qr = c_re[:, None] * pr1 - c_im[:, None] * pi1
    qi = c_re[:, None] * pi1 + c_im[:, None] * pr1
    wr_re = tile_diag(qr.transpose(0, 1, 3, 2), p, c).transpose(0, 2, 1, 3).reshape(nt, gt * p, tc * LANES)
    wr_im = tile_diag((-qi).transpose(0, 1, 3, 2), p, c).transpose(0, 2, 1, 3).reshape(nt, gt * p, tc * LANES)
    return (g2.astype(BF16), ws_re.astype(BF16), ws_im.astype(BF16), wr_re.astype(BF16), wr_im.astype(BF16),
            pr[:, tc].reshape(nt, 1, gt * p), pi[:, tc].reshape(nt, 1, gt * p))


def _s5_scan(u, tables, *, seq):
    t, w = u.shape
    g2, ws_re, ws_im, wr_re, wr_im, lr, li = tables
    nt, sp = lr.shape[0], lr.shape[-1]
    rows = seq // S5_CHUNK
    tile = lambda k, b: (k, 0, 0)
    return pl.pallas_call(
        functools.partial(_s5_kernel, tc=S5_CHUNK),
        out_shape=jax.ShapeDtypeStruct((t, w), F32),
        grid=(nt, t // seq),
        in_specs=[pl.BlockSpec((seq, LANES), lambda k, b: (b, k)),
                  pl.BlockSpec((None,) + g2.shape[1:], tile),
                  pl.BlockSpec((None,) + ws_re.shape[1:], tile),
                  pl.BlockSpec((None,) + ws_im.shape[1:], tile),
                  pl.BlockSpec((None,) + wr_re.shape[1:], tile),
                  pl.BlockSpec((None,) + wr_im.shape[1:], tile),
                  pl.BlockSpec((None, 1, sp), tile),
                  pl.BlockSpec((None, 1, sp), tile)],
        out_specs=pl.BlockSpec((seq, LANES), lambda k, b: (b, k)),
        scratch_shapes=[pltpu.VMEM((rows, sp), F32)] * 4,
        compiler_params=_cparams(("parallel", "parallel")),
        name="s5_scan",
    )(u, g2, ws_re, ws_im, wr_re, wr_im, lr, li)


def _s5_post_kernel(y_ref, u_ref, d_ref, w_ref, b_ref, o_ref):
    y = y_ref[...] + d_ref[...] * u_ref[...].astype(F32)
    c0 = math.sqrt(2.0 / math.pi)
    y = 0.5 * y * (1.0 + jnp.tanh(c0 * (y + 0.044715 * (y * y * y))))
    gate = jnp.dot(y.astype(BF16), w_ref[...], preferred_element_type=F32) + b_ref[...]
    o_ref[...] = (y * _sigmoid(gate)).astype(o_ref.dtype)


def _s5_post(y, proj, d_skip, w_glu, b_glu, *, tm):
    t, w = y.shape
    return pl.pallas_call(
        _s5_post_kernel,
        out_shape=jax.ShapeDtypeStruct((t, w), BF16),
        grid=(t // tm,),
        in_specs=[pl.BlockSpec((tm, w), lambda i: (i, 0)),
                  pl.BlockSpec((tm, w), lambda i: (i, 0)),
                  pl.BlockSpec((1, w), lambda i: (0, 0)),
                  pl.BlockSpec((w, w), lambda i: (0, 0)),
                  pl.BlockSpec((1, w), lambda i: (0, 0))],
        out_specs=pl.BlockSpec((tm, w), lambda i: (i, 0)),
        compiler_params=_cparams(("parallel",)),
        name="s5_post",
    )(y, proj, d_skip.reshape(1, w), w_glu, b_glu.reshape(1, w))


def _lru_kernel(x_ref, cw_ref, cb_ref, wr_ref, br_ref, wi_ref, bi_ref, lam_ref, o_ref,
                xbuf_ref, a_ref, g_ref, h_ref, *, tl, kw):
    pad = 8
    w = x_ref.shape[1]

    @pl.when(pl.program_id(1) == 0)
    def _():
        xbuf_ref[pl.ds(0, pad), :] = jnp.zeros((pad, w), F32)
        h_ref[...] = jnp.zeros_like(h_ref)

    @pl.when(pl.program_id(1) > 0)
    def _():
        xbuf_ref[pl.ds(0, pad), :] = xbuf_ref[pl.ds(tl, pad), :]

    xbuf_ref[pl.ds(pad, tl), :] = x_ref[...].astype(F32)
    cw = cw_ref[...]
    xc = cb_ref[...] + xbuf_ref[pl.ds(pad - (kw - 1), tl), :] * cw[0:1, :]
    for k in range(1, kw):
        xc = xc + xbuf_ref[pl.ds(pad - (kw - 1) + k, tl), :] * cw[k:k + 1, :]
    xcb = xc.astype(BF16)
    nsup = w // LRU_SUPER
    rs, is_ = [], []
    for s in range(nsup):
        xs = xcb[:, s * LRU_SUPER:(s + 1) * LRU_SUPER]
        rs.append(jnp.dot(xs, wr_ref[s], preferred_element_type=F32))
        is_.append(jnp.dot(xs, wi_ref[s], preferred_element_type=F32))
    r = _sigmoid(jnp.concatenate(rs, axis=1) + br_ref[...])
    i = _sigmoid(jnp.concatenate(is_, axis=1) + bi_ref[...])
    log_a = (-LRU_C) * r * _softplus(-lam_ref[...])
    a_ref[...] = jnp.exp(log_a)
    th = jnp.tanh(log_a)
    g_ref[...] = jnp.sqrt((-2.0 * th) / (1.0 - th)) * (i * xc)

    def body(t8, h):
        base = pl.multiple_of(t8 * 8, 8)
        for r_ in range(8):
            h = a_ref[pl.ds(base + r_, 1), :] * h + g_ref[pl.ds(base + r_, 1), :]
            g_ref[pl.ds(base + r_, 1), :] = h
        return h

    h_ref[...] = lax.fori_loop(0, tl // 8, body, h_ref[...])
    o_ref[...] = g_ref[...].astype(o_ref.dtype)


def _block_diag_super(wblk):
    n, k, _ = wblk.shape
    per = LRU_SUPER // k
    wb = wblk.reshape(n // per, per, k, k)
    eye = jnp.eye(per, dtype=wblk.dtype)
    sup = wb[:, :, :, None, :] * eye[None, :, None, :, None]
    return sup.reshape(n // per, LRU_SUPER, LRU_SUPER)


def _lru(proj3, col_block, conv_w, conv_b, w_r, b_r, w_i, b_i, lam, *, tl):
    b, l, _ = proj3.shape
    w = conv_w.shape[1]
    kw = conv_w.shape[0]
    wr = _block_diag_super(w_r).astype(BF16)
    wi = _block_diag_super(w_i).astype(BF16)
    vec = lambda bi, ti: (0, 0)
    full3 = lambda bi, ti: (0, 0, 0)
    return pl.pallas_call(
        functools.partial(_lru_kernel, tl=tl, kw=kw),
        out_shape=jax.ShapeDtypeStruct((b, l, w), BF16),
        grid=(b, l // tl),
        in_specs=[pl.BlockSpec((None, tl, w), lambda bi, ti: (bi, ti, col_block)),
                  pl.BlockSpec((kw, w), vec),
                  pl.BlockSpec((1, w), vec),
                  pl.BlockSpec(wr.shape, full3),
                  pl.BlockSpec((1, w), vec),
                  pl.BlockSpec(wi.shape, full3),
                  pl.BlockSpec((1, w), vec),
                  pl.BlockSpec((1, w), vec)],
        out_specs=pl.BlockSpec((None, tl, w), lambda bi, ti: (bi, ti, 0)),
        scratch_shapes=[pltpu.VMEM((tl + 8, w), F32), pltpu.VMEM((tl, w), F32),
                        pltpu.VMEM((tl, w), F32), pltpu.VMEM((1, w), F32)],
        compiler_params=_cparams(("parallel", "arbitrary")),
        name="rglru",
    )(proj3, conv_w, conv_b.reshape(1, w), wr, b_r.reshape(1, w), wi, b_i.reshape(1, w), lam.reshape(1, w))


def _attn_kernel(q_ref, k_ref, v_ref, o_ref, *, tq, hp):
    qi = pl.program_id(2)
    d = HEAD_DIM
    row = lax.broadcasted_iota(jnp.int32, (tq, tq), 0)
    col = lax.broadcasted_iota(jnp.int32, (tq, tq), 1)
    suffix = jnp.where(row > col, 1.0, 0.0).astype(BF16)
    suffix2 = jnp.concatenate([suffix, suffix], axis=0)
    causal = col < row
    qs = [q_ref[:, h * d:(h + 1) * d] for h in range(hp)]

    def blocks(j, state, masked):
        ks = pl.multiple_of(j * tq, tq)
        heads = range(hp)
        zs = [lax.dot_general(qs[h], k_ref[pl.ds(ks, tq), h * d:(h + 1) * d], (((1,), (1,)), ((), ())),
                              preferred_element_type=F32) for h in heads]
        sps = []
        for h in heads:
            sp = _softplus(zs[h])
            sps.append(jnp.where(causal, sp, 0.0) if masked else sp)
        sss = []
        for h in heads:
            hi = sps[h].astype(BF16)
            lo = (sps[h] - hi.astype(F32)).astype(BF16)
            sss.append(jnp.dot(jnp.concatenate([hi, lo], axis=1), suffix2, preferred_element_type=F32))
        out = []
        for h in heads:
            carry, acc = state[2 * h], state[2 * h + 1]
            w = jnp.exp(zs[h] - sps[h] - sss[h] - carry)
            if masked:
                w = jnp.where(causal, w, 0.0)
            acc = acc + jnp.dot(w.astype(BF16), v_ref[pl.ds(ks, tq), h * d:(h + 1) * d],
                                preferred_element_type=F32)
            out.extend((carry + jnp.sum(sps[h], axis=-1, keepdims=True), acc))
        return tuple(out)

    state = blocks(qi, (jnp.zeros((tq, 1), F32), jnp.zeros((tq, d), F32)) * hp, True)
    state = lax.fori_loop(0, qi, lambda jj, c: blocks(qi - 1 - jj, c, False), state)
    for h in range(hp):
        o_ref[:, h * d:(h + 1) * d] = state[2 * h + 1].astype(o_ref.dtype)


def _attention(proj3, q_blk, k_blk, v_blk, n_heads, *, tq, hp):
    b, l, _ = proj3.shape
    d = HEAD_DIM
    return pl.pallas_call(
        functools.partial(_attn_kernel, tq=tq, hp=hp),
        out_shape=jax.ShapeDtypeStruct((b, l, n_heads * d), BF16),
        grid=(b, n_heads // hp, l // tq),
        in_specs=[pl.BlockSpec((None, tq, hp * d), lambda bi, hi, qi: (bi, qi, q_blk // hp + hi)),
                  pl.BlockSpec((None, l, hp * d), lambda bi, hi, qi: (bi, 0, k_blk // hp + hi)),
                  pl.BlockSpec((None, l, hp * d), lambda bi, hi, qi: (bi, 0, v_blk // hp + hi))],
        out_specs=pl.BlockSpec((None, tq, hp * d), lambda bi, hi, qi: (bi, qi, hi)),
        compiler_params=_cparams(("parallel", "parallel", "arbitrary")),
        name="stick_attn",
    )(proj3, proj3, proj3)


def _merge_kernel(ya_ref, yb_ref, yc_ref, wa_ref, wb_ref, wc_ref, ga_ref, gb_ref, gc_ref, bg_ref, o_ref):
    bg = bg_ref[...]
    out = None
    for n, (y_ref, w_ref, gl_ref) in enumerate(((ya_ref, wa_ref, ga_ref), (yb_ref, wb_ref, gb_ref),
                                                (yc_ref, wc_ref, gc_ref))):
        gate = _sigmoid(gl_ref[...].astype(F32) + bg[n:n + 1, :])
        term = gate * jnp.dot(y_ref[...], w_ref[...], preferred_element_type=F32)
        out = term if out is None else out + term
    o_ref[...] = out.astype(o_ref.dtype)


def _merge(ys, ws, proj, gate_col, b_gate, *, tm, tn):
    t, w = ys[0].shape
    d = ws[0].shape[1]
    gblk = gate_col // tn
    per = d // tn
    y_spec = pl.BlockSpec((tm, w), lambda i, j: (i, 0))
    w_spec = pl.BlockSpec((w, tn), lambda i, j: (0, j))
    gate_specs = [pl.BlockSpec((tm, tn), functools.partial(lambda i, j, n: (i, gblk + n * per + j), n=n))
                  for n in range(N_BRANCH)]
    return pl.pallas_call(
        _merge_kernel,
        out_shape=jax.ShapeDtypeStruct((t, d), BF16),
        grid=(t // tm, d // tn),
        in_specs=[y_spec] * 3 + [w_spec] * 3 + gate_specs
                 + [pl.BlockSpec((N_BRANCH, tn), lambda i, j: (0, j))],
        out_specs=pl.BlockSpec((tm, tn), lambda i, j: (i, j)),
        compiler_params=_cparams(("parallel", "arbitrary")),
        name="merge",
    )(*ys, *ws, proj, proj, proj, b_gate.reshape(N_BRANCH, d))


def _outproj_kernel(m_ref, w_ref, h_ref, g_ref, ho_ref, hn_ref):
    h = h_ref[...] + jnp.dot(m_ref[...], w_ref[...], preferred_element_type=F32)
    ho_ref[...] = h
    hn_ref[...] = _rms_rows(h, g_ref[...]).astype(hn_ref.dtype)


def _outproj(merged, w_out, h, g, *, tm):
    t, d = h.shape
    row = lambda i: (i, 0)
    return pl.pallas_call(
        _outproj_kernel,
        out_shape=(jax.ShapeDtypeStruct((t, d), F32), jax.ShapeDtypeStruct((t, d), BF16)),
        grid=(t // tm,),
        in_specs=[pl.BlockSpec((tm, d), row),
                  pl.BlockSpec((d, d), lambda i: (0, 0)),
                  pl.BlockSpec((tm, d), row),
                  pl.BlockSpec((1, d), lambda i: (0, 0))],
        out_specs=(pl.BlockSpec((tm, d), row), pl.BlockSpec((tm, d), row)),
        compiler_params=_cparams(("parallel",)),
        name="outproj",
    )(merged, w_out, h, g.reshape(1, d))


def _mlp_kernel(x_ref, wu_ref, wd_ref, h_ref, g_ref, o_ref, acc_ref, *, final_norm):
    f = pl.program_id(1)

    @pl.when(f == 0)
    def _():
        acc_ref[...] = h_ref[...]

    hid = jnp.maximum(jnp.dot(x_ref[...], wu_ref[...], preferred_element_type=F32), 0.0)
    acc_ref[...] += jnp.dot((hid * hid).astype(BF16), wd_ref[...], preferred_element_type=F32)

    @pl.when(f == pl.num_programs(1) - 1)
    def _():
        h = acc_ref[...]
        o_ref[...] = _rms_rows(h, g_ref[...]) if final_norm else h


def _mlp(hn, w_up, w_down, h, g, *, tm, tf, final_norm):
    t, d = h.shape
    ff = w_up.shape[1]
    row = lambda i, f: (i, 0)
    return pl.pallas_call(
        functools.partial(_mlp_kernel, final_norm=final_norm),
        out_shape=jax.ShapeDtypeStruct((t, d), F32),
        grid=(t // tm, ff // tf),
        in_specs=[pl.BlockSpec((tm, d), row),
                  pl.BlockSpec((d, tf), lambda i, f: (0, f)),
                  pl.BlockSpec((tf, d), lambda i, f: (f, 0)),
                  pl.BlockSpec((tm, d), row),
                  pl.BlockSpec((1, d), lambda i, f: (0, 0))],
        out_specs=pl.BlockSpec((tm, d), row),
        scratch_shapes=[pltpu.VMEM((tm, d), F32)],
        compiler_params=_cparams(("parallel", "arbitrary")),
        name="mlp",
    )(hn, w_up, w_down, h, g.reshape(1, d))


def _layer(h, bsz, seq, p, *, final_g):
    t, d = h.shape
    s5_w = p["s5_d"].shape[0]
    lru_w = p["lru_lambda"].shape[0]
    attn_w = p["w_br_attn"].shape[0]
    n_heads = attn_w // HEAD_DIM
    off_lru = s5_w
    off_q = off_lru + lru_w
    off_k = off_q + attn_w
    off_v = off_k + attn_w
    off_gate = off_v + attn_w

    col = jnp.arange(p["w_in"].shape[1])
    q_scale = jnp.where((col >= off_q) & (col < off_k), HEAD_DIM ** -0.5, 1.0).astype(F32)
    proj, u_s5 = _inproj(h, p["norm_mix_g"], (p["w_in"] * q_scale[None, :]).astype(BF16), tm=1024, tn=s5_w)
    proj3 = proj.reshape(bsz, seq, proj.shape[1])

    tables = _s5_tables(p["s5_lam_re"], p["s5_lam_im"], p["s5_log_dt"], p["s5_b_re"], p["s5_b_im"],
                        p["s5_c_re"], p["s5_c_im"])
    y_ssm = _s5_scan(u_s5, tables, seq=seq)
    y_s5 = _s5_post(y_ssm, u_s5, p["s5_d"], p["s5_w_glu"].astype(BF16), p["s5_b_glu"], tm=1024)

    y_lru = _lru(proj3, off_lru // lru_w, p["lru_conv_w"], p["lru_conv_b"], p["lru_w_r"], p["lru_b_r"],
                 p["lru_w_i"], p["lru_b_i"], p["lru_lambda"], tl=512).reshape(t, lru_w)

    y_attn = _attention(proj3, off_q // HEAD_DIM, off_k // HEAD_DIM, off_v // HEAD_DIM, n_heads,
                        tq=256, hp=8).reshape(t, attn_w)

    merged = _merge((y_s5, y_lru, y_attn),
                    (p["w_br_s5"].astype(BF16), p["w_br_lru"].astype(BF16), p["w_br_attn"].astype(BF16)),
                    proj, off_gate, p["b_gate"], tm=1024, tn=512)
    h, hn = _outproj(merged, p["w_out"].astype(BF16), h, p["norm_mlp_g"], tm=256)
    g_last = p["norm_mlp_g"] if final_g is None else final_g
    return _mlp(hn, p["w_up"].astype(BF16), p["w_down"].astype(BF16), h, g_last,
                tm=512, tf=512, final_norm=final_g is not None)


_LAYER_PARAMS = ("norm_mix_g", "w_in", "b_gate", "s5_lam_re", "s5_lam_im", "s5_log_dt", "s5_b_re", "s5_b_im",
                 "s5_c_re", "s5_c_im", "s5_d", "s5_w_glu", "s5_b_glu", "lru_conv_w", "lru_conv_b", "lru_w_r",
                 "lru_b_r", "lru_w_i", "lru_b_i", "lru_lambda", "w_br_s5", "w_br_lru", "w_br_attn", "w_out",
                 "norm_mlp_g", "w_up", "w_down")


def kernel(x, norm_mix_g, w_in, b_gate, s5_lam_re, s5_lam_im, s5_log_dt, s5_b_re, s5_b_im, s5_c_re, s5_c_im,
           s5_d, s5_w_glu, s5_b_glu, lru_conv_w, lru_conv_b, lru_w_r, lru_b_r, lru_w_i, lru_b_i, lru_lambda,
           w_br_s5, w_br_lru, w_br_attn, w_out, norm_mlp_g, w_up, w_down, final_norm_g):
    stacked = dict(zip(_LAYER_PARAMS, (norm_mix_g, w_in, b_gate, s5_lam_re, s5_lam_im, s5_log_dt, s5_b_re,
                                       s5_b_im, s5_c_re, s5_c_im, s5_d, s5_w_glu, s5_b_glu, lru_conv_w,
                                       lru_conv_b, lru_w_r, lru_b_r, lru_w_i, lru_b_i, lru_lambda, w_br_s5,
                                       w_br_lru, w_br_attn, w_out, norm_mlp_g, w_up, w_down)))
    bsz, seq, d = x.shape
    depth = w_in.shape[0]
    h = x.reshape(bsz * seq, d).astype(F32)
    for layer in range(depth):
        p = {k: v[layer] for k, v in stacked.items()}
        h = _layer(h, bsz, seq, p, final_g=final_norm_g if layer == depth - 1 else None)
    return h.reshape(bsz, seq, d).astype(x.dtype)
```

```python
import functools
import math

import jax
import jax.numpy as jnp
from jax import lax
from jax.experimental import pallas as pl
from jax.experimental.pallas import tpu as pltpu

F32 = jnp.float32
BF16 = jnp.bfloat16

EPS = 1e-6
LANES = 128
HEAD_DIM = 128
S5_GROUP = 16
S5_CHUNK = 16
LRU_BLOCK = 64
LRU_SUPER = 256
LRU_C = 8.0
N_BRANCH = 3
LOG2E = 1.4426950408889634
VMEM_LIMIT = 56 * 1024 * 1024
HIGHEST = lax.Precision.HIGHEST


def _cparams(sem):
    return pltpu.CompilerParams(dimension_semantics=sem, vmem_limit_bytes=VMEM_LIMIT)


def _softplus(z):
    return jnp.maximum(z, 0.0) + jnp.log(1.0 + jnp.exp2(jnp.abs(z) * (-LOG2E)))


def _sigmoid(z):
    return 1.0 / (1.0 + jnp.exp(-z))


def _rms_rows(x, g):
    ms = jnp.mean(x * x, axis=-1, keepdims=True)
    return (x * lax.rsqrt(ms + EPS)) * g


def _inproj_kernel(x_ref, g_ref, w_ref, o_ref, o32_ref, xn_ref, *, row_chunk):
    j = pl.program_id(1)

    @pl.when(j == 0)
    def _():
        g = g_ref[...]

        def body(c, carry):
            r = pl.multiple_of(c * row_chunk, row_chunk)
            xn_ref[pl.ds(r, row_chunk), :] = _rms_rows(x_ref[pl.ds(r, row_chunk), :], g).astype(BF16)
            return carry

        lax.fori_loop(0, x_ref.shape[0] // row_chunk, body, 0)

    acc = jnp.dot(xn_ref[...], w_ref[...], preferred_element_type=F32)
    o_ref[...] = acc.astype(o_ref.dtype)

    @pl.when(j == 0)
    def _():
        o32_ref[...] = acc


def _inproj(h, g, w, *, tm, tn):
    t, d = h.shape
    n = w.shape[1]
    return pl.pallas_call(
        functools.partial(_inproj_kernel, row_chunk=32),
        out_shape=(jax.ShapeDtypeStruct((t, n), BF16), jax.ShapeDtypeStruct((t, tn), F32)),
        grid=(t // tm, n // tn),
        in_specs=[pl.BlockSpec((tm, d), lambda i, j: (i, 0)),
                  pl.BlockSpec((1, d), lambda i, j: (0, 0)),
                  pl.BlockSpec((d, tn), lambda i, j: (0, j))],
        out_specs=(pl.BlockSpec((tm, tn), lambda i, j: (i, j)),
                   pl.BlockSpec((tm, tn), lambda i, j: (i, 0))),
        scratch_shapes=[pltpu.VMEM((tm, d), BF16)],
        compiler_params=_cparams(("parallel", "arbitrary")),
        name="inproj",
    )(h, g.reshape(1, d), w)


def _s5_kernel(u_ref, g2_ref, ws_ref, wrt_ref, lr_ref, li_ref, o_ref,
               xre_ref, xim_ref, hre_ref, him_ref, *, tc):
    rows = u_ref.shape[0] // tc
    lanes = u_ref.shape[1]
    xc = jnp.concatenate([u_ref[pl.ds(t, rows, stride=tc), :].astype(BF16) for t in range(tc)], axis=1)
    lr = lr_ref[...]
    li = li_ref[...]
    sp = lr.shape[1]
    x = jnp.dot(xc, ws_ref[...], preferred_element_type=F32)
    xre_ref[...] = x[:, :sp]
    xim_ref[...] = x[:, sp:]

    def body(k, carry):
        hr, hi = carry
        base = pl.multiple_of(k * 8, 8)
        for r in range(8):
            hre_ref[pl.ds(base + r, 1), :] = hr
            him_ref[pl.ds(base + r, 1), :] = hi
            xr = xre_ref[pl.ds(base + r, 1), :]
            xi = xim_ref[pl.ds(base + r, 1), :]
            hr, hi = lr * hr - li * hi + xr, lr * hi + li * hr + xi
        return hr, hi

    zero = jnp.zeros(lr.shape, F32)
    lax.fori_loop(0, rows // 8, body, (zero, zero))

    h0 = jnp.concatenate([hre_ref[...].astype(BF16), him_ref[...].astype(BF16)], axis=1)
    yr = lax.dot_general(h0, wrt_ref[...], (((1,), (1,)), ((), ())), preferred_element_type=F32)
    nblk = g2_ref.shape[0] // lanes
    for m in range(tc // 2):
        kdim = (2 * m + 2) * lanes
        y = jnp.dot(xc[:, :kdim], g2_ref[(nblk * lanes - kdim):, :], preferred_element_type=F32)
        y = y + yr[:, 2 * m * lanes:(2 * m + 2) * lanes]
        o_ref[pl.ds(2 * m, rows, stride=tc), :] = y[:, :lanes]
        o_ref[pl.ds(2 * m + 1, rows, stride=tc), :] = y[:, lanes:]


def _s5_tables(lam_re, lam_im, log_dt, b_re, b_im, c_re, c_im):
    tc = S5_CHUNK
    g, p = lam_re.shape
    c = b_re.shape[-1]
    lr = lam_re.astype(F32)
    li = lam_im.astype(F32)
    dt = jnp.exp(log_dt.astype(F32))[:, None]
    ar, ai = lr * dt, li * dt
    ks = jnp.arange(tc + 1, dtype=F32)[None, :, None]
    mag = jnp.exp(ks * ar[:, None, :])
    pr = mag * jnp.cos(ks * ai[:, None, :])
    pi = mag * jnp.sin(ks * ai[:, None, :])
    nr = jnp.expm1(ar) * jnp.cos(ai) - 2.0 * jnp.sin(0.5 * ai) ** 2
    ni = jnp.exp(ar) * jnp.sin(ai)
    den = lr * lr + li * li
    fr = (nr * lr + ni * li) / den
    fi = (ni * lr - nr * li) / den
    bbr = fr[..., None] * b_re - fi[..., None] * b_im
    bbi = fr[..., None] * b_im + fi[..., None] * b_re
    er = pr[..., None] * bbr[:, None] - pi[..., None] * bbi[:, None]
    ei = pr[..., None] * bbi[:, None] + pi[..., None] * bbr[:, None]
    kk = (jnp.einsum('gop,gkpi->gkoi', c_re, er[:, :tc], precision=HIGHEST)
          - jnp.einsum('gop,gkpi->gkoi', c_im, ei[:, :tc], precision=HIGHEST))
    gt = LANES // c
    nt = g // gt

    def diag_lanes(x):
        a, sub = x.shape[1], x.shape[-1]
        xt = x.reshape(nt, gt, a, c, sub).transpose(0, 2, 1, 3, 4)
        tiled = jnp.tile(xt, (1, 1, 1, 1, gt))
        row_g = lax.broadcasted_iota(jnp.int32, tiled.shape, 2)
        col_g = lax.broadcasted_iota(jnp.int32, tiled.shape, 4) // sub
        return jnp.where(row_g == col_g, tiled, 0.0).astype(BF16).reshape(nt, a * gt * c, gt * sub)

    kt = kk.transpose(0, 1, 3, 2)[:, ::-1]
    zblk = jnp.zeros_like(kt[:, :1])
    g2 = jnp.concatenate([diag_lanes(jnp.concatenate([kt, zblk], axis=1)),
                          diag_lanes(jnp.concatenate([zblk, kt], axis=1))], axis=-1)
    ws = jnp.concatenate([diag_lanes(er[:, tc - 1::-1].transpose(0, 1, 3, 2)),
                          diag_lanes(ei[:, tc - 1::-1].transpose(0, 1, 3, 2))], axis=-1)
    pr1, pi1 = pr[:, 1:, None, :], pi[:, 1:, None, :]
    qr = c_re[:, None] * pr1 - c_im[:, None] * pi1
    qi = c_re[:, None] * pi1 + c_im[:, None] * pr1
    wrt = jnp.concatenate([diag_lanes(qr), diag_lanes(-qi)], axis=-1)
    return (g2, ws, wrt, pr[:, tc].reshape(nt, 1, gt * p), pi[:, tc].reshape(nt, 1, gt * p))


def _s5_scan(u, tables, *, seq):
    t, w = u.shape
    g2, ws, wrt, lr, li = tables
    nt, sp = lr.shape[0], lr.shape[-1]
    rows = seq // S5_CHUNK
    tile = lambda k, b: (k, 0, 0)
    return pl.pallas_call(
        functools.partial(_s5_kernel, tc=S5_CHUNK),
        out_shape=jax.ShapeDtypeStruct((t, w), F32),
        grid=(nt, t // seq),
        in_specs=[pl.BlockSpec((seq, LANES), lambda k, b: (b, k)),
                  pl.BlockSpec((None,) + g2.shape[1:], tile),
                  pl.BlockSpec((None,) + ws.shape[1:], tile),
                  pl.BlockSpec((None,) + wrt.shape[1:], tile),
                  pl.BlockSpec((None, 1, sp), tile),
                  pl.BlockSpec((None, 1, sp), tile)],
        out_specs=pl.BlockSpec((seq, LANES), lambda k, b: (b, k)),
        scratch_shapes=[pltpu.VMEM((rows, sp), F32)] * 4,
        compiler_params=_cparams(("parallel", "parallel")),
        name="s5_scan",
    )(u, g2, ws, wrt, lr, li)


def _s5_post_kernel(y_ref, u_ref, d_ref, w_ref, b_ref, o_ref):
    y = y_ref[...] + d_ref[...] * u_ref[...].astype(F32)
    c0 = math.sqrt(2.0 / math.pi)
    y = 0.5 * y * (1.0 + jnp.tanh(c0 * (y + 0.044715 * (y * y * y))))
    gate = jnp.dot(y.astype(BF16), w_ref[...], preferred_element_type=F32) + b_ref[...]
    o_ref[...] = (y * _sigmoid(gate)).astype(o_ref.dtype)


def _s5_post(y, proj, d_skip, w_glu, b_glu, *, tm):
    t, w = y.shape
    return pl.pallas_call(
        _s5_post_kernel,
        out_shape=jax.ShapeDtypeStruct((t, w), BF16),
        grid=(t // tm,),
        in_specs=[pl.BlockSpec((tm, w), lambda i: (i, 0)),
                  pl.BlockSpec((tm, w), lambda i: (i, 0)),
                  pl.BlockSpec((1, w), lambda i: (0, 0)),
                  pl.BlockSpec((w, w), lambda i: (0, 0)),
                  pl.BlockSpec((1, w), lambda i: (0, 0))],
        out_specs=pl.BlockSpec((tm, w), lambda i: (i, 0)),
        compiler_params=_cparams(("parallel",)),
        name="s5_post",
    )(y, proj, d_skip.reshape(1, w), w_glu, b_glu.reshape(1, w))


def _lru_kernel(x_ref, cw_ref, cb_ref, wr_ref, br_ref, wi_ref, bi_ref, lam_ref, o_ref,
                xbuf_ref, a_ref, g_ref, h_ref, *, tl, kw):
    pad = 8
    w = x_ref.shape[1]

    @pl.when(pl.program_id(1) == 0)
    def _():
        xbuf_ref[pl.ds(0, pad), :] = jnp.zeros((pad, w), F32)
        h_ref[...] = jnp.zeros_like(h_ref)

    @pl.when(pl.program_id(1) > 0)
    def _():
        xbuf_ref[pl.ds(0, pad), :] = xbuf_ref[pl.ds(tl, pad), :]

    xbuf_ref[pl.ds(pad, tl), :] = x_ref[...].astype(F32)
    cw = cw_ref[...]
    xc = cb_ref[...] + xbuf_ref[pl.ds(pad - (kw - 1), tl), :] * cw[0:1, :]
    for k in range(1, kw):
        xc = xc + xbuf_ref[pl.ds(pad - (kw - 1) + k, tl), :] * cw[k:k + 1, :]
    xcb = xc.astype(BF16)
    nsup = w // LRU_SUPER
    rs, is_ = [], []
    for s in range(nsup):
        xs = xcb[:, s * LRU_SUPER:(s + 1) * LRU_SUPER]
        rs.append(jnp.dot(xs, wr_ref[s], preferred_element_type=F32))
        is_.append(jnp.dot(xs, wi_ref[s], preferred_element_type=F32))
    r = _sigmoid(jnp.concatenate(rs, axis=1) + br_ref[...])
    i = _sigmoid(jnp.concatenate(is_, axis=1) + bi_ref[...])
    log_a = (-LRU_C) * r * _softplus(-lam_ref[...])
    a_ref[...] = jnp.exp(log_a)
    th = jnp.tanh(log_a)
    g_ref[...] = jnp.sqrt((-2.0 * th) / (1.0 - th)) * (i * xc)

    def body(t8, h):
        base = pl.multiple_of(t8 * 8, 8)
        for r_ in range(8):
            h = a_ref[pl.ds(base + r_, 1), :] * h + g_ref[pl.ds(base + r_, 1), :]
            g_ref[pl.ds(base + r_, 1), :] = h
        return h

    h_ref[...] = lax.fori_loop(0, tl // 8, body, h_ref[...])
    o_ref[...] = g_ref[...].astype(o_ref.dtype)


def _block_diag_super(wblk):
    n, k, _ = wblk.shape
    per = LRU_SUPER // k
    wb = wblk.reshape(n // per, per, k, k)
    eye = jnp.eye(per, dtype=wblk.dtype)
    sup = wb[:, :, :, None, :] * eye[None, :, None, :, None]
    return sup.reshape(n // per, LRU_SUPER, LRU_SUPER)


def _lru(proj3, col_block, conv_w, conv_b, w_r, b_r, w_i, b_i, lam, *, tl):
    b, l, _ = proj3.shape
    w = conv_w.shape[1]
    kw = conv_w.shape[0]
    wr = _block_diag_super(w_r).astype(BF16)
    wi = _block_diag_super(w_i).astype(BF16)
    vec = lambda bi, ti: (0, 0)
    full3 = lambda bi, ti: (0, 0, 0)
    return pl.pallas_call(
        functools.partial(_lru_kernel, tl=tl, kw=kw),
        out_shape=jax.ShapeDtypeStruct((b, l, w), BF16),
        grid=(b, l // tl),
        in_specs=[pl.BlockSpec((None, tl, w), lambda bi, ti: (bi, ti, col_block)),
                  pl.BlockSpec((kw, w), vec),
                  pl.BlockSpec((1, w), vec),
                  pl.BlockSpec(wr.shape, full3),
                  pl.BlockSpec((1, w), vec),
                  pl.BlockSpec(wi.shape, full3),
                  pl.BlockSpec((1, w), vec),
                  pl.BlockSpec((1, w), vec)],
        out_specs=pl.BlockSpec((None, tl, w), lambda bi, ti: (bi, ti, 0)),
        scratch_shapes=[pltpu.VMEM((tl + 8, w), F32), pltpu.VMEM((tl, w), F32),
                        pltpu.VMEM((tl, w), F32), pltpu.VMEM((1, w), F32)],
        compiler_params=_cparams(("parallel", "arbitrary")),
        name="rglru",
    )(proj3, conv_w, conv_b.reshape(1, w), wr, b_r.reshape(1, w), wi, b_i.reshape(1, w), lam.reshape(1, w))


def _attn_kernel(q_ref, k_ref, v_ref, o_ref, *, tq, hp):
    qi = pl.program_id(2)
    d = HEAD_DIM
    row = lax.broadcasted_iota(jnp.int32, (tq, tq), 0)
    col = lax.broadcasted_iota(jnp.int32, (tq, tq), 1)
    suffix = jnp.where(row > col, 1.0, 0.0).astype(BF16)
    causal = col < row
    qs = [q_ref[:, h * d:(h + 1) * d] for h in range(hp)]

    def blocks(j, state, masked):
        ks = pl.multiple_of(j * tq, tq)
        heads = range(hp)
        zs = [lax.dot_general(qs[h], k_ref[pl.ds(ks, tq), h * d:(h + 1) * d], (((1,), (1,)), ((), ())),
                              preferred_element_type=F32) for h in heads]
        sps = []
        for h in heads:
            sp = _softplus(zs[h])
            sps.append(jnp.where(causal, sp, 0.0) if masked else sp)
        sss = [jnp.dot(sps[h].astype(BF16), suffix, preferred_element_type=F32) for h in heads]
        out = []
        for h in heads:
            carry, acc = state[2 * h], state[2 * h + 1]
            w = jnp.exp(zs[h] - sps[h] - sss[h] - carry)
            if masked:
                w = jnp.where(causal, w, 0.0)
            acc = acc + jnp.dot(w.astype(BF16), v_ref[pl.ds(ks, tq), h * d:(h + 1) * d],
                                preferred_element_type=F32)
            out.extend((carry + jnp.sum(sps[h], axis=-1, keepdims=True), acc))
        return tuple(out)

    state = blocks(qi, (jnp.zeros((tq, 1), F32), jnp.zeros((tq, d), F32)) * hp, True)
    state = lax.fori_loop(0, qi, lambda jj, c: blocks(qi - 1 - jj, c, False), state)
    for h in range(hp):
        o_ref[:, h * d:(h + 1) * d] = state[2 * h + 1].astype(o_ref.dtype)


def _attention(proj3, q_blk, k_blk, v_blk, n_heads, *, tq, hp):
    b, l, _ = proj3.shape
    d = HEAD_DIM
    return pl.pallas_call(
        functools.partial(_attn_kernel, tq=tq, hp=hp),
        out_shape=jax.ShapeDtypeStruct((b, l, n_heads * d), BF16),
        grid=(b, n_heads // hp, l // tq),
        in_specs=[pl.BlockSpec((None, tq, hp * d), lambda bi, hi, qi: (bi, qi, q_blk // hp + hi)),
                  pl.BlockSpec((None, l, hp * d), lambda bi, hi, qi: (bi, 0, k_blk // hp + hi)),
                  pl.BlockSpec((None, l, hp * d), lambda bi, hi, qi: (bi, 0, v_blk // hp + hi))],
        out_specs=pl.BlockSpec((None, tq, hp * d), lambda bi, hi, qi: (bi, qi, hi)),
        compiler_params=_cparams(("parallel", "parallel", "arbitrary")),
        name="stick_attn",
    )(proj3, proj3, proj3)


def _merge_kernel(ya_ref, yb_ref, yc_ref, wa_ref, wb_ref, wc_ref, ga_ref, gb_ref, gc_ref, bg_ref, o_ref):
    bg = bg_ref[...]
    out = None
    for n, (y_ref, w_ref, gl_ref) in enumerate(((ya_ref, wa_ref, ga_ref), (yb_ref, wb_ref, gb_ref),
                                                (yc_ref, wc_ref, gc_ref))):
        gate = _sigmoid(gl_ref[...].astype(F32) + bg[n:n + 1, :])
        term = gate * jnp.dot(y_ref[...], w_ref[...], preferred_element_type=F32)
        out = term if out is None else out + term
    o_ref[...] = out.astype(o_ref.dtype)


def _merge(ys, ws, proj, gate_col, b_gate, *, tm, tn):
    t, w = ys[0].shape
    d = ws[0].shape[1]
    gblk = gate_col // tn
    per = d // tn
    y_spec = pl.BlockSpec((tm, w), lambda i, j: (i, 0))
    w_spec = pl.BlockSpec((w, tn), lambda i, j: (0, j))
    gate_specs = [pl.BlockSpec((tm, tn), functools.partial(lambda i, j, n: (i, gblk + n * per + j), n=n))
                  for n in range(N_BRANCH)]
    return pl.pallas_call(
        _merge_kernel,
        out_shape=jax.ShapeDtypeStruct((t, d), BF16),
        grid=(t // tm, d // tn),
        in_specs=[y_spec] * 3 + [w_spec] * 3 + gate_specs
                 + [pl.BlockSpec((N_BRANCH, tn), lambda i, j: (0, j))],
        out_specs=pl.BlockSpec((tm, tn), lambda i, j: (i, j)),
        compiler_params=_cparams(("parallel", "arbitrary")),
        name="merge",
    )(*ys, *ws, proj, proj, proj, b_gate.reshape(N_BRANCH, d))


def _outproj_kernel(m_ref, w_ref, h_ref, g_ref, ho_ref, hn_ref):
    h = h_ref[...] + jnp.dot(m_ref[...], w_ref[...], preferred_element_type=F32)
    ho_ref[...] = h
    hn_ref[...] = _rms_rows(h, g_ref[...]).astype(hn_ref.dtype)


def _outproj(merged, w_out, h, g, *, tm):
    t, d = h.shape
    row = lambda i: (i, 0)
    return pl.pallas_call(
        _outproj_kernel,
        out_shape=(jax.ShapeDtypeStruct((t, d), F32), jax.ShapeDtypeStruct((t, d), BF16)),
        grid=(t // tm,),
        in_specs=[pl.BlockSpec((tm, d), row),
                  pl.BlockSpec((d, d), lambda i: (0, 0)),
                  pl.BlockSpec((tm, d), row),
                  pl.BlockSpec((1, d), lambda i: (0, 0))],
        out_specs=(pl.BlockSpec((tm, d), row), pl.BlockSpec((tm, d), row)),
        compiler_params=_cparams(("parallel",)),
        name="outproj",
    )(merged, w_out, h, g.reshape(1, d))


def _mlp_kernel(x_ref, wu_ref, wd_ref, h_ref, g_ref, o_ref, acc_ref, *, final_norm):
    f = pl.program_id(1)

    @pl.when(f == 0)
    def _():
        acc_ref[...] = h_ref[...]

    hid = jnp.maximum(jnp.dot(x_ref[...], wu_ref[...], preferred_element_type=F32), 0.0)
    acc_ref[...] += jnp.dot((hid * hid).astype(BF16), wd_ref[...], preferred_element_type=F32)

    @pl.when(f == pl.num_programs(1) - 1)
    def _():
        h = acc_ref[...]
        o_ref[...] = _rms_rows(h, g_ref[...]) if final_norm else h


def _mlp(hn, w_up, w_down, h, g, *, tm, tf, final_norm):
    t, d = h.shape
    ff = w_up.shape[1]
    row = lambda i, f: (i, 0)
    return pl.pallas_call(
        functools.partial(_mlp_kernel, final_norm=final_norm),
        out_shape=jax.ShapeDtypeStruct((t, d), F32),
        grid=(t // tm, ff // tf),
        in_specs=[pl.BlockSpec((tm, d), row),
                  pl.BlockSpec((d, tf), lambda i, f: (0, f)),
                  pl.BlockSpec((tf, d), lambda i, f: (f, 0)),
                  pl.BlockSpec((tm, d), row),
                  pl.BlockSpec((1, d), lambda i, f: (0, 0))],
        out_specs=pl.BlockSpec((tm, d), row),
        scratch_shapes=[pltpu.VMEM((tm, d), F32)],
        compiler_params=_cparams(("parallel", "arbitrary")),
        name="mlp",
    )(hn, w_up, w_down, h, g.reshape(1, d))


def _layer(h, bsz, seq, p, *, final_g):
    t, d = h.shape
    s5_w = p["s5_d"].shape[0]
    lru_w = p["lru_lambda"].shape[0]
    attn_w = p["w_br_attn"].shape[0]
    n_heads = attn_w // HEAD_DIM
    off_lru = s5_w
    off_q = off_lru + lru_w
    off_k = off_q + attn_w
    off_v = off_k + attn_w
    off_gate = off_v + attn_w

    col = jnp.arange(p["w_in"].shape[1])
    q_scale = jnp.where((col >= off_q) & (col < off_k), HEAD_DIM ** -0.5, 1.0).astype(F32)
    proj, u_s5 = _inproj(h, p["norm_mix_g"], (p["w_in"] * q_scale[None, :]).astype(BF16), tm=1024, tn=s5_w)
    proj3 = proj.reshape(bsz, seq, proj.shape[1])

    tables = _s5_tables(p["s5_lam_re"], p["s5_lam_im"], p["s5_log_dt"], p["s5_b_re"], p["s5_b_im"],
                        p["s5_c_re"], p["s5_c_im"])
    y_ssm = _s5_scan(u_s5, tables, seq=seq)
    y_s5 = _s5_post(y_ssm, u_s5, p["s5_d"], p["s5_w_glu"].astype(BF16), p["s5_b_glu"], tm=1024)

    y_lru = _lru(proj3, off_lru // lru_w, p["lru_conv_w"], p["lru_conv_b"], p["lru_w_r"], p["lru_b_r"],
                 p["lru_w_i"], p["lru_b_i"], p["lru_lambda"], tl=512).reshape(t, lru_w)

    y_attn = _attention(proj3, off_q // HEAD_DIM, off_k // HEAD_DIM, off_v // HEAD_DIM, n_heads,
                        tq=256, hp=8).reshape(t, attn_w)

    merged = _merge((y_s5, y_lru, y_attn),
                    (p["w_br_s5"].astype(BF16), p["w_br_lru"].astype(BF16), p["w_br_attn"].astype(BF16)),
                    proj, off_gate, p["b_gate"], tm=1024, tn=512)
    h, hn = _outproj(merged, p["w_out"].astype(BF16), h, p["norm_mlp_g"], tm=256)
    g_last = p["norm_mlp_g"] if final_g is None else final_g
    return _mlp(hn, p["w_up"].astype(BF16), p["w_down"].astype(BF16), h, g_last,
                tm=512, tf=512, final_norm=final_g is not None)


_LAYER_PARAMS = ("norm_mix_g", "w_in", "b_gate", "s5_lam_re", "s5_lam_im", "s5_log_dt", "s5_b_re", "s5_b_im",
                 "s5_c_re", "s5_c_im", "s5_d", "s5_w_glu", "s5_b_glu", "lru_conv_w", "lru_conv_b", "lru_w_r",
                 "lru_b_r", "lru_w_i", "lru_b_i", "lru_lambda", "w_br_s5", "w_br_lru", "w_br_attn", "w_out",
                 "norm_mlp_g", "w_up", "w_down")


def kernel(x, norm_mix_g, w_in, b_gate, s5_lam_re, s5_lam_im, s5_log_dt, s5_b_re, s5_b_im, s5_c_re, s5_c_im,
           s5_d, s5_w_glu, s5_b_glu, lru_conv_w, lru_conv_b, lru_w_r, lru_b_r, lru_w_i, lru_b_i, lru_lambda,
           w_br_s5, w_br_lru, w_br_attn, w_out, norm_mlp_g, w_up, w_down, final_norm_g):
    stacked = dict(zip(_LAYER_PARAMS, (norm_mix_g, w_in, b_gate, s5_lam_re, s5_lam_im, s5_log_dt, s5_b_re,
                                       s5_b_im, s5_c_re, s5_c_im, s5_d, s5_w_glu, s5_b_glu, lru_conv_w,
                                       lru_conv_b, lru_w_r, lru_b_r, lru_w_i, lru_b_i, lru_lambda, w_br_s5,
                                       w_br_lru, w_br_attn, w_out, norm_mlp_g, w_up, w_down)))
    bsz, seq, d = x.shape
    depth = w_in.shape[0]
    h = x.reshape(bsz * seq, d).astype(F32)
    for layer in range(depth):
        p = {k: v[layer] for k, v in stacked.items()}
        h = _layer(h, bsz, seq, p, final_g=final_norm_g if layer == depth - 1 else None)
    return h.reshape(bsz, seq, d).astype(x.dtype)
```

```python
import functools
import math

import jax
import jax.numpy as jnp
from jax import lax
from jax.experimental import pallas as pl
from jax.experimental.pallas import tpu as pltpu

F32 = jnp.float32
BF16 = jnp.bfloat16

EPS = 1e-6
LANES = 128
HEAD_DIM = 128
S5_GROUP = 16
S5_CHUNK = 16
LRU_BLOCK = 64
LRU_SUPER = 256
LRU_C = 8.0
N_BRANCH = 3
LOG2E = 1.4426950408889634
VMEM_LIMIT = 56 * 1024 * 1024
HIGHEST = lax.Precision.HIGHEST


def _cparams(sem):
    return pltpu.CompilerParams(dimension_semantics=sem, vmem_limit_bytes=VMEM_LIMIT)


def _softplus(z):
    return jnp.maximum(z, 0.0) + jnp.log(1.0 + jnp.exp2(jnp.abs(z) * (-LOG2E)))


def _sigmoid(z):
    return 1.0 / (1.0 + jnp.exp(-z))


def _rms_rows(x, g):
    ms = jnp.mean(x * x, axis=-1, keepdims=True)
    return (x * lax.rsqrt(ms + EPS)) * g


def _inproj_kernel(x_ref, g_ref, w_ref, o_ref, o32_ref, xn_ref, *, row_chunk):
    j = pl.program_id(1)

    @pl.when(j == 0)
    def _():
        g = g_ref[...]

        def body(c, carry):
            r = pl.multiple_of(c * row_chunk, row_chunk)
            xn_ref[pl.ds(r, row_chunk), :] = _rms_rows(x_ref[pl.ds(r, row_chunk), :], g).astype(BF16)
            return carry

        lax.fori_loop(0, x_ref.shape[0] // row_chunk, body, 0)

    acc = jnp.dot(xn_ref[...], w_ref[...], preferred_element_type=F32)
    o_ref[...] = acc.astype(o_ref.dtype)

    @pl.when(j == 0)
    def _():
        o32_ref[...] = acc


def _inproj(h, g, w, *, tm, tn):
    t, d = h.shape
    n = w.shape[1]
    return pl.pallas_call(
        functools.partial(_inproj_kernel, row_chunk=32),
        out_shape=(jax.ShapeDtypeStruct((t, n), BF16), jax.ShapeDtypeStruct((t, tn), F32)),
        grid=(t // tm, n // tn),
        in_specs=[pl.BlockSpec((tm, d), lambda i, j: (i, 0)),
                  pl.BlockSpec((1, d), lambda i, j: (0, 0)),
                  pl.BlockSpec((d, tn), lambda i, j: (0, j))],
        out_specs=(pl.BlockSpec((tm, tn), lambda i, j: (i, j)),
                   pl.BlockSpec((tm, tn), lambda i, j: (i, 0))),
        scratch_shapes=[pltpu.VMEM((tm, d), BF16)],
        compiler_params=_cparams(("parallel", "arbitrary")),
        name="inproj",
    )(h, g.reshape(1, d), w)


def _s5_kernel(u_ref, g2_ref, ws_ref, wrt_ref, lr_ref, li_ref, o_ref,
               xre_ref, xim_ref, hre_ref, him_ref, *, tc):
    rows = u_ref.shape[0] // tc
    lanes = u_ref.shape[1]
    xc = jnp.concatenate([u_ref[pl.ds(t, rows, stride=tc), :].astype(BF16) for t in range(tc)], axis=1)
    lr = lr_ref[...]
    li = li_ref[...]
    sp = lr.shape[1]
    x = jnp.dot(xc, ws_ref[...], preferred_element_type=F32)
    xre_ref[...] = x[:, :sp]
    xim_ref[...] = x[:, sp:]

    def body(k, carry):
        hr, hi = carry
        base = pl.multiple_of(k * 8, 8)
        for r in range(8):
            hre_ref[pl.ds(base + r, 1), :] = hr
            him_ref[pl.ds(base + r, 1), :] = hi
            xr = xre_ref[pl.ds(base + r, 1), :]
            xi = xim_ref[pl.ds(base + r, 1), :]
            hr, hi = lr * hr - li * hi + xr, lr * hi + li * hr + xi
        return hr, hi

    zero = jnp.zeros(lr.shape, F32)
    lax.fori_loop(0, rows // 8, body, (zero, zero))

    h0 = jnp.concatenate([hre_ref[...].astype(BF16), him_ref[...].astype(BF16)], axis=1)
    yr = lax.dot_general(h0, wrt_ref[...], (((1,), (1,)), ((), ())), preferred_element_type=F32)
    nblk = g2_ref.shape[0] // lanes
    for m in range(tc // 2):
        kdim = (2 * m + 2) * lanes
        y = jnp.dot(xc[:, :kdim], g2_ref[(nblk * lanes - kdim):, :], preferred_element_type=F32)
        y = y + yr[:, 2 * m * lanes:(2 * m + 2) * lanes]
        o_ref[pl.ds(2 * m, rows, stride=tc), :] = y[:, :lanes]
        o_ref[pl.ds(2 * m + 1, rows, stride=tc), :] = y[:, lanes:]


def _lane_diag_kernel(a_ref, b_ref, o_ref, *, gt, c):
    rows, sub = a_ref.shape
    width = gt * sub
    src = lax.broadcasted_iota(jnp.int32, (sub, width), 0)
    dst = lax.broadcasted_iota(jnp.int32, (sub, width), 1)
    spread = jnp.where(dst % sub == src, 1.0, 0.0).astype(BF16)
    row_g = (lax.broadcasted_iota(jnp.int32, (rows, width), 0) // c) % gt
    col_g = lax.broadcasted_iota(jnp.int32, (rows, width), 1) // sub
    keep = row_g == col_g
    for n, x_ref in enumerate((a_ref, b_ref)):
        tiled = jnp.dot(x_ref[...].astype(BF16), spread, preferred_element_type=F32)
        o_ref[:, n * width:(n + 1) * width] = jnp.where(keep, tiled, 0.0).astype(o_ref.dtype)


def _lane_diag(a, b, *, gt, c):
    nt, rows, sub = a.shape
    spec = pl.BlockSpec((None, rows, sub), lambda k: (k, 0, 0))
    return pl.pallas_call(
        functools.partial(_lane_diag_kernel, gt=gt, c=c),
        out_shape=jax.ShapeDtypeStruct((nt, rows, 2 * gt * sub), BF16),
        grid=(nt,),
        in_specs=[spec, spec],
        out_specs=pl.BlockSpec((None, rows, 2 * gt * sub), lambda k: (k, 0, 0)),
        compiler_params=_cparams(("parallel",)),
        name="s5_lane_diag",
    )(a, b)


def _s5_tables(lam_re, lam_im, log_dt, b_re, b_im, c_re, c_im):
    tc = S5_CHUNK
    g, p = lam_re.shape
    c = b_re.shape[-1]
    lr = lam_re.astype(F32)
    li = lam_im.astype(F32)
    dt = jnp.exp(log_dt.astype(F32))[:, None]
    ar, ai = lr * dt, li * dt
    ks = jnp.arange(tc + 1, dtype=F32)[None, :, None]
    mag = jnp.exp(ks * ar[:, None, :])
    pr = mag * jnp.cos(ks * ai[:, None, :])
    pi = mag * jnp.sin(ks * ai[:, None, :])
    nr = jnp.expm1(ar) * jnp.cos(ai) - 2.0 * jnp.sin(0.5 * ai) ** 2
    ni = jnp.exp(ar) * jnp.sin(ai)
    den = lr * lr + li * li
    fr = (nr * lr + ni * li) / den
    fi = (ni * lr - nr * li) / den
    bbr = fr[..., None] * b_re - fi[..., None] * b_im
    bbi = fr[..., None] * b_im + fi[..., None] * b_re
    er = pr[..., None] * bbr[:, None] - pi[..., None] * bbi[:, None]
    ei = pr[..., None] * bbi[:, None] + pi[..., None] * bbr[:, None]
    kk = (jnp.einsum('gop,gkpi->gkoi', c_re, er[:, :tc], precision=HIGHEST)
          - jnp.einsum('gop,gkpi->gkoi', c_im, ei[:, :tc], precision=HIGHEST))
    gt = LANES // c
    nt = g // gt

    def tile_rows(x):
        a, sub = x.shape[1], x.shape[-1]
        return x.reshape(nt, gt, a, c, sub).transpose(0, 2, 1, 3, 4).reshape(nt, a * gt * c, sub)

    kt = kk.transpose(0, 1, 3, 2)[:, ::-1]
    zblk = jnp.zeros_like(kt[:, :1])
    g2 = _lane_diag(tile_rows(jnp.concatenate([kt, zblk], axis=1)),
                    tile_rows(jnp.concatenate([zblk, kt], axis=1)), gt=gt, c=c)
    ws = _lane_diag(tile_rows(er[:, tc - 1::-1].transpose(0, 1, 3, 2)),
                    tile_rows(ei[:, tc - 1::-1].transpose(0, 1, 3, 2)), gt=gt, c=c)
    pr1, pi1 = pr[:, 1:, None, :], pi[:, 1:, None, :]
    qr = c_re[:, None] * pr1 - c_im[:, None] * pi1
    qi = c_re[:, None] * pi1 + c_im[:, None] * pr1
    wrt = _lane_diag(tile_rows(qr), tile_rows(-qi), gt=gt, c=c)
    return (g2, ws, wrt, pr[:, tc].reshape(nt, 1, gt * p), pi[:, tc].reshape(nt, 1, gt * p))


def _s5_scan(u, tables, *, seq):
    t, w = u.shape
    g2, ws, wrt, lr, li = tables
    nt, sp = lr.shape[0], lr.shape[-1]
    rows = seq // S5_CHUNK
    tile = lambda k, b: (k, 0, 0)
    return pl.pallas_call(
        functools.partial(_s5_kernel, tc=S5_CHUNK),
        out_shape=jax.ShapeDtypeStruct((t, w), F32),
        grid=(nt, t // seq),
        in_specs=[pl.BlockSpec((seq, LANES), lambda k, b: (b, k)),
                  pl.BlockSpec((None,) + g2.shape[1:], tile),
                  pl.BlockSpec((None,) + ws.shape[1:], tile),
                  pl.BlockSpec((None,) + wrt.shape[1:], tile),
                  pl.BlockSpec((None, 1, sp), tile),
                  pl.BlockSpec((None, 1, sp), tile)],
        out_specs=pl.BlockSpec((seq, LANES), lambda k, b: (b, k)),
        scratch_shapes=[pltpu.VMEM((rows, sp), F32)] * 4,
        compiler_params=_cparams(("parallel", "parallel")),
        name="s5_scan",
    )(u, g2, ws, wrt, lr, li)


def _s5_post_kernel(y_ref, u_ref, d_ref, w_ref, b_ref, o_ref):
    y = y_ref[...] + d_ref[...] * u_ref[...].astype(F32)
    c0 = math.sqrt(2.0 / math.pi)
    y = 0.5 * y * (1.0 + jnp.tanh(c0 * (y + 0.044715 * (y * y * y))))
    gate = jnp.dot(y.astype(BF16), w_ref[...], preferred_element_type=F32) + b_ref[...]
    o_ref[...] = (y * _sigmoid(gate)).astype(o_ref.dtype)


def _s5_post(y, proj, d_skip, w_glu, b_glu, *, tm):
    t, w = y.shape
    return pl.pallas_call(
        _s5_post_kernel,
        out_shape=jax.ShapeDtypeStruct((t, w), BF16),
        grid=(t // tm,),
        in_specs=[pl.BlockSpec((tm, w), lambda i: (i, 0)),
                  pl.BlockSpec((tm, w), lambda i: (i, 0)),
                  pl.BlockSpec((1, w), lambda i: (0, 0)),
                  pl.BlockSpec((w, w), lambda i: (0, 0)),
                  pl.BlockSpec((1, w), lambda i: (0, 0))],
        out_specs=pl.BlockSpec((tm, w), lambda i: (i, 0)),
        compiler_params=_cparams(("parallel",)),
        name="s5_post",
    )(y, proj, d_skip.reshape(1, w), w_glu, b_glu.reshape(1, w))


def _lru_kernel(x_ref, cw_ref, cb_ref, wr_ref, br_ref, wi_ref, bi_ref, lam_ref, o_ref,
                xbuf_ref, a_ref, g_ref, h_ref, *, tl, kw):
    pad = 8
    w = x_ref.shape[1]

    @pl.when(pl.program_id(1) == 0)
    def _():
        xbuf_ref[pl.ds(0, pad), :] = jnp.zeros((pad, w), F32)
        h_ref[...] = jnp.zeros_like(h_ref)

    @pl.when(pl.program_id(1) > 0)
    def _():
        xbuf_ref[pl.ds(0, pad), :] = xbuf_ref[pl.ds(tl, pad), :]

    xbuf_ref[pl.ds(pad, tl), :] = x_ref[...].astype(F32)
    cw = cw_ref[...]
    xc = cb_ref[...] + xbuf_ref[pl.ds(pad - (kw - 1), tl), :] * cw[0:1, :]
    for k in range(1, kw):
        xc = xc + xbuf_ref[pl.ds(pad - (kw - 1) + k, tl), :] * cw[k:k + 1, :]
    xcb = xc.astype(BF16)
    nsup = w // LRU_SUPER
    rs, is_ = [], []
    for s in range(nsup):
        xs = xcb[:, s * LRU_SUPER:(s + 1) * LRU_SUPER]
        rs.append(jnp.dot(xs, wr_ref[s], preferred_element_type=F32))
        is_.append(jnp.dot(xs, wi_ref[s], preferred_element_type=F32))
    r = _sigmoid(jnp.concatenate(rs, axis=1) + br_ref[...])
    i = _sigmoid(jnp.concatenate(is_, axis=1) + bi_ref[...])
    log_a = (-LRU_C) * r * _softplus(-lam_ref[...])
    a_ref[...] = jnp.exp(log_a)
    th = jnp.tanh(log_a)
    g_ref[...] = jnp.sqrt((-2.0 * th) / (1.0 - th)) * (i * xc)

    def body(t8, h):
        base = pl.multiple_of(t8 * 8, 8)
        for r_ in range(8):
            h = a_ref[pl.ds(base + r_, 1), :] * h + g_ref[pl.ds(base + r_, 1), :]
            g_ref[pl.ds(base + r_, 1), :] = h
        return h

    h_ref[...] = lax.fori_loop(0, tl // 8, body, h_ref[...])
    o_ref[...] = g_ref[...].astype(o_ref.dtype)


def _block_diag_super(wblk):
    n, k, _ = wblk.shape
    per = LRU_SUPER // k
    wb = wblk.reshape(n // per, per, k, k)
    eye = jnp.eye(per, dtype=wblk.dtype)
    sup = wb[:, :, :, None, :] * eye[None, :, None, :, None]
    return sup.reshape(n // per, LRU_SUPER, LRU_SUPER)


def _lru(proj3, col_block, conv_w, conv_b, w_r, b_r, w_i, b_i, lam, *, tl):
    b, l, _ = proj3.shape
    w = conv_w.shape[1]
    kw = conv_w.shape[0]
    wr = _block_diag_super(w_r).astype(BF16)
    wi = _block_diag_super(w_i).astype(BF16)
    vec = lambda bi, ti: (0, 0)
    full3 = lambda bi, ti: (0, 0, 0)
    return pl.pallas_call(
        functools.partial(_lru_kernel, tl=tl, kw=kw),
        out_shape=jax.ShapeDtypeStruct((b, l, w), BF16),
        grid=(b, l // tl),
        in_specs=[pl.BlockSpec((None, tl, w), lambda bi, ti: (bi, ti, col_block)),
                  pl.BlockSpec((kw, w), vec),
                  pl.BlockSpec((1, w), vec),
                  pl.BlockSpec(wr.shape, full3),
                  pl.BlockSpec((1, w), vec),
                  pl.BlockSpec(wi.shape, full3),
                  pl.BlockSpec((1, w), vec),
                  pl.BlockSpec((1, w), vec)],
        out_specs=pl.BlockSpec((None, tl, w), lambda bi, ti: (bi, ti, 0)),
        scratch_shapes=[pltpu.VMEM((tl + 8, w), F32), pltpu.VMEM((tl, w), F32),
                        pltpu.VMEM((tl, w), F32), pltpu.VMEM((1, w), F32)],
        compiler_params=_cparams(("parallel", "arbitrary")),
        name="rglru",
    )(proj3, conv_w, conv_b.reshape(1, w), wr, b_r.reshape(1, w), wi, b_i.reshape(1, w), lam.reshape(1, w))


def _attn_kernel(q_ref, k_ref, v_ref, o_ref, *, tq, hp):
    qi = pl.program_id(2)
    d = HEAD_DIM
    row = lax.broadcasted_iota(jnp.int32, (tq, tq), 0)
    col = lax.broadcasted_iota(jnp.int32, (tq, tq), 1)
    suffix = jnp.where(row > col, 1.0, 0.0).astype(BF16)
    causal = col < row
    qs = [q_ref[:, h * d:(h + 1) * d] for h in range(hp)]

    def blocks(j, state, masked):
        ks = pl.multiple_of(j * tq, tq)
        heads = range(hp)
        zs = [lax.dot_general(qs[h], k_ref[pl.ds(ks, tq), h * d:(h + 1) * d], (((1,), (1,)), ((), ())),
                              preferred_element_type=F32) for h in heads]
        sps = []
        for h in heads:
            sp = _softplus(zs[h])
            sps.append(jnp.where(causal, sp, 0.0) if masked else sp)
        sss = [jnp.dot(sps[h].astype(BF16), suffix, preferred_element_type=F32) for h in heads]
        out = []
        for h in heads:
            carry, acc = state[2 * h], state[2 * h + 1]
            w = jnp.exp(zs[h] - sps[h] - sss[h] - carry)
            if masked:
                w = jnp.where(causal, w, 0.0)
            acc = acc + jnp.dot(w.astype(BF16), v_ref[pl.ds(ks, tq), h * d:(h + 1) * d],
                                preferred_element_type=F32)
            out.extend((carry + jnp.sum(sps[h], axis=-1, keepdims=True), acc))
        return tuple(out)

    state = blocks(qi, (jnp.zeros((tq, 1), F32), jnp.zeros((tq, d), F32)) * hp, True)
    state = lax.fori_loop(0, qi, lambda jj, c: blocks(qi - 1 - jj, c, False), state)
    for h in range(hp):
        o_ref[:, h * d:(h + 1) * d] = state[2 * h + 1].astype(o_ref.dtype)


def _attention(proj3, q_blk, k_blk, v_blk, n_heads, *, tq, hp):
    b, l, _ = proj3.shape
    d = HEAD_DIM
    return pl.pallas_call(
        functools.partial(_attn_kernel, tq=tq, hp=hp),
        out_shape=jax.ShapeDtypeStruct((b, l, n_heads * d), BF16),
        grid=(b, n_heads // hp, l // tq),
        in_specs=[pl.BlockSpec((None, tq, hp * d), lambda bi, hi, qi: (bi, qi, q_blk // hp + hi)),
                  pl.BlockSpec((None, l, hp * d), lambda bi, hi, qi: (bi, 0, k_blk // hp + hi)),
                  pl.BlockSpec((None, l, hp * d), lambda bi, hi, qi: (bi, 0, v_blk // hp + hi))],
        out_specs=pl.BlockSpec((None, tq, hp * d), lambda bi, hi, qi: (bi, qi, hi)),
        compiler_params=_cparams(("parallel", "parallel", "arbitrary")),
        name="stick_attn",
    )(proj3, proj3, proj3)


def _merge_kernel(ya_ref, yb_ref, yc_ref, wa_ref, wb_ref, wc_ref, ga_ref, gb_ref, gc_ref, bg_ref, o_ref):
    bg = bg_ref[...]
    out = None
    for n, (y_ref, w_ref, gl_ref) in enumerate(((ya_ref, wa_ref, ga_ref), (yb_ref, wb_ref, gb_ref),
                                                (yc_ref, wc_ref, gc_ref))):
        gate = _sigmoid(gl_ref[...].astype(F32) + bg[n:n + 1, :])
        term = gate * jnp.dot(y_ref[...], w_ref[...], preferred_element_type=F32)
        out = term if out is None else out + term
    o_ref[...] = out.astype(o_ref.dtype)


def _merge(ys, ws, proj, gate_col, b_gate, *, tm, tn):
    t, w = ys[0].shape
    d = ws[0].shape[1]
    gblk = gate_col // tn
    per = d // tn
    y_spec = pl.BlockSpec((tm, w), lambda i, j: (i, 0))
    w_spec = pl.BlockSpec((w, tn), lambda i, j: (0, j))
    gate_specs = [pl.BlockSpec((tm, tn), functools.partial(lambda i, j, n: (i, gblk + n * per + j), n=n))
                  for n in range(N_BRANCH)]
    return pl.pallas_call(
        _merge_kernel,
        out_shape=jax.ShapeDtypeStruct((t, d), BF16),
        grid=(t // tm, d // tn),
        in_specs=[y_spec] * 3 + [w_spec] * 3 + gate_specs
                 + [pl.BlockSpec((N_BRANCH, tn), lambda i, j: (0, j))],
        out_specs=pl.BlockSpec((tm, tn), lambda i, j: (i, j)),
        compiler_params=_cparams(("parallel", "arbitrary")),
        name="merge",
    )(*ys, *ws, proj, proj, proj, b_gate.reshape(N_BRANCH, d))


def _outproj_kernel(m_ref, w_ref, h_ref, g_ref, ho_ref, hn_ref):
    h = h_ref[...] + jnp.dot(m_ref[...], w_ref[...], preferred_element_type=F32)
    ho_ref[...] = h
    hn_ref[...] = _rms_rows(h, g_ref[...]).astype(hn_ref.dtype)


def _outproj(merged, w_out, h, g, *, tm):
    t, d = h.shape
    row = lambda i: (i, 0)
    return pl.pallas_call(
        _outproj_kernel,
        out_shape=(jax.ShapeDtypeStruct((t, d), F32), jax.ShapeDtypeStruct((t, d), BF16)),
        grid=(t // tm,),
        in_specs=[pl.BlockSpec((tm, d), row),
                  pl.BlockSpec((d, d), lambda i: (0, 0)),
                  pl.BlockSpec((tm, d), row),
                  pl.BlockSpec((1, d), lambda i: (0, 0))],
        out_specs=(pl.BlockSpec((tm, d), row), pl.BlockSpec((tm, d), row)),
        compiler_params=_cparams(("parallel",)),
        name="outproj",
    )(merged, w_out, h, g.reshape(1, d))


def _mlp_kernel(x_ref, wu_ref, wd_ref, h_ref, g_ref, o_ref, *, final_norm, parts):
    f = pl.program_id(1)

    @pl.when(f == 0)
    def _():
        o_ref[...] = h_ref[...]

    x = x_ref[...]
    part = wu_ref.shape[1] // parts
    hids = [jnp.maximum(jnp.dot(x, wu_ref[:, n * part:(n + 1) * part], preferred_element_type=F32), 0.0)
            for n in range(parts)]
    hid = jnp.concatenate([(hd * hd).astype(BF16) for hd in hids], axis=1)
    o_ref[...] += jnp.dot(hid, wd_ref[...], preferred_element_type=F32)

    if final_norm:
        @pl.when(f == pl.num_programs(1) - 1)
        def _():
            o_ref[...] = _rms_rows(o_ref[...], g_ref[...])


def _mlp(hn, w_up, w_down, h, g, *, tm, tf, final_norm):
    t, d = h.shape
    ff = w_up.shape[1]
    row = lambda i, f: (i, 0)
    return pl.pallas_call(
        functools.partial(_mlp_kernel, final_norm=final_norm, parts=2),
        out_shape=jax.ShapeDtypeStruct((t, d), F32),
        grid=(t // tm, ff // tf),
        in_specs=[pl.BlockSpec((tm, d), row),
                  pl.BlockSpec((d, tf), lambda i, f: (0, f)),
                  pl.BlockSpec((tf, d), lambda i, f: (f, 0)),
                  pl.BlockSpec((tm, d), row),
                  pl.BlockSpec((1, d), lambda i, f: (0, 0))],
        out_specs=pl.BlockSpec((tm, d), row),
        compiler_params=_cparams(("parallel", "arbitrary")),
        name="mlp",
    )(hn, w_up, w_down, h, g.reshape(1, d))


def _layer(h, bsz, seq, p, *, final_g):
    t, d = h.shape
    s5_w = p["s5_d"].shape[0]
    lru_w = p["lru_lambda"].shape[0]
    attn_w = p["w_br_attn"].shape[0]
    n_heads = attn_w // HEAD_DIM
    off_lru = s5_w
    off_q = off_lru + lru_w
    off_k = off_q + attn_w
    off_v = off_k + attn_w
    off_gate = off_v + attn_w

    col = jnp.arange(p["w_in"].shape[1])
    q_scale = jnp.where((col >= off_q) & (col < off_k), HEAD_DIM ** -0.5, 1.0).astype(F32)
    proj, u_s5 = _inproj(h, p["norm_mix_g"], (p["w_in"] * q_scale[None, :]).astype(BF16), tm=1024, tn=s5_w)
    proj3 = proj.reshape(bsz, seq, proj.shape[1])

    tables = _s5_tables(p["s5_lam_re"], p["s5_lam_im"], p["s5_log_dt"], p["s5_b_re"], p["s5_b_im"],
                        p["s5_c_re"], p["s5_c_im"])
    y_ssm = _s5_scan(u_s5, tables, seq=seq)
    y_s5 = _s5_post(y_ssm, u_s5, p["s5_d"], p["s5_w_glu"].astype(BF16), p["s5_b_glu"], tm=1024)

    y_lru = _lru(proj3, off_lru // lru_w, p["lru_conv_w"], p["lru_conv_b"], p["lru_w_r"], p["lru_b_r"],
                 p["lru_w_i"], p["lru_b_i"], p["lru_lambda"], tl=512).reshape(t, lru_w)

    y_attn = _attention(proj3, off_q // HEAD_DIM, off_k // HEAD_DIM, off_v // HEAD_DIM, n_heads,
                        tq=256, hp=8).reshape(t, attn_w)

    merged = _merge((y_s5, y_lru, y_attn),
                    (p["w_br_s5"].astype(BF16), p["w_br_lru"].astype(BF16), p["w_br_attn"].astype(BF16)),
                    proj, off_gate, p["b_gate"], tm=1024, tn=512)
    h, hn = _outproj(merged, p["w_out"].astype(BF16), h, p["norm_mlp_g"], tm=512)
    g_last = p["norm_mlp_g"] if final_g is None else final_g
    return _mlp(hn, p["w_up"].astype(BF16), p["w_down"].astype(BF16), h, g_last,
                tm=512, tf=1024, final_norm=final_g is not None)


_LAYER_PARAMS = ("norm_mix_g", "w_in", "b_gate", "s5_lam_re", "s5_lam_im", "s5_log_dt", "s5_b_re", "s5_b_im",
                 "s5_c_re", "s5_c_im", "s5_d", "s5_w_glu", "s5_b_glu", "lru_conv_w", "lru_conv_b", "lru_w_r",
                 "lru_b_r", "lru_w_i", "lru_b_i", "lru_lambda", "w_br_s5", "w_br_lru", "w_br_attn", "w_out",
                 "norm_mlp_g", "w_up", "w_down")


def kernel(x, norm_mix_g, w_in, b_gate, s5_lam_re, s5_lam_im, s5_log_dt, s5_b_re, s5_b_im, s5_c_re, s5_c_im,
           s5_d, s5_w_glu, s5_b_glu, lru_conv_w, lru_conv_b, lru_w_r, lru_b_r, lru_w_i, lru_b_i, lru_lambda,
           w_br_s5, w_br_lru, w_br_attn, w_out, norm_mlp_g, w_up, w_down, final_norm_g):
    stacked = dict(zip(_LAYER_PARAMS, (norm_mix_g, w_in, b_gate, s5_lam_re, s5_lam_im, s5_log_dt, s5_b_re,
                                       s5_b_im, s5_c_re, s5_c_im, s5_d, s5_w_glu, s5_b_glu, lru_conv_w,
                                       lru_conv_b, lru_w_r, lru_b_r, lru_w_i, lru_b_i, lru_lambda, w_br_s5,
                                       w_br_lru, w_br_attn, w_out, norm_mlp_g, w_up, w_down)))
    bsz, seq, d = x.shape
    depth = w_in.shape[0]
    h = x.reshape(bsz * seq, d).astype(F32)
    for layer in range(depth):
        p = {k: v[layer] for k, v in stacked.items()}
        h = _layer(h, bsz, seq, p, final_g=final_norm_g if layer == depth - 1 else None)
    return h.reshape(bsz, seq, d).astype(x.dtype)
```

```python
import functools
import math

import jax
import jax.numpy as jnp
from jax import lax
from jax.experimental import pallas as pl
from jax.experimental.pallas import tpu as pltpu

F32 = jnp.float32
BF16 = jnp.bfloat16

EPS = 1e-6
LANES = 128
HEAD_DIM = 128
S5_GROUP = 16
S5_CHUNK = 16
LRU_BLOCK = 64
LRU_SUPER = 256
LRU_C = 8.0
N_BRANCH = 3
LOG2E = 1.4426950408889634
VMEM_LIMIT = 56 * 1024 * 1024
HIGHEST = lax.Precision.HIGHEST


def _cparams(sem):
    return pltpu.CompilerParams(dimension_semantics=sem, vmem_limit_bytes=VMEM_LIMIT)


def _softplus(z):
    return jnp.maximum(z, 0.0) + jnp.log(1.0 + jnp.exp2(jnp.abs(z) * (-LOG2E)))


def _sigmoid(z):
    return 1.0 / (1.0 + jnp.exp(-z))


def _rms_rows(x, g):
    ms = jnp.mean(x * x, axis=-1, keepdims=True)
    return (x * lax.rsqrt(ms + EPS)) * g


def _inproj_kernel(x_ref, g_ref, w_ref, o_ref, o32_ref, xn_ref, *, row_chunk):
    j = pl.program_id(1)

    @pl.when(j == 0)
    def _():
        g = g_ref[...]
        w = w_ref[...]
        for r in range(0, x_ref.shape[0], row_chunk):
            xn = _rms_rows(x_ref[r:r + row_chunk, :], g).astype(BF16)
            xn_ref[r:r + row_chunk, :] = xn
            acc = jnp.dot(xn, w, preferred_element_type=F32)
            o_ref[r:r + row_chunk, :] = acc.astype(o_ref.dtype)
            o32_ref[r:r + row_chunk, :] = acc

    @pl.when(j > 0)
    def _():
        o_ref[...] = jnp.dot(xn_ref[...], w_ref[...], preferred_element_type=F32).astype(o_ref.dtype)


def _inproj(h, g, w, *, tm, tn):
    t, d = h.shape
    n = w.shape[1]
    return pl.pallas_call(
        functools.partial(_inproj_kernel, row_chunk=256),
        out_shape=(jax.ShapeDtypeStruct((t, n), BF16), jax.ShapeDtypeStruct((t, tn), F32)),
        grid=(t // tm, n // tn),
        in_specs=[pl.BlockSpec((tm, d), lambda i, j: (i, 0)),
                  pl.BlockSpec((1, d), lambda i, j: (0, 0)),
                  pl.BlockSpec((d, tn), lambda i, j: (0, j))],
        out_specs=(pl.BlockSpec((tm, tn), lambda i, j: (i, j)),
                   pl.BlockSpec((tm, tn), lambda i, j: (i, 0))),
        scratch_shapes=[pltpu.VMEM((tm, d), BF16)],
        compiler_params=_cparams(("parallel", "arbitrary")),
        name="inproj",
    )(h, g.reshape(1, d), w)


def _s5_kernel(u_ref, g2_ref, ws_ref, wrt_ref, lr_ref, li_ref, o_ref,
               xre_ref, xim_ref, hre_ref, him_ref, *, tc):
    rows = u_ref.shape[0] // tc
    lanes = u_ref.shape[1]
    xc = jnp.concatenate([u_ref[pl.ds(t, rows, stride=tc), :].astype(BF16) for t in range(tc)], axis=1)
    lr = lr_ref[...]
    li = li_ref[...]
    sp = lr.shape[1]
    x = jnp.dot(xc, ws_ref[...], preferred_element_type=F32)
    xre_ref[...] = x[:, :sp]
    xim_ref[...] = x[:, sp:]

    def body(k, carry):
        hr, hi = carry
        base = pl.multiple_of(k * 8, 8)
        for r in range(8):
            hre_ref[pl.ds(base + r, 1), :] = hr
            him_ref[pl.ds(base + r, 1), :] = hi
            xr = xre_ref[pl.ds(base + r, 1), :]
            xi = xim_ref[pl.ds(base + r, 1), :]
            hr, hi = lr * hr - li * hi + xr, lr * hi + li * hr + xi
        return hr, hi

    zero = jnp.zeros(lr.shape, F32)
    lax.fori_loop(0, rows // 8, body, (zero, zero))

    h0 = jnp.concatenate([hre_ref[...].astype(BF16), him_ref[...].astype(BF16)], axis=1)
    yr = lax.dot_general(h0, wrt_ref[...], (((1,), (1,)), ((), ())), preferred_element_type=F32)
    nblk = g2_ref.shape[0] // lanes
    for m in range(tc // 2):
        kdim = (2 * m + 2) * lanes
        y = jnp.dot(xc[:, :kdim], g2_ref[(nblk * lanes - kdim):, :], preferred_element_type=F32)
        y = y + yr[:, 2 * m * lanes:(2 * m + 2) * lanes]
        o_ref[pl.ds(2 * m, rows, stride=tc), :] = y[:, :lanes]
        o_ref[pl.ds(2 * m + 1, rows, stride=tc), :] = y[:, lanes:]


def _lane_diag_kernel(a_ref, b_ref, o_ref, *, gt, c):
    rows, sub = a_ref.shape
    width = gt * sub
    src = lax.broadcasted_iota(jnp.int32, (sub, width), 0)
    dst = lax.broadcasted_iota(jnp.int32, (sub, width), 1)
    spread = jnp.where(dst % sub == src, 1.0, 0.0).astype(BF16)
    row_g = (lax.broadcasted_iota(jnp.int32, (rows, width), 0) // c) % gt
    col_g = lax.broadcasted_iota(jnp.int32, (rows, width), 1) // sub
    keep = row_g == col_g
    for n, x_ref in enumerate((a_ref, b_ref)):
        tiled = jnp.dot(x_ref[...].astype(BF16), spread, preferred_element_type=F32)
        o_ref[:, n * width:(n + 1) * width] = jnp.where(keep, tiled, 0.0).astype(o_ref.dtype)


def _lane_diag(a, b, *, gt, c):
    nt, rows, sub = a.shape
    spec = pl.BlockSpec((None, rows, sub), lambda k: (k, 0, 0))
    return pl.pallas_call(
        functools.partial(_lane_diag_kernel, gt=gt, c=c),
        out_shape=jax.ShapeDtypeStruct((nt, rows, 2 * gt * sub), BF16),
        grid=(nt,),
        in_specs=[spec, spec],
        out_specs=pl.BlockSpec((None, rows, 2 * gt * sub), lambda k: (k, 0, 0)),
        compiler_params=_cparams(("parallel",)),
        name="s5_lane_diag",
    )(a, b)


def _s5_tables(lam_re, lam_im, log_dt, b_re, b_im, c_re, c_im):
    tc = S5_CHUNK
    g, p = lam_re.shape
    c = b_re.shape[-1]
    lr = lam_re.astype(F32)
    li = lam_im.astype(F32)
    dt = jnp.exp(log_dt.astype(F32))[:, None]
    ar, ai = lr * dt, li * dt
    ks = jnp.arange(tc + 1, dtype=F32)[None, :, None]
    mag = jnp.exp(ks * ar[:, None, :])
    pr = mag * jnp.cos(ks * ai[:, None, :])
    pi = mag * jnp.sin(ks * ai[:, None, :])
    nr = jnp.expm1(ar) * jnp.cos(ai) - 2.0 * jnp.sin(0.5 * ai) ** 2
    ni = jnp.exp(ar) * jnp.sin(ai)
    den = lr * lr + li * li
    fr = (nr * lr + ni * li) / den
    fi = (ni * lr - nr * li) / den
    bbr = fr[..., None] * b_re - fi[..., None] * b_im
    bbi = fr[..., None] * b_im + fi[..., None] * b_re
    er = pr[..., None] * bbr[:, None] - pi[..., None] * bbi[:, None]
    ei = pr[..., None] * bbi[:, None] + pi[..., None] * bbr[:, None]
    kk = (jnp.einsum('gop,gkpi->gkoi', c_re, er[:, :tc], precision=HIGHEST)
          - jnp.einsum('gop,gkpi->gkoi', c_im, ei[:, :tc], precision=HIGHEST))
    gt = LANES // c
    nt = g // gt

    def tile_rows(x):
        a, sub = x.shape[1], x.shape[-1]
        return x.reshape(nt, gt, a, c, sub).transpose(0, 2, 1, 3, 4).reshape(nt, a * gt * c, sub)

    kt = kk.transpose(0, 1, 3, 2)[:, ::-1]
    zblk = jnp.zeros_like(kt[:, :1])
    g2 = _lane_diag(tile_rows(jnp.concatenate([kt, zblk], axis=1)),
                    tile_rows(jnp.concatenate([zblk, kt], axis=1)), gt=gt, c=c)
    ws = _lane_diag(tile_rows(er[:, tc - 1::-1].transpose(0, 1, 3, 2)),
                    tile_rows(ei[:, tc - 1::-1].transpose(0, 1, 3, 2)), gt=gt, c=c)
    pr1, pi1 = pr[:, 1:, None, :], pi[:, 1:, None, :]
    qr = c_re[:, None] * pr1 - c_im[:, None] * pi1
    qi = c_re[:, None] * pi1 + c_im[:, None] * pr1
    wrt = _lane_diag(tile_rows(qr), tile_rows(-qi), gt=gt, c=c)
    return (g2, ws, wrt, pr[:, tc].reshape(nt, 1, gt * p), pi[:, tc].reshape(nt, 1, gt * p))


def _s5_scan(u, tables, *, seq):
    t, w = u.shape
    g2, ws, wrt, lr, li = tables
    nt, sp = lr.shape[0], lr.shape[-1]
    rows = seq // S5_CHUNK
    tile = lambda k, b: (k, 0, 0)
    return pl.pallas_call(
        functools.partial(_s5_kernel, tc=S5_CHUNK),
        out_shape=jax.ShapeDtypeStruct((t, w), F32),
        grid=(nt, t // seq),
        in_specs=[pl.BlockSpec((seq, LANES), lambda k, b: (b, k)),
                  pl.BlockSpec((None,) + g2.shape[1:], tile),
                  pl.BlockSpec((None,) + ws.shape[1:], tile),
                  pl.BlockSpec((None,) + wrt.shape[1:], tile),
                  pl.BlockSpec((None, 1, sp), tile),
                  pl.BlockSpec((None, 1, sp), tile)],
        out_specs=pl.BlockSpec((seq, LANES), lambda k, b: (b, k)),
        scratch_shapes=[pltpu.VMEM((rows, sp), F32)] * 4,
        compiler_params=_cparams(("parallel", "parallel")),
        name="s5_scan",
    )(u, g2, ws, wrt, lr, li)


def _s5_post_kernel(y_ref, u_ref, d_ref, w_ref, b_ref, o_ref):
    y = y_ref[...] + d_ref[...] * u_ref[...].astype(F32)
    c0 = math.sqrt(2.0 / math.pi)
    y = 0.5 * y * (1.0 + jnp.tanh(c0 * (y + 0.044715 * (y * y * y))))
    gate = jnp.dot(y.astype(BF16), w_ref[...], preferred_element_type=F32) + b_ref[...]
    o_ref[...] = (y * _sigmoid(gate)).astype(o_ref.dtype)


def _s5_post(y, proj, d_skip, w_glu, b_glu, *, tm):
    t, w = y.shape
    return pl.pallas_call(
        _s5_post_kernel,
        out_shape=jax.ShapeDtypeStruct((t, w), BF16),
        grid=(t // tm,),
        in_specs=[pl.BlockSpec((tm, w), lambda i: (i, 0)),
                  pl.BlockSpec((tm, w), lambda i: (i, 0)),
                  pl.BlockSpec((1, w), lambda i: (0, 0)),
                  pl.BlockSpec((w, w), lambda i: (0, 0)),
                  pl.BlockSpec((1, w), lambda i: (0, 0))],
        out_specs=pl.BlockSpec((tm, w), lambda i: (i, 0)),
        compiler_params=_cparams(("parallel",)),
        name="s5_post",
    )(y, proj, d_skip.reshape(1, w), w_glu, b_glu.reshape(1, w))


def _lru_kernel(x_ref, cw_ref, cb_ref, wr_ref, br_ref, wi_ref, bi_ref, lam_ref, o_ref,
                xbuf_ref, a_ref, g_ref, h_ref, *, tl, kw):
    pad = 8
    w = x_ref.shape[1]

    @pl.when(pl.program_id(1) == 0)
    def _():
        xbuf_ref[pl.ds(0, pad), :] = jnp.zeros((pad, w), F32)
        h_ref[...] = jnp.zeros_like(h_ref)

    @pl.when(pl.program_id(1) > 0)
    def _():
        xbuf_ref[pl.ds(0, pad), :] = xbuf_ref[pl.ds(tl, pad), :]

    xbuf_ref[pl.ds(pad, tl), :] = x_ref[...].astype(F32)
    cw = cw_ref[...]
    xc = cb_ref[...] + xbuf_ref[pl.ds(pad - (kw - 1), tl), :] * cw[0:1, :]
    for k in range(1, kw):
        xc = xc + xbuf_ref[pl.ds(pad - (kw - 1) + k, tl), :] * cw[k:k + 1, :]
    xcb = xc.astype(BF16)
    nsup = w // LRU_SUPER
    rs, is_ = [], []
    for s in range(nsup):
        xs = xcb[:, s * LRU_SUPER:(s + 1) * LRU_SUPER]
        rs.append(jnp.dot(xs, wr_ref[s], preferred_element_type=F32))
        is_.append(jnp.dot(xs, wi_ref[s], preferred_element_type=F32))
    r = _sigmoid(jnp.concatenate(rs, axis=1) + br_ref[...])
    i = _sigmoid(jnp.concatenate(is_, axis=1) + bi_ref[...])
    log_a = (-LRU_C) * r * _softplus(-lam_ref[...])
    a_ref[...] = jnp.exp(log_a)
    th = jnp.tanh(log_a)
    g_ref[...] = jnp.sqrt((-2.0 * th) / (1.0 - th)) * (i * xc)

    row8 = lax.broadcasted_iota(jnp.int32, (8, w), 0)

    def body(t8, h):
        base = pl.multiple_of(t8 * 8, 8)
        a = a_ref[pl.ds(base, 8), :]
        b = g_ref[pl.ds(base, 8), :]
        for sh in (1, 2, 4):
            ok = row8 >= sh
            a_prev = jnp.where(ok, pltpu.roll(a, sh, axis=0), 1.0)
            b_prev = jnp.where(ok, pltpu.roll(b, sh, axis=0), 0.0)
            b = a * b_prev + b
            a = a * a_prev
        hs = a * h + b
        g_ref[pl.ds(base, 8), :] = hs
        return hs[7:8, :]

    h_ref[...] = lax.fori_loop(0, tl // 8, body, h_ref[...])
    o_ref[...] = g_ref[...].astype(o_ref.dtype)


def _block_diag_super(wblk):
    n, k, _ = wblk.shape
    per = LRU_SUPER // k
    wb = wblk.reshape(n // per, per, k, k)
    eye = jnp.eye(per, dtype=wblk.dtype)
    sup = wb[:, :, :, None, :] * eye[None, :, None, :, None]
    return sup.reshape(n // per, LRU_SUPER, LRU_SUPER)


def _lru(proj3, col_block, conv_w, conv_b, w_r, b_r, w_i, b_i, lam, *, tl):
    b, l, _ = proj3.shape
    w = conv_w.shape[1]
    kw = conv_w.shape[0]
    wr = _block_diag_super(w_r).astype(BF16)
    wi = _block_diag_super(w_i).astype(BF16)
    vec = lambda bi, ti: (0, 0)
    full3 = lambda bi, ti: (0, 0, 0)
    return pl.pallas_call(
        functools.partial(_lru_kernel, tl=tl, kw=kw),
        out_shape=jax.ShapeDtypeStruct((b, l, w), BF16),
        grid=(b, l // tl),
        in_specs=[pl.BlockSpec((None, tl, w), lambda bi, ti: (bi, ti, col_block)),
                  pl.BlockSpec((kw, w), vec),
                  pl.BlockSpec((1, w), vec),
                  pl.BlockSpec(wr.shape, full3),
                  pl.BlockSpec((1, w), vec),
                  pl.BlockSpec(wi.shape, full3),
                  pl.BlockSpec((1, w), vec),
                  pl.BlockSpec((1, w), vec)],
        out_specs=pl.BlockSpec((None, tl, w), lambda bi, ti: (bi, ti, 0)),
        scratch_shapes=[pltpu.VMEM((tl + 8, w), F32), pltpu.VMEM((tl, w), F32),
                        pltpu.VMEM((tl, w), F32), pltpu.VMEM((1, w), F32)],
        compiler_params=_cparams(("parallel", "arbitrary")),
        name="rglru",
    )(proj3, conv_w, conv_b.reshape(1, w), wr, b_r.reshape(1, w), wi, b_i.reshape(1, w), lam.reshape(1, w))


def _attn_kernel(q_ref, k_ref, v_ref, o_ref, *, tq, hp):
    qi = pl.program_id(2)
    d = HEAD_DIM
    row = lax.broadcasted_iota(jnp.int32, (tq, tq), 0)
    col = lax.broadcasted_iota(jnp.int32, (tq, tq), 1)
    suffix = jnp.where(row > col, 1.0, 0.0).astype(BF16)
    causal = col < row
    qs = [q_ref[:, h * d:(h + 1) * d] for h in range(hp)]

    def blocks(j, state, masked):
        ks = pl.multiple_of(j * tq, tq)
        heads = range(hp)
        zs = [lax.dot_general(qs[h], k_ref[pl.ds(ks, tq), h * d:(h + 1) * d], (((1,), (1,)), ((), ())),
                              preferred_element_type=F32) for h in heads]
        sps = []
        for h in heads:
            sp = _softplus(zs[h])
            sps.append(jnp.where(causal, sp, 0.0) if masked else sp)
        sss = [jnp.dot(sps[h].astype(BF16), suffix, preferred_element_type=F32) for h in heads]
        out = []
        for h in heads:
            carry, acc = state[2 * h], state[2 * h + 1]
            w = jnp.exp(zs[h] - sps[h] - sss[h] - carry)
            if masked:
                w = jnp.where(causal, w, 0.0)
            acc = acc + jnp.dot(w.astype(BF16), v_ref[pl.ds(ks, tq), h * d:(h + 1) * d],
                                preferred_element_type=F32)
            out.extend((carry + jnp.sum(sps[h], axis=-1, keepdims=True), acc))
        return tuple(out)

    state = blocks(qi, (jnp.zeros((tq, 1), F32), jnp.zeros((tq, d), F32)) * hp, True)
    state = lax.fori_loop(0, qi, lambda jj, c: blocks(qi - 1 - jj, c, False), state)
    for h in range(hp):
        o_ref[:, h * d:(h + 1) * d] = state[2 * h + 1].astype(o_ref.dtype)


def _attention(proj3, q_blk, k_blk, v_blk, n_heads, *, tq, hp):
    b, l, _ = proj3.shape
    d = HEAD_DIM
    return pl.pallas_call(
        functools.partial(_attn_kernel, tq=tq, hp=hp),
        out_shape=jax.ShapeDtypeStruct((b, l, n_heads * d), BF16),
        grid=(b, n_heads // hp, l // tq),
        in_specs=[pl.BlockSpec((None, tq, hp * d), lambda bi, hi, qi: (bi, qi, q_blk // hp + hi)),
                  pl.BlockSpec((None, l, hp * d), lambda bi, hi, qi: (bi, 0, k_blk // hp + hi)),
                  pl.BlockSpec((None, l, hp * d), lambda bi, hi, qi: (bi, 0, v_blk // hp + hi))],
        out_specs=pl.BlockSpec((None, tq, hp * d), lambda bi, hi, qi: (bi, qi, hi)),
        compiler_params=_cparams(("parallel", "parallel", "arbitrary")),
        name="stick_attn",
    )(proj3, proj3, proj3)


def _merge_kernel(ya_ref, yb_ref, yc_ref, wa_ref, wb_ref, wc_ref, ga_ref, gb_ref, gc_ref, bg_ref, o_ref):
    bg = bg_ref[...]
    out = None
    for n, (y_ref, w_ref, gl_ref) in enumerate(((ya_ref, wa_ref, ga_ref), (yb_ref, wb_ref, gb_ref),
                                                (yc_ref, wc_ref, gc_ref))):
        gate = _sigmoid(gl_ref[...].astype(F32) + bg[n:n + 1, :])
        term = gate * jnp.dot(y_ref[...], w_ref[...], preferred_element_type=F32)
        out = term if out is None else out + term
    o_ref[...] = out.astype(o_ref.dtype)


def _merge(ys, ws, proj, gate_col, b_gate, *, tm, tn):
    t, w = ys[0].shape
    d = ws[0].shape[1]
    gblk = gate_col // tn
    per = d // tn
    y_spec = pl.BlockSpec((tm, w), lambda i, j: (i, 0))
    w_spec = pl.BlockSpec((w, tn), lambda i, j: (0, j))
    gate_specs = [pl.BlockSpec((tm, tn), functools.partial(lambda i, j, n: (i, gblk + n * per + j), n=n))
                  for n in range(N_BRANCH)]
    return pl.pallas_call(
        _merge_kernel,
        out_shape=jax.ShapeDtypeStruct((t, d), BF16),
        grid=(t // tm, d // tn),
        in_specs=[y_spec] * 3 + [w_spec] * 3 + gate_specs
                 + [pl.BlockSpec((N_BRANCH, tn), lambda i, j: (0, j))],
        out_specs=pl.BlockSpec((tm, tn), lambda i, j: (i, j)),
        compiler_params=_cparams(("parallel", "arbitrary")),
        name="merge",
    )(*ys, *ws, proj, proj, proj, b_gate.reshape(N_BRANCH, d))


def _outproj_kernel(m_ref, w_ref, h_ref, g_ref, ho_ref, hn_ref):
    h = h_ref[...] + jnp.dot(m_ref[...], w_ref[...], preferred_element_type=F32)
    ho_ref[...] = h
    hn_ref[...] = _rms_rows(h, g_ref[...]).astype(hn_ref.dtype)


def _outproj(merged, w_out, h, g, *, tm):
    t, d = h.shape
    row = lambda i: (i, 0)
    return pl.pallas_call(
        _outproj_kernel,
        out_shape=(jax.ShapeDtypeStruct((t, d), F32), jax.ShapeDtypeStruct((t, d), BF16)),
        grid=(t // tm,),
        in_specs=[pl.BlockSpec((tm, d), row),
                  pl.BlockSpec((d, d), lambda i: (0, 0)),
                  pl.BlockSpec((tm, d), row),
                  pl.BlockSpec((1, d), lambda i: (0, 0))],
        out_specs=(pl.BlockSpec((tm, d), row), pl.BlockSpec((tm, d), row)),
        compiler_params=_cparams(("parallel",)),
        name="outproj",
    )(merged, w_out, h, g.reshape(1, d))


def _mlp_kernel(x_ref, wu_ref, wd_ref, h_ref, g_ref, o_ref, *, final_norm, parts):
    f = pl.program_id(1)

    @pl.when(f == 0)
    def _():
        o_ref[...] = h_ref[...]

    x = x_ref[...]
    part = wu_ref.shape[1] // parts
    hids = [jnp.maximum(jnp.dot(x, wu_ref[:, n * part:(n + 1) * part], preferred_element_type=F32), 0.0)
            for n in range(parts)]
    hid = jnp.concatenate([(hd * hd).astype(BF16) for hd in hids], axis=1)
    o_ref[...] += jnp.dot(hid, wd_ref[...], preferred_element_type=F32)

    if final_norm:
        @pl.when(f == pl.num_programs(1) - 1)
        def _():
            o_ref[...] = _rms_rows(o_ref[...], g_ref[...])


def _mlp(hn, w_up, w_down, h, g, *, tm, tf, final_norm):
    t, d = h.shape
    ff = w_up.shape[1]
    row = lambda i, f: (i, 0)
    return pl.pallas_call(
        functools.partial(_mlp_kernel, final_norm=final_norm, parts=2),
        out_shape=jax.ShapeDtypeStruct((t, d), F32),
        grid=(t // tm, ff // tf),
        in_specs=[pl.BlockSpec((tm, d), row),
                  pl.BlockSpec((d, tf), lambda i, f: (0, f)),
                  pl.BlockSpec((tf, d), lambda i, f: (f, 0)),
                  pl.BlockSpec((tm, d), row),
                  pl.BlockSpec((1, d), lambda i, f: (0, 0))],
        out_specs=pl.BlockSpec((tm, d), row),
        compiler_params=_cparams(("parallel", "arbitrary")),
        name="mlp",
    )(hn, w_up, w_down, h, g.reshape(1, d))


def _layer(h, bsz, seq, p, *, final_g):
    t, d = h.shape
    s5_w = p["s5_d"].shape[0]
    lru_w = p["lru_lambda"].shape[0]
    attn_w = p["w_br_attn"].shape[0]
    n_heads = attn_w // HEAD_DIM
    off_lru = s5_w
    off_q = off_lru + lru_w
    off_k = off_q + attn_w
    off_v = off_k + attn_w
    off_gate = off_v + attn_w

    col = jnp.arange(p["w_in"].shape[1])
    q_scale = jnp.where((col >= off_q) & (col < off_k), HEAD_DIM ** -0.5, 1.0).astype(F32)
    proj, u_s5 = _inproj(h, p["norm_mix_g"], (p["w_in"] * q_scale[None, :]).astype(BF16), tm=1024, tn=s5_w)
    proj3 = proj.reshape(bsz, seq, proj.shape[1])

    tables = _s5_tables(p["s5_lam_re"], p["s5_lam_im"], p["s5_log_dt"], p["s5_b_re"], p["s5_b_im"],
                        p["s5_c_re"], p["s5_c_im"])
    y_ssm = _s5_scan(u_s5, tables, seq=seq)
    y_s5 = _s5_post(y_ssm, u_s5, p["s5_d"], p["s5_w_glu"].astype(BF16), p["s5_b_glu"], tm=1024)

    y_lru = _lru(proj3, off_lru // lru_w, p["lru_conv_w"], p["lru_conv_b"], p["lru_w_r"], p["lru_b_r"],
                 p["lru_w_i"], p["lru_b_i"], p["lru_lambda"], tl=512).reshape(t, lru_w)

    y_attn = _attention(proj3, off_q // HEAD_DIM, off_k // HEAD_DIM, off_v // HEAD_DIM, n_heads,
                        tq=256, hp=8).reshape(t, attn_w)

    merged = _merge((y_s5, y_lru, y_attn),
                    (p["w_br_s5"].astype(BF16), p["w_br_lru"].astype(BF16), p["w_br_attn"].astype(BF16)),
                    proj, off_gate, p["b_gate"], tm=1024, tn=512)
    h, hn = _outproj(merged, p["w_out"].astype(BF16), h, p["norm_mlp_g"], tm=512)
    g_last = p["norm_mlp_g"] if final_g is None else final_g
    return _mlp(hn, p["w_up"].astype(BF16), p["w_down"].astype(BF16), h, g_last,
                tm=512, tf=1024, final_norm=final_g is not None)


_LAYER_PARAMS = ("norm_mix_g", "w_in", "b_gate", "s5_lam_re", "s5_lam_im", "s5_log_dt", "s5_b_re", "s5_b_im",
                 "s5_c_re", "s5_c_im", "s5_d", "s5_w_glu", "s5_b_glu", "lru_conv_w", "lru_conv_b", "lru_w_r",
                 "lru_b_r", "lru_w_i", "lru_b_i", "lru_lambda", "w_br_s5", "w_br_lru", "w_br_attn", "w_out",
                 "norm_mlp_g", "w_up", "w_down")


def kernel(x, norm_mix_g, w_in, b_gate, s5_lam_re, s5_lam_im, s5_log_dt, s5_b_re, s5_b_im, s5_c_re, s5_c_im,
           s5_d, s5_w_glu, s5_b_glu, lru_conv_w, lru_conv_b, lru_w_r, lru_b_r, lru_w_i, lru_b_i, lru_lambda,
           w_br_s5, w_br_lru, w_br_attn, w_out, norm_mlp_g, w_up, w_down, final_norm_g):
    stacked = dict(zip(_LAYER_PARAMS, (norm_mix_g, w_in, b_gate, s5_lam_re, s5_lam_im, s5_log_dt, s5_b_re,
                                       s5_b_im, s5_c_re, s5_c_im, s5_d, s5_w_glu, s5_b_glu, lru_conv_w,
                                       lru_conv_b, lru_w_r, lru_b_r, lru_w_i, lru_b_i, lru_lambda, w_br_s5,
                                       w_br_lru, w_br_attn, w_out, norm_mlp_g, w_up, w_down)))
    bsz, seq, d = x.shape
    depth = w_in.shape[0]
    h = x.reshape(bsz * seq, d).astype(F32)
    for layer in range(depth):
        p = {k: v[layer] for k, v in stacked.items()}
        h = _layer(h, bsz, seq, p, final_g=final_norm_g if layer == depth - 1 else None)
    return h.reshape(bsz, seq, d).astype(x.dtype)
```

```python
import functools
import math

import jax
import jax.numpy as jnp
from jax import lax
from jax.experimental import pallas as pl
from jax.experimental.pallas import tpu as pltpu

F32 = jnp.float32
BF16 = jnp.bfloat16

EPS = 1e-6
LANES = 128
HEAD_DIM = 128
S5_GROUP = 16
S5_CHUNK = 16
LRU_BLOCK = 64
LRU_SUPER = 256
LRU_C = 8.0
N_BRANCH = 3
LOG2E = 1.4426950408889634
VMEM_LIMIT = 56 * 1024 * 1024
HIGHEST = lax.Precision.HIGHEST


def _cparams(sem):
    return pltpu.CompilerParams(dimension_semantics=sem, vmem_limit_bytes=VMEM_LIMIT)


def _softplus(z):
    return jnp.maximum(z, 0.0) + jnp.log(1.0 + jnp.exp2(jnp.abs(z) * (-LOG2E)))


def _sigmoid(z):
    return 1.0 / (1.0 + jnp.exp(-z))


def _rms_rows(x, g):
    ms = jnp.mean(x * x, axis=-1, keepdims=True)
    return (x * lax.rsqrt(ms + EPS)) * g


def _inproj_kernel(x_ref, g_ref, w_ref, o_ref, o32_ref, xn_ref, *, row_chunk):
    j = pl.program_id(1)

    @pl.when(j == 0)
    def _():
        g = g_ref[...]
        w = w_ref[...]
        for r in range(0, x_ref.shape[0], row_chunk):
            xn = _rms_rows(x_ref[r:r + row_chunk, :], g).astype(BF16)
            xn_ref[r:r + row_chunk, :] = xn
            acc = jnp.dot(xn, w, preferred_element_type=F32)
            o_ref[r:r + row_chunk, :] = acc.astype(o_ref.dtype)
            o32_ref[r:r + row_chunk, :] = acc

    @pl.when(j > 0)
    def _():
        o_ref[...] = jnp.dot(xn_ref[...], w_ref[...], preferred_element_type=F32).astype(o_ref.dtype)


def _inproj(h, g, w, *, tm, tn):
    t, d = h.shape
    n = w.shape[1]
    return pl.pallas_call(
        functools.partial(_inproj_kernel, row_chunk=256),
        out_shape=(jax.ShapeDtypeStruct((t, n), BF16), jax.ShapeDtypeStruct((t, tn), F32)),
        grid=(t // tm, n // tn),
        in_specs=[pl.BlockSpec((tm, d), lambda i, j: (i, 0)),
                  pl.BlockSpec((1, d), lambda i, j: (0, 0)),
                  pl.BlockSpec((d, tn), lambda i, j: (0, j))],
        out_specs=(pl.BlockSpec((tm, tn), lambda i, j: (i, j)),
                   pl.BlockSpec((tm, tn), lambda i, j: (i, 0))),
        scratch_shapes=[pltpu.VMEM((tm, d), BF16)],
        compiler_params=_cparams(("parallel", "arbitrary")),
        name="inproj",
    )(h, g.reshape(1, d), w)


def _s5_kernel(u_ref, g2_ref, ws_ref, wrt_ref, lr_ref, li_ref, o_ref,
               xre_ref, xim_ref, hre_ref, him_ref, *, tc):
    rows = u_ref.shape[0] // tc
    lanes = u_ref.shape[1]
    xc = jnp.concatenate([u_ref[pl.ds(t, rows, stride=tc), :].astype(BF16) for t in range(tc)], axis=1)
    lr = lr_ref[...]
    li = li_ref[...]
    sp = lr.shape[1]
    x = jnp.dot(xc, ws_ref[...], preferred_element_type=F32)
    xre_ref[...] = x[:, :sp]
    xim_ref[...] = x[:, sp:]

    def body(k, carry):
        hr, hi = carry
        base = pl.multiple_of(k * 8, 8)
        for r in range(8):
            hre_ref[pl.ds(base + r, 1), :] = hr
            him_ref[pl.ds(base + r, 1), :] = hi
            xr = xre_ref[pl.ds(base + r, 1), :]
            xi = xim_ref[pl.ds(base + r, 1), :]
            hr, hi = lr * hr - li * hi + xr, lr * hi + li * hr + xi
        return hr, hi

    zero = jnp.zeros(lr.shape, F32)
    lax.fori_loop(0, rows // 8, body, (zero, zero))

    h0 = jnp.concatenate([hre_ref[...].astype(BF16), him_ref[...].astype(BF16)], axis=1)
    yr = lax.dot_general(h0, wrt_ref[...], (((1,), (1,)), ((), ())), preferred_element_type=F32)
    nblk = g2_ref.shape[0] // lanes
    for m in range(tc // 2):
        kdim = (2 * m + 2) * lanes
        y = jnp.dot(xc[:, :kdim], g2_ref[(nblk * lanes - kdim):, :], preferred_element_type=F32)
        y = y + yr[:, 2 * m * lanes:(2 * m + 2) * lanes]
        o_ref[pl.ds(2 * m, rows, stride=tc), :] = y[:, :lanes]
        o_ref[pl.ds(2 * m + 1, rows, stride=tc), :] = y[:, lanes:]


def _lane_diag_kernel(a_ref, b_ref, o_ref, *, gt, c):
    rows, sub = a_ref.shape
    width = gt * sub
    src = lax.broadcasted_iota(jnp.int32, (sub, width), 0)
    dst = lax.broadcasted_iota(jnp.int32, (sub, width), 1)
    spread = jnp.where(dst % sub == src, 1.0, 0.0).astype(BF16)
    row_g = (lax.broadcasted_iota(jnp.int32, (rows, width), 0) // c) % gt
    col_g = lax.broadcasted_iota(jnp.int32, (rows, width), 1) // sub
    keep = row_g == col_g
    for n, x_ref in enumerate((a_ref, b_ref)):
        tiled = jnp.dot(x_ref[...].astype(BF16), spread, preferred_element_type=F32)
        o_ref[:, n * width:(n + 1) * width] = jnp.where(keep, tiled, 0.0).astype(o_ref.dtype)


def _lane_diag(a, b, *, gt, c):
    nt, rows, sub = a.shape
    spec = pl.BlockSpec((None, rows, sub), lambda k: (k, 0, 0))
    return pl.pallas_call(
        functools.partial(_lane_diag_kernel, gt=gt, c=c),
        out_shape=jax.ShapeDtypeStruct((nt, rows, 2 * gt * sub), BF16),
        grid=(nt,),
        in_specs=[spec, spec],
        out_specs=pl.BlockSpec((None, rows, 2 * gt * sub), lambda k: (k, 0, 0)),
        compiler_params=_cparams(("parallel",)),
        name="s5_lane_diag",
    )(a, b)


def _s5_tables(lam_re, lam_im, log_dt, b_re, b_im, c_re, c_im):
    tc = S5_CHUNK
    g, p = lam_re.shape
    c = b_re.shape[-1]
    lr = lam_re.astype(F32)
    li = lam_im.astype(F32)
    dt = jnp.exp(log_dt.astype(F32))[:, None]
    ar, ai = lr * dt, li * dt
    ks = jnp.arange(tc + 1, dtype=F32)[None, :, None]
    mag = jnp.exp(ks * ar[:, None, :])
    pr = mag * jnp.cos(ks * ai[:, None, :])
    pi = mag * jnp.sin(ks * ai[:, None, :])
    nr = jnp.expm1(ar) * jnp.cos(ai) - 2.0 * jnp.sin(0.5 * ai) ** 2
    ni = jnp.exp(ar) * jnp.sin(ai)
    den = lr * lr + li * li
    fr = (nr * lr + ni * li) / den
    fi = (ni * lr - nr * li) / den
    bbr = fr[..., None] * b_re - fi[..., None] * b_im
    bbi = fr[..., None] * b_im + fi[..., None] * b_re
    er = pr[..., None] * bbr[:, None] - pi[..., None] * bbi[:, None]
    ei = pr[..., None] * bbi[:, None] + pi[..., None] * bbr[:, None]
    kk = (jnp.einsum('gop,gkpi->gkoi', c_re, er[:, :tc], precision=HIGHEST)
          - jnp.einsum('gop,gkpi->gkoi', c_im, ei[:, :tc], precision=HIGHEST))
    gt = LANES // c
    nt = g // gt

    def tile_rows(x):
        a, sub = x.shape[1], x.shape[-1]
        return x.reshape(nt, gt, a, c, sub).transpose(0, 2, 1, 3, 4).reshape(nt, a * gt * c, sub)

    kt = kk.transpose(0, 1, 3, 2)[:, ::-1]
    zblk = jnp.zeros_like(kt[:, :1])
    g2 = _lane_diag(tile_rows(jnp.concatenate([kt, zblk], axis=1)),
                    tile_rows(jnp.concatenate([zblk, kt], axis=1)), gt=gt, c=c)
    ws = _lane_diag(tile_rows(er[:, tc - 1::-1].transpose(0, 1, 3, 2)),
                    tile_rows(ei[:, tc - 1::-1].transpose(0, 1, 3, 2)), gt=gt, c=c)
    pr1, pi1 = pr[:, 1:, None, :], pi[:, 1:, None, :]
    qr = c_re[:, None] * pr1 - c_im[:, None] * pi1
    qi = c_re[:, None] * pi1 + c_im[:, None] * pr1
    wrt = _lane_diag(tile_rows(qr), tile_rows(-qi), gt=gt, c=c)
    return (g2, ws, wrt, pr[:, tc].reshape(nt, 1, gt * p), pi[:, tc].reshape(nt, 1, gt * p))


def _s5_scan(u, tables, *, seq):
    t, w = u.shape
    g2, ws, wrt, lr, li = tables
    nt, sp = lr.shape[0], lr.shape[-1]
    rows = seq // S5_CHUNK
    tile = lambda k, b: (k, 0, 0)
    return pl.pallas_call(
        functools.partial(_s5_kernel, tc=S5_CHUNK),
        out_shape=jax.ShapeDtypeStruct((t, w), F32),
        grid=(nt, t // seq),
        in_specs=[pl.BlockSpec((seq, LANES), lambda k, b: (b, k)),
                  pl.BlockSpec((None,) + g2.shape[1:], tile),
                  pl.BlockSpec((None,) + ws.shape[1:], tile),
                  pl.BlockSpec((None,) + wrt.shape[1:], tile),
                  pl.BlockSpec((None, 1, sp), tile),
                  pl.BlockSpec((None, 1, sp), tile)],
        out_specs=pl.BlockSpec((seq, LANES), lambda k, b: (b, k)),
        scratch_shapes=[pltpu.VMEM((rows, sp), F32)] * 4,
        compiler_params=_cparams(("parallel", "parallel")),
        name="s5_scan",
    )(u, g2, ws, wrt, lr, li)


def _s5_post_kernel(y_ref, u_ref, d_ref, w_ref, b_ref, o_ref):
    y = y_ref[...] + d_ref[...] * u_ref[...].astype(F32)
    c0 = math.sqrt(2.0 / math.pi)
    y = 0.5 * y * (1.0 + jnp.tanh(c0 * (y + 0.044715 * (y * y * y))))
    gate = jnp.dot(y.astype(BF16), w_ref[...], preferred_element_type=F32) + b_ref[...]
    o_ref[...] = (y * _sigmoid(gate)).astype(o_ref.dtype)


def _s5_post(y, proj, d_skip, w_glu, b_glu, *, tm):
    t, w = y.shape
    return pl.pallas_call(
        _s5_post_kernel,
        out_shape=jax.ShapeDtypeStruct((t, w), BF16),
        grid=(t // tm,),
        in_specs=[pl.BlockSpec((tm, w), lambda i: (i, 0)),
                  pl.BlockSpec((tm, w), lambda i: (i, 0)),
                  pl.BlockSpec((1, w), lambda i: (0, 0)),
                  pl.BlockSpec((w, w), lambda i: (0, 0)),
                  pl.BlockSpec((1, w), lambda i: (0, 0))],
        out_specs=pl.BlockSpec((tm, w), lambda i: (i, 0)),
        compiler_params=_cparams(("parallel",)),
        name="s5_post",
    )(y, proj, d_skip.reshape(1, w), w_glu, b_glu.reshape(1, w))


def _lru_kernel(x_ref, cw_ref, cb_ref, wr_ref, br_ref, wi_ref, bi_ref, lam_ref, o_ref,
                xbuf_ref, a_ref, g_ref, h_ref, *, tl, kw):
    pad = 8
    w = x_ref.shape[1]

    @pl.when(pl.program_id(1) == 0)
    def _():
        xbuf_ref[pl.ds(0, pad), :] = jnp.zeros((pad, w), F32)
        h_ref[...] = jnp.zeros_like(h_ref)

    @pl.when(pl.program_id(1) > 0)
    def _():
        xbuf_ref[pl.ds(0, pad), :] = xbuf_ref[pl.ds(tl, pad), :]

    xbuf_ref[pl.ds(pad, tl), :] = x_ref[...].astype(F32)
    cw = cw_ref[...]
    xc = cb_ref[...] + xbuf_ref[pl.ds(pad - (kw - 1), tl), :] * cw[0:1, :]
    for k in range(1, kw):
        xc = xc + xbuf_ref[pl.ds(pad - (kw - 1) + k, tl), :] * cw[k:k + 1, :]
    xcb = xc.astype(BF16)
    nsup = w // LRU_SUPER
    rs, is_ = [], []
    for s in range(nsup):
        xs = xcb[:, s * LRU_SUPER:(s + 1) * LRU_SUPER]
        rs.append(jnp.dot(xs, wr_ref[s], preferred_element_type=F32))
        is_.append(jnp.dot(xs, wi_ref[s], preferred_element_type=F32))
    r = _sigmoid(jnp.concatenate(rs, axis=1) + br_ref[...])
    i = _sigmoid(jnp.concatenate(is_, axis=1) + bi_ref[...])
    log_a = (-LRU_C) * r * _softplus(-lam_ref[...])
    a_ref[...] = jnp.exp(log_a)
    th = jnp.tanh(log_a)
    g_ref[...] = jnp.sqrt((-2.0 * th) / (1.0 - th)) * (i * xc)

    row8 = lax.broadcasted_iota(jnp.int32, (8, w), 0)

    def body(t8, h):
        base = pl.multiple_of(t8 * 8, 8)
        a = a_ref[pl.ds(base, 8), :]
        b = g_ref[pl.ds(base, 8), :]
        for sh in (1, 2, 4):
            ok = row8 >= sh
            a_prev = jnp.where(ok, pltpu.roll(a, sh, axis=0), 1.0)
            b_prev = jnp.where(ok, pltpu.roll(b, sh, axis=0), 0.0)
            b = a * b_prev + b
            a = a * a_prev
        hs = a * h + b
        g_ref[pl.ds(base, 8), :] = hs
        return hs[7:8, :]

    h_ref[...] = lax.fori_loop(0, tl // 8, body, h_ref[...])
    o_ref[...] = g_ref[...].astype(o_ref.dtype)


def _block_diag_super(wblk):
    n, k, _ = wblk.shape
    per = LRU_SUPER // k
    wb = wblk.reshape(n // per, per, k, k)
    eye = jnp.eye(per, dtype=wblk.dtype)
    sup = wb[:, :, :, None, :] * eye[None, :, None, :, None]
    return sup.reshape(n // per, LRU_SUPER, LRU_SUPER)


def _lru(proj3, col_block, conv_w, conv_b, w_r, b_r, w_i, b_i, lam, *, tl):
    b, l, _ = proj3.shape
    w = conv_w.shape[1]
    kw = conv_w.shape[0]
    wr = _block_diag_super(w_r).astype(BF16)
    wi = _block_diag_super(w_i).astype(BF16)
    vec = lambda bi, ti: (0, 0)
    full3 = lambda bi, ti: (0, 0, 0)
    return pl.pallas_call(
        functools.partial(_lru_kernel, tl=tl, kw=kw),
        out_shape=jax.ShapeDtypeStruct((b, l, w), BF16),
        grid=(b, l // tl),
        in_specs=[pl.BlockSpec((None, tl, w), lambda bi, ti: (bi, ti, col_block)),
                  pl.BlockSpec((kw, w), vec),
                  pl.BlockSpec((1, w), vec),
                  pl.BlockSpec(wr.shape, full3),
                  pl.BlockSpec((1, w), vec),
                  pl.BlockSpec(wi.shape, full3),
                  pl.BlockSpec((1, w), vec),
                  pl.BlockSpec((1, w), vec)],
        out_specs=pl.BlockSpec((None, tl, w), lambda bi, ti: (bi, ti, 0)),
        scratch_shapes=[pltpu.VMEM((tl + 8, w), F32), pltpu.VMEM((tl, w), F32),
                        pltpu.VMEM((tl, w), F32), pltpu.VMEM((1, w), F32)],
        compiler_params=_cparams(("parallel", "arbitrary")),
        name="rglru",
    )(proj3, conv_w, conv_b.reshape(1, w), wr, b_r.reshape(1, w), wi, b_i.reshape(1, w), lam.reshape(1, w))


def _attn_kernel(q_ref, k_ref, v_ref, o_ref, *, tq, hp):
    qi = pl.program_id(2)
    d = HEAD_DIM
    row = lax.broadcasted_iota(jnp.int32, (tq, tq), 0)
    col = lax.broadcasted_iota(jnp.int32, (tq, tq), 1)
    suffix = jnp.where(row > col, 1.0, 0.0).astype(BF16)
    causal = col < row
    qs = [q_ref[:, h * d:(h + 1) * d] for h in range(hp)]

    def blocks(j_lo, nb, state, masked):
        ks = pl.multiple_of(j_lo * tq, tq)
        heads = range(hp)
        order = range(nb - 1, -1, -1)
        zs = [lax.dot_general(qs[h], k_ref[pl.ds(ks, nb * tq), h * d:(h + 1) * d], (((1,), (1,)), ((), ())),
                              preferred_element_type=F32) for h in heads]
        sps = []
        for h in heads:
            sp = _softplus(zs[h])
            sps.append(jnp.where(causal, sp, 0.0) if masked else sp)
        sss = [[jnp.dot(sps[h][:, n * tq:(n + 1) * tq].astype(BF16), suffix, preferred_element_type=F32)
                for n in range(nb)] for h in heads]
        out = []
        for h in heads:
            carry, acc = state[2 * h], state[2 * h + 1]
            ws = [None] * nb
            for n in order:
                cols = slice(n * tq, (n + 1) * tq)
                w = jnp.exp(zs[h][:, cols] - sps[h][:, cols] - sss[h][n] - carry)
                ws[n] = (jnp.where(causal, w, 0.0) if masked else w).astype(BF16)
                carry = carry + jnp.sum(sps[h][:, cols], axis=-1, keepdims=True)
            acc = acc + jnp.dot(jnp.concatenate(ws, axis=1), v_ref[pl.ds(ks, nb * tq), h * d:(h + 1) * d],
                                preferred_element_type=F32)
            out.extend((carry, acc))
        return tuple(out)

    state = blocks(qi, 1, (jnp.zeros((tq, 1), F32), jnp.zeros((tq, d), F32)) * hp, True)
    state = lax.cond(qi % 2 == 1, lambda c: blocks(qi - 1, 1, c, False), lambda c: c, state)
    npairs = qi // 2
    state = lax.fori_loop(0, npairs, lambda p, c: blocks(2 * (npairs - 1 - p), 2, c, False), state)
    for h in range(hp):
        o_ref[:, h * d:(h + 1) * d] = state[2 * h + 1].astype(o_ref.dtype)


def _attention(proj3, q_blk, k_blk, v_blk, n_heads, *, tq, hp):
    b, l, _ = proj3.shape
    d = HEAD_DIM
    return pl.pallas_call(
        functools.partial(_attn_kernel, tq=tq, hp=hp),
        out_shape=jax.ShapeDtypeStruct((b, l, n_heads * d), BF16),
        grid=(b, n_heads // hp, l // tq),
        in_specs=[pl.BlockSpec((None, tq, hp * d), lambda bi, hi, qi: (bi, qi, q_blk // hp + hi)),
                  pl.BlockSpec((None, l, hp * d), lambda bi, hi, qi: (bi, 0, k_blk // hp + hi)),
                  pl.BlockSpec((None, l, hp * d), lambda bi, hi, qi: (bi, 0, v_blk // hp + hi))],
        out_specs=pl.BlockSpec((None, tq, hp * d), lambda bi, hi, qi: (bi, qi, hi)),
        compiler_params=_cparams(("parallel", "parallel", "arbitrary")),
        name="stick_attn",
    )(proj3, proj3, proj3)


def _merge_kernel(ya_ref, yb_ref, yc_ref, wa_ref, wb_ref, wc_ref, ga_ref, gb_ref, gc_ref, bg_ref, o_ref):
    bg = bg_ref[...]
    out = None
    for n, (y_ref, w_ref, gl_ref) in enumerate(((ya_ref, wa_ref, ga_ref), (yb_ref, wb_ref, gb_ref),
                                                (yc_ref, wc_ref, gc_ref))):
        gate = _sigmoid(gl_ref[...].astype(F32) + bg[n:n + 1, :])
        term = gate * jnp.dot(y_ref[...], w_ref[...], preferred_element_type=F32)
        out = term if out is None else out + term
    o_ref[...] = out.astype(o_ref.dtype)


def _merge(ys, ws, proj, gate_col, b_gate, *, tm, tn):
    t, w = ys[0].shape
    d = ws[0].shape[1]
    gblk = gate_col // tn
    per = d // tn
    y_spec = pl.BlockSpec((tm, w), lambda i, j: (i, 0))
    w_spec = pl.BlockSpec((w, tn), lambda i, j: (0, j))
    gate_specs = [pl.BlockSpec((tm, tn), functools.partial(lambda i, j, n: (i, gblk + n * per + j), n=n))
                  for n in range(N_BRANCH)]
    return pl.pallas_call(
        _merge_kernel,
        out_shape=jax.ShapeDtypeStruct((t, d), BF16),
        grid=(t // tm, d // tn),
        in_specs=[y_spec] * 3 + [w_spec] * 3 + gate_specs
                 + [pl.BlockSpec((N_BRANCH, tn), lambda i, j: (0, j))],
        out_specs=pl.BlockSpec((tm, tn), lambda i, j: (i, j)),
        compiler_params=_cparams(("parallel", "arbitrary")),
        name="merge",
    )(*ys, *ws, proj, proj, proj, b_gate.reshape(N_BRANCH, d))


def _outproj_kernel(m_ref, w_ref, h_ref, g_ref, ho_ref, hn_ref):
    h = h_ref[...] + jnp.dot(m_ref[...], w_ref[...], preferred_element_type=F32)
    ho_ref[...] = h
    hn_ref[...] = _rms_rows(h, g_ref[...]).astype(hn_ref.dtype)


def _outproj(merged, w_out, h, g, *, tm):
    t, d = h.shape
    row = lambda i: (i, 0)
    return pl.pallas_call(
        _outproj_kernel,
        out_shape=(jax.ShapeDtypeStruct((t, d), F32), jax.ShapeDtypeStruct((t, d), BF16)),
        grid=(t // tm,),
        in_specs=[pl.BlockSpec((tm, d), row),
                  pl.BlockSpec((d, d), lambda i: (0, 0)),
                  pl.BlockSpec((tm, d), row),
                  pl.BlockSpec((1, d), lambda i: (0, 0))],
        out_specs=(pl.BlockSpec((tm, d), row), pl.BlockSpec((tm, d), row)),
        compiler_params=_cparams(("parallel",)),
        name="outproj",
    )(merged, w_out, h, g.reshape(1, d))


def _mlp_kernel(x_ref, wu_ref, wd_ref, h_ref, g_ref, o_ref, *, final_norm, parts):
    f = pl.program_id(1)

    @pl.when(f == 0)
    def _():
        o_ref[...] = h_ref[...]

    x = x_ref[...]
    part = wu_ref.shape[1] // parts
    hids = [jnp.maximum(jnp.dot(x, wu_ref[:, n * part:(n + 1) * part], preferred_element_type=F32), 0.0)
            for n in range(parts)]
    hid = jnp.concatenate([(hd * hd).astype(BF16) for hd in hids], axis=1)
    o_ref[...] += jnp.dot(hid, wd_ref[...], preferred_element_type=F32)

    if final_norm:
        @pl.when(f == pl.num_programs(1) - 1)
        def _():
            o_ref[...] = _rms_rows(o_ref[...], g_ref[...])


def _mlp(hn, w_up, w_down, h, g, *, tm, tf, final_norm):
    t, d = h.shape
    ff = w_up.shape[1]
    row = lambda i, f: (i, 0)
    return pl.pallas_call(
        functools.partial(_mlp_kernel, final_norm=final_norm, parts=2),
        out_shape=jax.ShapeDtypeStruct((t, d), F32),
        grid=(t // tm, ff // tf),
        in_specs=[pl.BlockSpec((tm, d), row),
                  pl.BlockSpec((d, tf), lambda i, f: (0, f)),
                  pl.BlockSpec((tf, d), lambda i, f: (f, 0)),
                  pl.BlockSpec((tm, d), row),
                  pl.BlockSpec((1, d), lambda i, f: (0, 0))],
        out_specs=pl.BlockSpec((tm, d), row),
        compiler_params=_cparams(("parallel", "arbitrary")),
        name="mlp",
    )(hn, w_up, w_down, h, g.reshape(1, d))


def _layer(h, bsz, seq, p, *, final_g):
    t, d = h.shape
    s5_w = p["s5_d"].shape[0]
    lru_w = p["lru_lambda"].shape[0]
    attn_w = p["w_br_attn"].shape[0]
    n_heads = attn_w // HEAD_DIM
    off_lru = s5_w
    off_q = off_lru + lru_w
    off_k = off_q + attn_w
    off_v = off_k + attn_w
    off_gate = off_v + attn_w

    col = jnp.arange(p["w_in"].shape[1])
    q_scale = jnp.where((col >= off_q) & (col < off_k), HEAD_DIM ** -0.5, 1.0).astype(F32)
    proj, u_s5 = _inproj(h, p["norm_mix_g"], (p["w_in"] * q_scale[None, :]).astype(BF16), tm=1024, tn=s5_w)
    proj3 = proj.reshape(bsz, seq, proj.shape[1])

    tables = _s5_tables(p["s5_lam_re"], p["s5_lam_im"], p["s5_log_dt"], p["s5_b_re"], p["s5_b_im"],
                        p["s5_c_re"], p["s5_c_im"])
    y_ssm = _s5_scan(u_s5, tables, seq=seq)
    y_s5 = _s5_post(y_ssm, u_s5, p["s5_d"], p["s5_w_glu"].astype(BF16), p["s5_b_glu"], tm=1024)

    y_lru = _lru(proj3, off_lru // lru_w, p["lru_conv_w"], p["lru_conv_b"], p["lru_w_r"], p["lru_b_r"],
                 p["lru_w_i"], p["lru_b_i"], p["lru_lambda"], tl=512).reshape(t, lru_w)

    y_attn = _attention(proj3, off_q // HEAD_DIM, off_k // HEAD_DIM, off_v // HEAD_DIM, n_heads,
                        tq=256, hp=8).reshape(t, attn_w)

    merged = _merge((y_s5, y_lru, y_attn),
                    (p["w_br_s5"].astype(BF16), p["w_br_lru"].astype(BF16), p["w_br_attn"].astype(BF16)),
                    proj, off_gate, p["b_gate"], tm=1024, tn=512)
    h, hn = _outproj(merged, p["w_out"].astype(BF16), h, p["norm_mlp_g"], tm=512)
    g_last = p["norm_mlp_g"] if final_g is None else final_g
    return _mlp(hn, p["w_up"].astype(BF16), p["w_down"].astype(BF16), h, g_last,
                tm=512, tf=1024, final_norm=final_g is not None)


_LAYER_PARAMS = ("norm_mix_g", "w_in", "b_gate", "s5_lam_re", "s5_lam_im", "s5_log_dt", "s5_b_re", "s5_b_im",
                 "s5_c_re", "s5_c_im", "s5_d", "s5_w_glu", "s5_b_glu", "lru_conv_w", "lru_conv_b", "lru_w_r",
                 "lru_b_r", "lru_w_i", "lru_b_i", "lru_lambda", "w_br_s5", "w_br_lru", "w_br_attn", "w_out",
                 "norm_mlp_g", "w_up", "w_down")


def kernel(x, norm_mix_g, w_in, b_gate, s5_lam_re, s5_lam_im, s5_log_dt, s5_b_re, s5_b_im, s5_c_re, s5_c_im,
           s5_d, s5_w_glu, s5_b_glu, lru_conv_w, lru_conv_b, lru_w_r, lru_b_r, lru_w_i, lru_b_i, lru_lambda,
           w_br_s5, w_br_lru, w_br_attn, w_out, norm_mlp_g, w_up, w_down, final_norm_g):
    stacked = dict(zip(_LAYER_PARAMS, (norm_mix_g, w_in, b_gate, s5_lam_re, s5_lam_im, s5_log_dt, s5_b_re,
                                       s5_b_im, s5_c_re, s5_c_im, s5_d, s5_w_glu, s5_b_glu, lru_conv_w,
                                       lru_conv_b, lru_w_r, lru_b_r, lru_w_i, lru_b_i, lru_lambda, w_br_s5,
                                       w_br_lru, w_br_attn, w_out, norm_mlp_g, w_up, w_down)))
    bsz, seq, d = x.shape
    depth = w_in.shape[0]
    h = x.reshape(bsz * seq, d).astype(F32)
    for layer in range(depth):
        p = {k: v[layer] for k, v in stacked.items()}
        h = _layer(h, bsz, seq, p, final_g=final_norm_g if layer == depth - 1 else None)
    return h.reshape(bsz, seq, d).astype(x.dtype)
```

```python
import functools
import math

import jax
import jax.numpy as jnp
from jax import lax
from jax.experimental import pallas as pl
from jax.experimental.pallas import tpu as pltpu

F32 = jnp.float32
BF16 = jnp.bfloat16

EPS = 1e-6
LANES = 128
HEAD_DIM = 128
ATTN_UNDERFLOW = 128.0
S5_GROUP = 16
S5_CHUNK = 16
LRU_BLOCK = 64
LRU_SUPER = 256
LRU_C = 8.0
N_BRANCH = 3
LOG2E = 1.4426950408889634
VMEM_LIMIT = 56 * 1024 * 1024
HIGHEST = lax.Precision.HIGHEST


def _cparams(sem):
    return pltpu.CompilerParams(dimension_semantics=sem, vmem_limit_bytes=VMEM_LIMIT)


def _softplus(z):
    return jnp.maximum(z, 0.0) + jnp.log(1.0 + jnp.exp2(jnp.abs(z) * (-LOG2E)))


def _sigmoid(z):
    return 1.0 / (1.0 + jnp.exp(-z))


def _rms_rows(x, g):
    ms = jnp.mean(x * x, axis=-1, keepdims=True)
    return (x * lax.rsqrt(ms + EPS)) * g


def _inproj_kernel(x_ref, g_ref, w_ref, o_ref, o32_ref, xn_ref, *, row_chunk):
    j = pl.program_id(1)

    @pl.when(j == 0)
    def _():
        g = g_ref[...]
        w = w_ref[...]
        for r in range(0, x_ref.shape[0], row_chunk):
            xn = _rms_rows(x_ref[r:r + row_chunk, :], g).astype(BF16)
            xn_ref[r:r + row_chunk, :] = xn
            acc = jnp.dot(xn, w, preferred_element_type=F32)
            o_ref[r:r + row_chunk, :] = acc.astype(o_ref.dtype)
            o32_ref[r:r + row_chunk, :] = acc

    @pl.when(j > 0)
    def _():
        o_ref[...] = jnp.dot(xn_ref[...], w_ref[...], preferred_element_type=F32).astype(o_ref.dtype)


def _inproj(h, g, w, *, tm, tn):
    t, d = h.shape
    n = w.shape[1]
    return pl.pallas_call(
        functools.partial(_inproj_kernel, row_chunk=256),
        out_shape=(jax.ShapeDtypeStruct((t, n), BF16), jax.ShapeDtypeStruct((t, tn), F32)),
        grid=(t // tm, n // tn),
        in_specs=[pl.BlockSpec((tm, d), lambda i, j: (i, 0)),
                  pl.BlockSpec((1, d), lambda i, j: (0, 0)),
                  pl.BlockSpec((d, tn), lambda i, j: (0, j))],
        out_specs=(pl.BlockSpec((tm, tn), lambda i, j: (i, j)),
                   pl.BlockSpec((tm, tn), lambda i, j: (i, 0))),
        scratch_shapes=[pltpu.VMEM((tm, d), BF16)],
        compiler_params=_cparams(("parallel", "arbitrary")),
        name="inproj",
    )(h, g.reshape(1, d), w)


def _s5_kernel(u_ref, g2_ref, ws_ref, wrt_ref, lr_ref, li_ref, o_ref,
               xre_ref, xim_ref, hre_ref, him_ref, *, tc):
    rows = u_ref.shape[0] // tc
    lanes = u_ref.shape[1]
    xc = jnp.concatenate([u_ref[pl.ds(t, rows, stride=tc), :].astype(BF16) for t in range(tc)], axis=1)
    lr = lr_ref[...]
    li = li_ref[...]
    sp = lr.shape[1]
    x = jnp.dot(xc, ws_ref[...], preferred_element_type=F32)
    xre_ref[...] = x[:, :sp]
    xim_ref[...] = x[:, sp:]

    def body(k, carry):
        hr, hi = carry
        base = pl.multiple_of(k * 8, 8)
        for r in range(8):
            hre_ref[pl.ds(base + r, 1), :] = hr
            him_ref[pl.ds(base + r, 1), :] = hi
            xr = xre_ref[pl.ds(base + r, 1), :]
            xi = xim_ref[pl.ds(base + r, 1), :]
            hr, hi = lr * hr - li * hi + xr, lr * hi + li * hr + xi
        return hr, hi

    zero = jnp.zeros(lr.shape, F32)
    lax.fori_loop(0, rows // 8, body, (zero, zero))

    h0 = jnp.concatenate([hre_ref[...].astype(BF16), him_ref[...].astype(BF16)], axis=1)
    yr = lax.dot_general(h0, wrt_ref[...], (((1,), (1,)), ((), ())), preferred_element_type=F32)
    nblk = g2_ref.shape[0] // lanes
    for m in range(tc // 2):
        kdim = (2 * m + 2) * lanes
        y = jnp.dot(xc[:, :kdim], g2_ref[(nblk * lanes - kdim):, :], preferred_element_type=F32)
        y = y + yr[:, 2 * m * lanes:(2 * m + 2) * lanes]
        o_ref[pl.ds(2 * m, rows, stride=tc), :] = y[:, :lanes]
        o_ref[pl.ds(2 * m + 1, rows, stride=tc), :] = y[:, lanes:]


def _lane_diag_kernel(a_ref, b_ref, o_ref, *, gt, c):
    rows, sub = a_ref.shape
    width = gt * sub
    src = lax.broadcasted_iota(jnp.int32, (sub, width), 0)
    dst = lax.broadcasted_iota(jnp.int32, (sub, width), 1)
    spread = jnp.where(dst % sub == src, 1.0, 0.0).astype(BF16)
    row_g = (lax.broadcasted_iota(jnp.int32, (rows, width), 0) // c) % gt
    col_g = lax.broadcasted_iota(jnp.int32, (rows, width), 1) // sub
    keep = row_g == col_g
    for n, x_ref in enumerate((a_ref, b_ref)):
        tiled = jnp.dot(x_ref[...].astype(BF16), spread, preferred_element_type=F32)
        o_ref[:, n * width:(n + 1) * width] = jnp.where(keep, tiled, 0.0).astype(o_ref.dtype)


def _lane_diag(a, b, *, gt, c):
    nt, rows, sub = a.shape
    spec = pl.BlockSpec((None, rows, sub), lambda k: (k, 0, 0))
    return pl.pallas_call(
        functools.partial(_lane_diag_kernel, gt=gt, c=c),
        out_shape=jax.ShapeDtypeStruct((nt, rows, 2 * gt * sub), BF16),
        grid=(nt,),
        in_specs=[spec, spec],
        out_specs=pl.BlockSpec((None, rows, 2 * gt * sub), lambda k: (k, 0, 0)),
        compiler_params=_cparams(("parallel",)),
        name="s5_lane_diag",
    )(a, b)


def _s5_tables(lam_re, lam_im, log_dt, b_re, b_im, c_re, c_im):
    tc = S5_CHUNK
    g, p = lam_re.shape
    c = b_re.shape[-1]
    lr = lam_re.astype(F32)
    li = lam_im.astype(F32)
    dt = jnp.exp(log_dt.astype(F32))[:, None]
    ar, ai = lr * dt, li * dt
    ks = jnp.arange(tc + 1, dtype=F32)[None, :, None]
    mag = jnp.exp(ks * ar[:, None, :])
    pr = mag * jnp.cos(ks * ai[:, None, :])
    pi = mag * jnp.sin(ks * ai[:, None, :])
    nr = jnp.expm1(ar) * jnp.cos(ai) - 2.0 * jnp.sin(0.5 * ai) ** 2
    ni = jnp.exp(ar) * jnp.sin(ai)
    den = lr * lr + li * li
    fr = (nr * lr + ni * li) / den
    fi = (ni * lr - nr * li) / den
    bbr = fr[..., None] * b_re - fi[..., None] * b_im
    bbi = fr[..., None] * b_im + fi[..., None] * b_re
    er = pr[..., None] * bbr[:, None] - pi[..., None] * bbi[:, None]
    ei = pr[..., None] * bbi[:, None] + pi[..., None] * bbr[:, None]
    kk = (jnp.einsum('gop,gkpi->gkoi', c_re, er[:, :tc], precision=HIGHEST)
          - jnp.einsum('gop,gkpi->gkoi', c_im, ei[:, :tc], precision=HIGHEST))
    gt = LANES // c
    nt = g // gt

    def tile_rows(x):
        a, sub = x.shape[1], x.shape[-1]
        return x.reshape(nt, gt, a, c, sub).transpose(0, 2, 1, 3, 4).reshape(nt, a * gt * c, sub)

    kt = kk.transpose(0, 1, 3, 2)[:, ::-1]
    zblk = jnp.zeros_like(kt[:, :1])
    g2 = _lane_diag(tile_rows(jnp.concatenate([kt, zblk], axis=1)),
                    tile_rows(jnp.concatenate([zblk, kt], axis=1)), gt=gt, c=c)
    ws = _lane_diag(tile_rows(er[:, tc - 1::-1].transpose(0, 1, 3, 2)),
                    tile_rows(ei[:, tc - 1::-1].transpose(0, 1, 3, 2)), gt=gt, c=c)
    pr1, pi1 = pr[:, 1:, None, :], pi[:, 1:, None, :]
    qr = c_re[:, None] * pr1 - c_im[:, None] * pi1
    qi = c_re[:, None] * pi1 + c_im[:, None] * pr1
    wrt = _lane_diag(tile_rows(qr), tile_rows(-qi), gt=gt, c=c)
    return (g2, ws, wrt, pr[:, tc].reshape(nt, 1, gt * p), pi[:, tc].reshape(nt, 1, gt * p))


def _s5_scan(u, tables, *, seq):
    t, w = u.shape
    g2, ws, wrt, lr, li = tables
    nt, sp = lr.shape[0], lr.shape[-1]
    rows = seq // S5_CHUNK
    tile = lambda k, b: (k, 0, 0)
    return pl.pallas_call(
        functools.partial(_s5_kernel, tc=S5_CHUNK),
        out_shape=jax.ShapeDtypeStruct((t, w), F32),
        grid=(nt, t // seq),
        in_specs=[pl.BlockSpec((seq, LANES), lambda k, b: (b, k)),
                  pl.BlockSpec((None,) + g2.shape[1:], tile),
                  pl.BlockSpec((None,) + ws.shape[1:], tile),
                  pl.BlockSpec((None,) + wrt.shape[1:], tile),
                  pl.BlockSpec((None, 1, sp), tile),
                  pl.BlockSpec((None, 1, sp), tile)],
        out_specs=pl.BlockSpec((seq, LANES), lambda k, b: (b, k)),
        scratch_shapes=[pltpu.VMEM((rows, sp), F32)] * 4,
        compiler_params=_cparams(("parallel", "parallel")),
        name="s5_scan",
    )(u, g2, ws, wrt, lr, li)


def _s5_post_kernel(y_ref, u_ref, d_ref, w_ref, b_ref, o_ref):
    y = y_ref[...] + d_ref[...] * u_ref[...].astype(F32)
    c0 = math.sqrt(2.0 / math.pi)
    y = 0.5 * y * (1.0 + jnp.tanh(c0 * (y + 0.044715 * (y * y * y))))
    gate = jnp.dot(y.astype(BF16), w_ref[...], preferred_element_type=F32) + b_ref[...]
    o_ref[...] = (y * _sigmoid(gate)).astype(o_ref.dtype)


def _s5_post(y, proj, d_skip, w_glu, b_glu, *, tm):
    t, w = y.shape
    return pl.pallas_call(
        _s5_post_kernel,
        out_shape=jax.ShapeDtypeStruct((t, w), BF16),
        grid=(t // tm,),
        in_specs=[pl.BlockSpec((tm, w), lambda i: (i, 0)),
                  pl.BlockSpec((tm, w), lambda i: (i, 0)),
                  pl.BlockSpec((1, w), lambda i: (0, 0)),
                  pl.BlockSpec((w, w), lambda i: (0, 0)),
                  pl.BlockSpec((1, w), lambda i: (0, 0))],
        out_specs=pl.BlockSpec((tm, w), lambda i: (i, 0)),
        compiler_params=_cparams(("parallel",)),
        name="s5_post",
    )(y, proj, d_skip.reshape(1, w), w_glu, b_glu.reshape(1, w))


def _lru_kernel(x_ref, cw_ref, cb_ref, wr_ref, br_ref, wi_ref, bi_ref, lam_ref, o_ref,
                xbuf_ref, a_ref, g_ref, h_ref, *, tl, kw):
    pad = 8
    w = x_ref.shape[1]

    @pl.when(pl.program_id(1) == 0)
    def _():
        xbuf_ref[pl.ds(0, pad), :] = jnp.zeros((pad, w), F32)
        h_ref[...] = jnp.zeros_like(h_ref)

    @pl.when(pl.program_id(1) > 0)
    def _():
        xbuf_ref[pl.ds(0, pad), :] = xbuf_ref[pl.ds(tl, pad), :]

    xbuf_ref[pl.ds(pad, tl), :] = x_ref[...].astype(F32)
    cw = cw_ref[...]
    xc = cb_ref[...] + xbuf_ref[pl.ds(pad - (kw - 1), tl), :] * cw[0:1, :]
    for k in range(1, kw):
        xc = xc + xbuf_ref[pl.ds(pad - (kw - 1) + k, tl), :] * cw[k:k + 1, :]
    xcb = xc.astype(BF16)
    nsup = w // LRU_SUPER
    rs, is_ = [], []
    for s in range(nsup):
        xs = xcb[:, s * LRU_SUPER:(s + 1) * LRU_SUPER]
        rs.append(jnp.dot(xs, wr_ref[s], preferred_element_type=F32))
        is_.append(jnp.dot(xs, wi_ref[s], preferred_element_type=F32))
    r = _sigmoid(jnp.concatenate(rs, axis=1) + br_ref[...])
    i = _sigmoid(jnp.concatenate(is_, axis=1) + bi_ref[...])
    log_a = (-LRU_C) * r * _softplus(-lam_ref[...])
    a_ref[...] = jnp.exp(log_a)
    th = jnp.tanh(log_a)
    g_ref[...] = jnp.sqrt((-2.0 * th) / (1.0 - th)) * (i * xc)

    row8 = lax.broadcasted_iota(jnp.int32, (8, w), 0)

    def body(t8, h):
        base = pl.multiple_of(t8 * 8, 8)
        a = a_ref[pl.ds(base, 8), :]
        b = g_ref[pl.ds(base, 8), :]
        for sh in (1, 2, 4):
            ok = row8 >= sh
            a_prev = jnp.where(ok, pltpu.roll(a, sh, axis=0), 1.0)
            b_prev = jnp.where(ok, pltpu.roll(b, sh, axis=0), 0.0)
            b = a * b_prev + b
            a = a * a_prev
        hs = a * h + b
        g_ref[pl.ds(base, 8), :] = hs
        return hs[7:8, :]

    h_ref[...] = lax.fori_loop(0, tl // 8, body, h_ref[...])
    o_ref[...] = g_ref[...].astype(o_ref.dtype)


def _block_diag_super(wblk):
    n, k, _ = wblk.shape
    per = LRU_SUPER // k
    wb = wblk.reshape(n // per, per, k, k)
    eye = jnp.eye(per, dtype=wblk.dtype)
    sup = wb[:, :, :, None, :] * eye[None, :, None, :, None]
    return sup.reshape(n // per, LRU_SUPER, LRU_SUPER)


def _lru(proj3, col_block, conv_w, conv_b, w_r, b_r, w_i, b_i, lam, *, tl):
    b, l, _ = proj3.shape
    w = conv_w.shape[1]
    kw = conv_w.shape[0]
    wr = _block_diag_super(w_r).astype(BF16)
    wi = _block_diag_super(w_i).astype(BF16)
    vec = lambda bi, ti: (0, 0)
    full3 = lambda bi, ti: (0, 0, 0)
    return pl.pallas_call(
        functools.partial(_lru_kernel, tl=tl, kw=kw),
        out_shape=jax.ShapeDtypeStruct((b, l, w), BF16),
        grid=(b, l // tl),
        in_specs=[pl.BlockSpec((None, tl, w), lambda bi, ti: (bi, ti, col_block)),
                  pl.BlockSpec((kw, w), vec),
                  pl.BlockSpec((1, w), vec),
                  pl.BlockSpec(wr.shape, full3),
                  pl.BlockSpec((1, w), vec),
                  pl.BlockSpec(wi.shape, full3),
                  pl.BlockSpec((1, w), vec),
                  pl.BlockSpec((1, w), vec)],
        out_specs=pl.BlockSpec((None, tl, w), lambda bi, ti: (bi, ti, 0)),
        scratch_shapes=[pltpu.VMEM((tl + 8, w), F32), pltpu.VMEM((tl, w), F32),
                        pltpu.VMEM((tl, w), F32), pltpu.VMEM((1, w), F32)],
        compiler_params=_cparams(("parallel", "arbitrary")),
        name="rglru",
    )(proj3, conv_w, conv_b.reshape(1, w), wr, b_r.reshape(1, w), wi, b_i.reshape(1, w), lam.reshape(1, w))


def _attn_kernel(q_ref, k_ref, v_ref, o_ref, *, tq, hp):
    qi = pl.program_id(2)
    d = HEAD_DIM
    row = lax.broadcasted_iota(jnp.int32, (tq, tq), 0)
    col = lax.broadcasted_iota(jnp.int32, (tq, tq), 1)
    suffix = jnp.where(row > col, 1.0, 0.0).astype(BF16)
    causal = col < row
    qs = [q_ref[:, h * d:(h + 1) * d] for h in range(hp)]

    def blocks(j_lo, nb, state, masked):
        ks = pl.multiple_of(j_lo * tq, tq)
        heads = range(hp)
        order = range(nb - 1, -1, -1)
        zs = [lax.dot_general(qs[h], k_ref[pl.ds(ks, nb * tq), h * d:(h + 1) * d], (((1,), (1,)), ((), ())),
                              preferred_element_type=F32) for h in heads]
        sps = []
        for h in heads:
            sp = _softplus(zs[h])
            sps.append(jnp.where(causal, sp, 0.0) if masked else sp)
        sss = [[jnp.dot(sps[h][:, n * tq:(n + 1) * tq].astype(BF16), suffix, preferred_element_type=F32)
                for n in range(nb)] for h in heads]
        out = []
        for h in heads:
            carry, acc = state[2 * h], state[2 * h + 1]
            ws = [None] * nb
            for n in order:
                cols = slice(n * tq, (n + 1) * tq)
                w = jnp.exp(zs[h][:, cols] - sps[h][:, cols] - sss[h][n] - carry)
                ws[n] = (jnp.where(causal, w, 0.0) if masked else w).astype(BF16)
                carry = carry + jnp.sum(sps[h][:, cols], axis=-1, keepdims=True)
            acc = acc + jnp.dot(jnp.concatenate(ws, axis=1), v_ref[pl.ds(ks, nb * tq), h * d:(h + 1) * d],
                                preferred_element_type=F32)
            out.extend((carry, acc))
        return tuple(out)

    state = blocks(qi, 1, (jnp.zeros((tq, 1), F32), jnp.zeros((tq, d), F32)) * hp, True)

    def keep_going(c):
        return jnp.logical_and(c[0] < qi, c[1] == 0)

    def sweep(c):
        jj = c[0]
        new = blocks(qi - 1 - jj, 1, c[2:], False)
        low = new[0]
        for h in range(1, hp):
            low = jnp.minimum(low, new[2 * h])
        done = (jnp.min(low) > ATTN_UNDERFLOW).astype(jnp.int32)
        return (jj + 1, done) + new

    state = lax.while_loop(keep_going, sweep, (jnp.int32(0), jnp.int32(0)) + state)[2:]
    for h in range(hp):
        o_ref[:, h * d:(h + 1) * d] = state[2 * h + 1].astype(o_ref.dtype)


def _attention(proj3, q_blk, k_blk, v_blk, n_heads, *, tq, hp):
    b, l, _ = proj3.shape
    d = HEAD_DIM
    return pl.pallas_call(
        functools.partial(_attn_kernel, tq=tq, hp=hp),
        out_shape=jax.ShapeDtypeStruct((b, l, n_heads * d), BF16),
        grid=(b, n_heads // hp, l // tq),
        in_specs=[pl.BlockSpec((None, tq, hp * d), lambda bi, hi, qi: (bi, qi, q_blk // hp + hi)),
                  pl.BlockSpec((None, l, hp * d), lambda bi, hi, qi: (bi, 0, k_blk // hp + hi)),
                  pl.BlockSpec((None, l, hp * d), lambda bi, hi, qi: (bi, 0, v_blk // hp + hi))],
        out_specs=pl.BlockSpec((None, tq, hp * d), lambda bi, hi, qi: (bi, qi, hi)),
        compiler_params=_cparams(("parallel", "parallel", "arbitrary")),
        name="stick_attn",
    )(proj3, proj3, proj3)


def _merge_kernel(ya_ref, yb_ref, yc_ref, wa_ref, wb_ref, wc_ref, ga_ref, gb_ref, gc_ref, bg_ref, o_ref):
    bg = bg_ref[...]
    out = None
    for n, (y_ref, w_ref, gl_ref) in enumerate(((ya_ref, wa_ref, ga_ref), (yb_ref, wb_ref, gb_ref),
                                                (yc_ref, wc_ref, gc_ref))):
        gate = _sigmoid(gl_ref[...].astype(F32) + bg[n:n + 1, :])
        term = gate * jnp.dot(y_ref[...], w_ref[...], preferred_element_type=F32)
        out = term if out is None else out + term
    o_ref[...] = out.astype(o_ref.dtype)


def _merge(ys, ws, proj, gate_col, b_gate, *, tm, tn):
    t, w = ys[0].shape
    d = ws[0].shape[1]
    gblk = gate_col // tn
    per = d // tn
    y_spec = pl.BlockSpec((tm, w), lambda i, j: (i, 0))
    w_spec = pl.BlockSpec((w, tn), lambda i, j: (0, j))
    gate_specs = [pl.BlockSpec((tm, tn), functools.partial(lambda i, j, n: (i, gblk + n * per + j), n=n))
                  for n in range(N_BRANCH)]
    return pl.pallas_call(
        _merge_kernel,
        out_shape=jax.ShapeDtypeStruct((t, d), BF16),
        grid=(t // tm, d // tn),
        in_specs=[y_spec] * 3 + [w_spec] * 3 + gate_specs
                 + [pl.BlockSpec((N_BRANCH, tn), lambda i, j: (0, j))],
        out_specs=pl.BlockSpec((tm, tn), lambda i, j: (i, j)),
        compiler_params=_cparams(("parallel", "arbitrary")),
        name="merge",
    )(*ys, *ws, proj, proj, proj, b_gate.reshape(N_BRANCH, d))


def _outproj_kernel(m_ref, w_ref, h_ref, g_ref, ho_ref, hn_ref):
    h = h_ref[...] + jnp.dot(m_ref[...], w_ref[...], preferred_element_type=F32)
    ho_ref[...] = h
    hn_ref[...] = _rms_rows(h, g_ref[...]).astype(hn_ref.dtype)


def _outproj(merged, w_out, h, g, *, tm):
    t, d = h.shape
    row = lambda i: (i, 0)
    return pl.pallas_call(
        _outproj_kernel,
        out_shape=(jax.ShapeDtypeStruct((t, d), F32), jax.ShapeDtypeStruct((t, d), BF16)),
        grid=(t // tm,),
        in_specs=[pl.BlockSpec((tm, d), row),
                  pl.BlockSpec((d, d), lambda i: (0, 0)),
                  pl.BlockSpec((tm, d), row),
                  pl.BlockSpec((1, d), lambda i: (0, 0))],
        out_specs=(pl.BlockSpec((tm, d), row), pl.BlockSpec((tm, d), row)),
        compiler_params=_cparams(("parallel",)),
        name="outproj",
    )(merged, w_out, h, g.reshape(1, d))


def _mlp_kernel(x_ref, wu_ref, wd_ref, h_ref, g_ref, o_ref, *, final_norm, parts):
    f = pl.program_id(1)

    @pl.when(f == 0)
    def _():
        o_ref[...] = h_ref[...]

    x = x_ref[...]
    part = wu_ref.shape[1] // parts
    hids = [jnp.maximum(jnp.dot(x, wu_ref[:, n * part:(n + 1) * part], preferred_element_type=F32), 0.0)
            for n in range(parts)]
    hid = jnp.concatenate([(hd * hd).astype(BF16) for hd in hids], axis=1)
    o_ref[...] += jnp.dot(hid, wd_ref[...], preferred_element_type=F32)

    if final_norm:
        @pl.when(f == pl.num_programs(1) - 1)
        def _():
            o_ref[...] = _rms_rows(o_ref[...], g_ref[...])


def _mlp(hn, w_up, w_down, h, g, *, tm, tf, final_norm):
    t, d = h.shape
    ff = w_up.shape[1]
    row = lambda i, f: (i, 0)
    return pl.pallas_call(
        functools.partial(_mlp_kernel, final_norm=final_norm, parts=2),
        out_shape=jax.ShapeDtypeStruct((t, d), F32),
        grid=(t // tm, ff // tf),
        in_specs=[pl.BlockSpec((tm, d), row),
                  pl.BlockSpec((d, tf), lambda i, f: (0, f)),
                  pl.BlockSpec((tf, d), lambda i, f: (f, 0)),
                  pl.BlockSpec((tm, d), row),
                  pl.BlockSpec((1, d), lambda i, f: (0, 0))],
        out_specs=pl.BlockSpec((tm, d), row),
        compiler_params=_cparams(("parallel", "arbitrary")),
        name="mlp",
    )(hn, w_up, w_down, h, g.reshape(1, d))


def _layer(h, bsz, seq, p, *, final_g):
    t, d = h.shape
    s5_w = p["s5_d"].shape[0]
    lru_w = p["lru_lambda"].shape[0]
    attn_w = p["w_br_attn"].shape[0]
    n_heads = attn_w // HEAD_DIM
    off_lru = s5_w
    off_q = off_lru + lru_w
    off_k = off_q + attn_w
    off_v = off_k + attn_w
    off_gate = off_v + attn_w

    col = jnp.arange(p["w_in"].shape[1])
    q_scale = jnp.where((col >= off_q) & (col < off_k), HEAD_DIM ** -0.5, 1.0).astype(F32)
    proj, u_s5 = _inproj(h, p["norm_mix_g"], (p["w_in"] * q_scale[None, :]).astype(BF16), tm=1024, tn=s5_w)
    proj3 = proj.reshape(bsz, seq, proj.shape[1])

    tables = _s5_tables(p["s5_lam_re"], p["s5_lam_im"], p["s5_log_dt"], p["s5_b_re"], p["s5_b_im"],
                        p["s5_c_re"], p["s5_c_im"])
    y_ssm = _s5_scan(u_s5, tables, seq=seq)
    y_s5 = _s5_post(y_ssm, u_s5, p["s5_d"], p["s5_w_glu"].astype(BF16), p["s5_b_glu"], tm=1024)

    y_lru = _lru(proj3, off_lru // lru_w, p["lru_conv_w"], p["lru_conv_b"], p["lru_w_r"], p["lru_b_r"],
                 p["lru_w_i"], p["lru_b_i"], p["lru_lambda"], tl=512).reshape(t, lru_w)

    y_attn = _attention(proj3, off_q // HEAD_DIM, off_k // HEAD_DIM, off_v // HEAD_DIM, n_heads,
                        tq=256, hp=8).reshape(t, attn_w)

    merged = _merge((y_s5, y_lru, y_attn),
                    (p["w_br_s5"].astype(BF16), p["w_br_lru"].astype(BF16), p["w_br_attn"].astype(BF16)),
                    proj, off_gate, p["b_gate"], tm=1024, tn=512)
    h, hn = _outproj(merged, p["w_out"].astype(BF16), h, p["norm_mlp_g"], tm=512)
    g_last = p["norm_mlp_g"] if final_g is None else final_g
    return _mlp(hn, p["w_up"].astype(BF16), p["w_down"].astype(BF16), h, g_last,
                tm=512, tf=1024, final_norm=final_g is not None)


_LAYER_PARAMS = ("norm_mix_g", "w_in", "b_gate", "s5_lam_re", "s5_lam_im", "s5_log_dt", "s5_b_re", "s5_b_im",
                 "s5_c_re", "s5_c_im", "s5_d", "s5_w_glu", "s5_b_glu", "lru_conv_w", "lru_conv_b", "lru_w_r",
                 "lru_b_r", "lru_w_i", "lru_b_i", "lru_lambda", "w_br_s5", "w_br_lru", "w_br_attn", "w_out",
                 "norm_mlp_g", "w_up", "w_down")


def kernel(x, norm_mix_g, w_in, b_gate, s5_lam_re, s5_lam_im, s5_log_dt, s5_b_re, s5_b_im, s5_c_re, s5_c_im,
           s5_d, s5_w_glu, s5_b_glu, lru_conv_w, lru_conv_b, lru_w_r, lru_b_r, lru_w_i, lru_b_i, lru_lambda,
           w_br_s5, w_br_lru, w_br_attn, w_out, norm_mlp_g, w_up, w_down, final_norm_g):
    stacked = dict(zip(_LAYER_PARAMS, (norm_mix_g, w_in, b_gate, s5_lam_re, s5_lam_im, s5_log_dt, s5_b_re,
                                       s5_b_im, s5_c_re, s5_c_im, s5_d, s5_w_glu, s5_b_glu, lru_conv_w,
                                       lru_conv_b, lru_w_r, lru_b_r, lru_w_i, lru_b_i, lru_lambda, w_br_s5,
                                       w_br_lru, w_br_attn, w_out, norm_mlp_g, w_up, w_down)))
    bsz, seq, d = x.shape
    depth = w_in.shape[0]
    h = x.reshape(bsz * seq, d).astype(F32)
    for layer in range(depth):
        p = {k: v[layer] for k, v in stacked.items()}
        h = _layer(h, bsz, seq, p, final_g=final_norm_g if layer == depth - 1 else None)
    return h.reshape(bsz, seq, d).astype(x.dtype)
```

```python
import functools
import math

import jax
import jax.numpy as jnp
from jax import lax
from jax.experimental import pallas as pl
from jax.experimental.pallas import tpu as pltpu

F32 = jnp.float32
BF16 = jnp.bfloat16

EPS = 1e-6
LANES = 128
HEAD_DIM = 128
ATTN_UNDERFLOW = 128.0
S5_GROUP = 16
S5_CHUNK = 16
LRU_BLOCK = 64
LRU_SUPER = 256
LRU_C = 8.0
N_BRANCH = 3
LOG2E = 1.4426950408889634
VMEM_LIMIT = 56 * 1024 * 1024
HIGHEST = lax.Precision.HIGHEST


def _cparams(sem):
    return pltpu.CompilerParams(dimension_semantics=sem, vmem_limit_bytes=VMEM_LIMIT)


def _softplus(z):
    return jnp.maximum(z, 0.0) + jnp.log(1.0 + jnp.exp2(jnp.abs(z) * (-LOG2E)))


def _sigmoid(z):
    return 1.0 / (1.0 + jnp.exp(-z))


def _rms_rows(x, g):
    ms = jnp.mean(x * x, axis=-1, keepdims=True)
    return (x * lax.rsqrt(ms + EPS)) * g


def _inproj_kernel(x_ref, g_ref, w_ref, o_ref, o32_ref, xn_ref, *, row_chunk):
    j = pl.program_id(1)

    @pl.when(j == 0)
    def _():
        g = g_ref[...]
        w = w_ref[...]
        for r in range(0, x_ref.shape[0], row_chunk):
            xn = _rms_rows(x_ref[r:r + row_chunk, :], g).astype(BF16)
            xn_ref[r:r + row_chunk, :] = xn
            acc = jnp.dot(xn, w, preferred_element_type=F32)
            o_ref[r:r + row_chunk, :] = acc.astype(o_ref.dtype)
            o32_ref[r:r + row_chunk, :] = acc

    @pl.when(j > 0)
    def _():
        o_ref[...] = jnp.dot(xn_ref[...], w_ref[...], preferred_element_type=F32).astype(o_ref.dtype)


def _inproj(h, g, w, *, tm, tn):
    t, d = h.shape
    n = w.shape[1]
    return pl.pallas_call(
        functools.partial(_inproj_kernel, row_chunk=256),
        out_shape=(jax.ShapeDtypeStruct((t, n), BF16), jax.ShapeDtypeStruct((t, tn), F32)),
        grid=(t // tm, n // tn),
        in_specs=[pl.BlockSpec((tm, d), lambda i, j: (i, 0)),
                  pl.BlockSpec((1, d), lambda i, j: (0, 0)),
                  pl.BlockSpec((d, tn), lambda i, j: (0, j))],
        out_specs=(pl.BlockSpec((tm, tn), lambda i, j: (i, j)),
                   pl.BlockSpec((tm, tn), lambda i, j: (i, 0))),
        scratch_shapes=[pltpu.VMEM((tm, d), BF16)],
        compiler_params=_cparams(("parallel", "arbitrary")),
        name="inproj",
    )(h, g.reshape(1, d), w)


def _s5_kernel(u_ref, g2_ref, ws_ref, wrt_ref, lr_ref, li_ref, o_ref,
               xre_ref, xim_ref, hre_ref, him_ref, *, tc):
    rows = u_ref.shape[0] // tc
    lanes = u_ref.shape[1]
    xc = jnp.concatenate([u_ref[pl.ds(t, rows, stride=tc), :].astype(BF16) for t in range(tc)], axis=1)
    lr = lr_ref[...]
    li = li_ref[...]
    sp = lr.shape[1]
    x = jnp.dot(xc, ws_ref[...], preferred_element_type=F32)
    xre_ref[...] = x[:, :sp]
    xim_ref[...] = x[:, sp:]

    def body(k, carry):
        hr, hi = carry
        base = pl.multiple_of(k * 8, 8)
        for r in range(8):
            hre_ref[pl.ds(base + r, 1), :] = hr
            him_ref[pl.ds(base + r, 1), :] = hi
            xr = xre_ref[pl.ds(base + r, 1), :]
            xi = xim_ref[pl.ds(base + r, 1), :]
            hr, hi = lr * hr - li * hi + xr, lr * hi + li * hr + xi
        return hr, hi

    zero = jnp.zeros(lr.shape, F32)
    lax.fori_loop(0, rows // 8, body, (zero, zero))

    h0 = jnp.concatenate([hre_ref[...].astype(BF16), him_ref[...].astype(BF16)], axis=1)
    yr = lax.dot_general(h0, wrt_ref[...], (((1,), (1,)), ((), ())), preferred_element_type=F32)
    nblk = g2_ref.shape[0] // lanes
    for m in range(tc // 2):
        kdim = (2 * m + 2) * lanes
        y = jnp.dot(xc[:, :kdim], g2_ref[(nblk * lanes - kdim):, :], preferred_element_type=F32)
        y = y + yr[:, 2 * m * lanes:(2 * m + 2) * lanes]
        o_ref[pl.ds(2 * m, rows, stride=tc), :] = y[:, :lanes]
        o_ref[pl.ds(2 * m + 1, rows, stride=tc), :] = y[:, lanes:]


def _lane_diag_kernel(a_ref, b_ref, o_ref, *, gt, c):
    rows, sub = a_ref.shape
    width = gt * sub
    src = lax.broadcasted_iota(jnp.int32, (sub, width), 0)
    dst = lax.broadcasted_iota(jnp.int32, (sub, width), 1)
    spread = jnp.where(dst % sub == src, 1.0, 0.0).astype(BF16)
    row_g = (lax.broadcasted_iota(jnp.int32, (rows, width), 0) // c) % gt
    col_g = lax.broadcasted_iota(jnp.int32, (rows, width), 1) // sub
    keep = row_g == col_g
    for n, x_ref in enumerate((a_ref, b_ref)):
        tiled = jnp.dot(x_ref[...].astype(BF16), spread, preferred_element_type=F32)
        o_ref[:, n * width:(n + 1) * width] = jnp.where(keep, tiled, 0.0).astype(o_ref.dtype)


def _lane_diag(a, b, *, gt, c):
    nt, rows, sub = a.shape
    spec = pl.BlockSpec((None, rows, sub), lambda k: (k, 0, 0))
    return pl.pallas_call(
        functools.partial(_lane_diag_kernel, gt=gt, c=c),
        out_shape=jax.ShapeDtypeStruct((nt, rows, 2 * gt * sub), BF16),
        grid=(nt,),
        in_specs=[spec, spec],
        out_specs=pl.BlockSpec((None, rows, 2 * gt * sub), lambda k: (k, 0, 0)),
        compiler_params=_cparams(("parallel",)),
        name="s5_lane_diag",
    )(a, b)


def _s5_factors(lam_re, lam_im, log_dt, b_re, b_im, c_re, c_im):
    tc = S5_CHUNK
    g, p = lam_re.shape
    c = b_re.shape[-1]
    lr = lam_re.astype(F32)
    li = lam_im.astype(F32)
    dt = jnp.exp(log_dt.astype(F32))[:, None]
    ar, ai = lr * dt, li * dt
    ks = jnp.arange(tc + 1, dtype=F32)[None, :, None]
    mag = jnp.exp(ks * ar[:, None, :])
    pr = mag * jnp.cos(ks * ai[:, None, :])
    pi = mag * jnp.sin(ks * ai[:, None, :])
    nr = jnp.expm1(ar) * jnp.cos(ai) - 2.0 * jnp.sin(0.5 * ai) ** 2
    ni = jnp.exp(ar) * jnp.sin(ai)
    den = lr * lr + li * li
    fr = (nr * lr + ni * li) / den
    fi = (ni * lr - nr * li) / den
    bbr = fr[..., None] * b_re - fi[..., None] * b_im
    bbi = fr[..., None] * b_im + fi[..., None] * b_re
    er = pr[..., None] * bbr[:, None] - pi[..., None] * bbi[:, None]
    ei = pr[..., None] * bbi[:, None] + pi[..., None] * bbr[:, None]
    kk = (jnp.einsum('gop,gkpi->gkoi', c_re, er[:, :tc], precision=HIGHEST)
          - jnp.einsum('gop,gkpi->gkoi', c_im, ei[:, :tc], precision=HIGHEST))
    gt = LANES // c
    nt = g // gt

    def tile_rows(x):
        a, sub = x.shape[1], x.shape[-1]
        return x.reshape(nt, gt, a, c, sub).transpose(0, 2, 1, 3, 4).reshape(nt, a * gt * c, sub)

    kt = kk.transpose(0, 1, 3, 2)[:, ::-1]
    zblk = jnp.zeros_like(kt[:, :1])
    g2_ab = (tile_rows(jnp.concatenate([kt, zblk], axis=1)), tile_rows(jnp.concatenate([zblk, kt], axis=1)))
    ws_ab = (tile_rows(er[:, tc - 1::-1].transpose(0, 1, 3, 2)), tile_rows(ei[:, tc - 1::-1].transpose(0, 1, 3, 2)))
    pr1, pi1 = pr[:, 1:, None, :], pi[:, 1:, None, :]
    qr = c_re[:, None] * pr1 - c_im[:, None] * pi1
    qi = c_re[:, None] * pi1 + c_im[:, None] * pr1
    wrt_ab = (tile_rows(qr), tile_rows(-qi))
    return g2_ab, ws_ab, wrt_ab, pr[:, tc].reshape(nt, 1, gt * p), pi[:, tc].reshape(nt, 1, gt * p)


def _s5_tables(lam_re, lam_im, log_dt, b_re, b_im, c_re, c_im):
    depth, groups = lam_re.shape[:2]
    c = b_re.shape[-1]
    gt = LANES // c
    g2_ab, ws_ab, wrt_ab, lr, li = jax.vmap(_s5_factors)(lam_re, lam_im, log_dt, b_re, b_im, c_re, c_im)

    def expand(ab):
        a, b = (x.reshape((depth * x.shape[1],) + x.shape[2:]) for x in ab)
        out = _lane_diag(a, b, gt=gt, c=c)
        return out.reshape((depth, out.shape[0] // depth) + out.shape[1:])

    g2, ws, wrt = expand(g2_ab), expand(ws_ab), expand(wrt_ab)
    return [(g2[l], ws[l], wrt[l], lr[l], li[l]) for l in range(depth)]


def _s5_scan(u, tables, *, seq):
    t, w = u.shape
    g2, ws, wrt, lr, li = tables
    nt, sp = lr.shape[0], lr.shape[-1]
    rows = seq // S5_CHUNK
    tile = lambda k, b: (k, 0, 0)
    return pl.pallas_call(
        functools.partial(_s5_kernel, tc=S5_CHUNK),
        out_shape=jax.ShapeDtypeStruct((t, w), F32),
        grid=(nt, t // seq),
        in_specs=[pl.BlockSpec((seq, LANES), lambda k, b: (b, k)),
                  pl.BlockSpec((None,) + g2.shape[1:], tile),
                  pl.BlockSpec((None,) + ws.shape[1:], tile),
                  pl.BlockSpec((None,) + wrt.shape[1:], tile),
                  pl.BlockSpec((None, 1, sp), tile),
                  pl.BlockSpec((None, 1, sp), tile)],
        out_specs=pl.BlockSpec((seq, LANES), lambda k, b: (b, k)),
        scratch_shapes=[pltpu.VMEM((rows, sp), F32)] * 4,
        compiler_params=_cparams(("parallel", "parallel")),
        name="s5_scan",
    )(u, g2, ws, wrt, lr, li)


def _s5_post_kernel(y_ref, u_ref, d_ref, w_ref, b_ref, o_ref):
    y = y_ref[...] + d_ref[...] * u_ref[...].astype(F32)
    c0 = math.sqrt(2.0 / math.pi)
    y = 0.5 * y * (1.0 + jnp.tanh(c0 * (y + 0.044715 * (y * y * y))))
    gate = jnp.dot(y.astype(BF16), w_ref[...], preferred_element_type=F32) + b_ref[...]
    o_ref[...] = (y * _sigmoid(gate)).astype(o_ref.dtype)


def _s5_post(y, proj, d_skip, w_glu, b_glu, *, tm):
    t, w = y.shape
    return pl.pallas_call(
        _s5_post_kernel,
        out_shape=jax.ShapeDtypeStruct((t, w), BF16),
        grid=(t // tm,),
        in_specs=[pl.BlockSpec((tm, w), lambda i: (i, 0)),
                  pl.BlockSpec((tm, w), lambda i: (i, 0)),
                  pl.BlockSpec((1, w), lambda i: (0, 0)),
                  pl.BlockSpec((w, w), lambda i: (0, 0)),
                  pl.BlockSpec((1, w), lambda i: (0, 0))],
        out_specs=pl.BlockSpec((tm, w), lambda i: (i, 0)),
        compiler_params=_cparams(("parallel",)),
        name="s5_post",
    )(y, proj, d_skip.reshape(1, w), w_glu, b_glu.reshape(1, w))


def _lru_kernel(x_ref, cw_ref, cb_ref, wr_ref, br_ref, wi_ref, bi_ref, lam_ref, o_ref,
                xbuf_ref, a_ref, g_ref, h_ref, *, tl, kw):
    pad = 8
    w = x_ref.shape[1]

    @pl.when(pl.program_id(1) == 0)
    def _():
        xbuf_ref[pl.ds(0, pad), :] = jnp.zeros((pad, w), F32)
        h_ref[...] = jnp.zeros_like(h_ref)

    @pl.when(pl.program_id(1) > 0)
    def _():
        xbuf_ref[pl.ds(0, pad), :] = xbuf_ref[pl.ds(tl, pad), :]

    xbuf_ref[pl.ds(pad, tl), :] = x_ref[...].astype(F32)
    cw = cw_ref[...]
    xc = cb_ref[...] + xbuf_ref[pl.ds(pad - (kw - 1), tl), :] * cw[0:1, :]
    for k in range(1, kw):
        xc = xc + xbuf_ref[pl.ds(pad - (kw - 1) + k, tl), :] * cw[k:k + 1, :]
    xcb = xc.astype(BF16)
    nsup = w // LRU_SUPER
    rs, is_ = [], []
    for s in range(nsup):
        xs = xcb[:, s * LRU_SUPER:(s + 1) * LRU_SUPER]
        rs.append(jnp.dot(xs, wr_ref[s], preferred_element_type=F32))
        is_.append(jnp.dot(xs, wi_ref[s], preferred_element_type=F32))
    r = _sigmoid(jnp.concatenate(rs, axis=1) + br_ref[...])
    i = _sigmoid(jnp.concatenate(is_, axis=1) + bi_ref[...])
    log_a = (-LRU_C) * r * _softplus(-lam_ref[...])
    a_ref[...] = jnp.exp(log_a)
    th = jnp.tanh(log_a)
    g_ref[...] = jnp.sqrt((-2.0 * th) / (1.0 - th)) * (i * xc)

    row8 = lax.broadcasted_iota(jnp.int32, (8, w), 0)

    def body(t8, h):
        base = pl.multiple_of(t8 * 8, 8)
        a = a_ref[pl.ds(base, 8), :]
        b = g_ref[pl.ds(base, 8), :]
        for sh in (1, 2, 4):
            ok = row8 >= sh
            a_prev = jnp.where(ok, pltpu.roll(a, sh, axis=0), 1.0)
            b_prev = jnp.where(ok, pltpu.roll(b, sh, axis=0), 0.0)
            b = a * b_prev + b
            a = a * a_prev
        hs = a * h + b
        g_ref[pl.ds(base, 8), :] = hs
        return hs[7:8, :]

    h_ref[...] = lax.fori_loop(0, tl // 8, body, h_ref[...])
    o_ref[...] = g_ref[...].astype(o_ref.dtype)


def _block_diag_super(wblk):
    n, k, _ = wblk.shape
    per = LRU_SUPER // k
    wb = wblk.reshape(n // per, per, k, k)
    eye = jnp.eye(per, dtype=wblk.dtype)
    sup = wb[:, :, :, None, :] * eye[None, :, None, :, None]
    return sup.reshape(n // per, LRU_SUPER, LRU_SUPER)


def _lru(proj3, col_block, conv_w, conv_b, w_r, b_r, w_i, b_i, lam, *, tl):
    b, l, _ = proj3.shape
    w = conv_w.shape[1]
    kw = conv_w.shape[0]
    wr = _block_diag_super(w_r).astype(BF16)
    wi = _block_diag_super(w_i).astype(BF16)
    vec = lambda bi, ti: (0, 0)
    full3 = lambda bi, ti: (0, 0, 0)
    return pl.pallas_call(
        functools.partial(_lru_kernel, tl=tl, kw=kw),
        out_shape=jax.ShapeDtypeStruct((b, l, w), BF16),
        grid=(b, l // tl),
        in_specs=[pl.BlockSpec((None, tl, w), lambda bi, ti: (bi, ti, col_block)),
                  pl.BlockSpec((kw, w), vec),
                  pl.BlockSpec((1, w), vec),
                  pl.BlockSpec(wr.shape, full3),
                  pl.BlockSpec((1, w), vec),
                  pl.BlockSpec(wi.shape, full3),
                  pl.BlockSpec((1, w), vec),
                  pl.BlockSpec((1, w), vec)],
        out_specs=pl.BlockSpec((None, tl, w), lambda bi, ti: (bi, ti, 0)),
        scratch_shapes=[pltpu.VMEM((tl + 8, w), F32), pltpu.VMEM((tl, w), F32),
                        pltpu.VMEM((tl, w), F32), pltpu.VMEM((1, w), F32)],
        compiler_params=_cparams(("parallel", "arbitrary")),
        name="rglru",
    )(proj3, conv_w, conv_b.reshape(1, w), wr, b_r.reshape(1, w), wi, b_i.reshape(1, w), lam.reshape(1, w))


def _attn_kernel(q_ref, k_ref, v_ref, o_ref, *, tq, hp):
    qi = pl.program_id(2)
    d = HEAD_DIM
    row = lax.broadcasted_iota(jnp.int32, (tq, tq), 0)
    col = lax.broadcasted_iota(jnp.int32, (tq, tq), 1)
    suffix = jnp.where(row > col, 1.0, 0.0).astype(BF16)
    causal = col < row
    qs = [q_ref[:, h * d:(h + 1) * d] for h in range(hp)]

    def blocks(j_lo, nb, state, masked):
        ks = pl.multiple_of(j_lo * tq, tq)
        heads = range(hp)
        order = range(nb - 1, -1, -1)
        zs = [lax.dot_general(qs[h], k_ref[pl.ds(ks, nb * tq), h * d:(h + 1) * d], (((1,), (1,)), ((), ())),
                              preferred_element_type=F32) for h in heads]
        sps = []
        for h in heads:
            sp = _softplus(zs[h])
            sps.append(jnp.where(causal, sp, 0.0) if masked else sp)
        sss = [[jnp.dot(sps[h][:, n * tq:(n + 1) * tq].astype(BF16), suffix, preferred_element_type=F32)
                for n in range(nb)] for h in heads]
        out = []
        for h in heads:
            carry, acc = state[2 * h], state[2 * h + 1]
            ws = [None] * nb
            for n in order:
                cols = slice(n * tq, (n + 1) * tq)
                w = jnp.exp(zs[h][:, cols] - sps[h][:, cols] - sss[h][n] - carry)
                ws[n] = (jnp.where(causal, w, 0.0) if masked else w).astype(BF16)
                carry = carry + jnp.sum(sps[h][:, cols], axis=-1, keepdims=True)
            acc = acc + jnp.dot(jnp.concatenate(ws, axis=1), v_ref[pl.ds(ks, nb * tq), h * d:(h + 1) * d],
                                preferred_element_type=F32)
            out.extend((carry, acc))
        return tuple(out)

    state = blocks(qi, 1, (jnp.zeros((tq, 1), F32), jnp.zeros((tq, d), F32)) * hp, True)

    def keep_going(c):
        return jnp.logical_and(c[0] < qi, c[1] == 0)

    def sweep(c):
        jj = c[0]
        new = blocks(qi - 1 - jj, 1, c[2:], False)
        low = new[0]
        for h in range(1, hp):
            low = jnp.minimum(low, new[2 * h])
        done = (jnp.min(low) > ATTN_UNDERFLOW).astype(jnp.int32)
        return (jj + 1, done) + new

    state = lax.while_loop(keep_going, sweep, (jnp.int32(0), jnp.int32(0)) + state)[2:]
    for h in range(hp):
        o_ref[:, h * d:(h + 1) * d] = state[2 * h + 1].astype(o_ref.dtype)


def _attention(proj3, q_blk, k_blk, v_blk, n_heads, *, tq, hp):
    b, l, _ = proj3.shape
    d = HEAD_DIM
    return pl.pallas_call(
        functools.partial(_attn_kernel, tq=tq, hp=hp),
        out_shape=jax.ShapeDtypeStruct((b, l, n_heads * d), BF16),
        grid=(b, n_heads // hp, l // tq),
        in_specs=[pl.BlockSpec((None, tq, hp * d), lambda bi, hi, qi: (bi, qi, q_blk // hp + hi)),
                  pl.BlockSpec((None, l, hp * d), lambda bi, hi, qi: (bi, 0, k_blk // hp + hi)),
                  pl.BlockSpec((None, l, hp * d), lambda bi, hi, qi: (bi, 0, v_blk // hp + hi))],
        out_specs=pl.BlockSpec((None, tq, hp * d), lambda bi, hi, qi: (bi, qi, hi)),
        compiler_params=_cparams(("parallel", "parallel", "arbitrary")),
        name="stick_attn",
    )(proj3, proj3, proj3)


def _merge_kernel(ya_ref, yb_ref, yc_ref, wa_ref, wb_ref, wc_ref, ga_ref, gb_ref, gc_ref, bg_ref, o_ref):
    bg = bg_ref[...]
    out = None
    for n, (y_ref, w_ref, gl_ref) in enumerate(((ya_ref, wa_ref, ga_ref), (yb_ref, wb_ref, gb_ref),
                                                (yc_ref, wc_ref, gc_ref))):
        gate = _sigmoid(gl_ref[...].astype(F32) + bg[n:n + 1, :])
        term = gate * jnp.dot(y_ref[...], w_ref[...], preferred_element_type=F32)
        out = term if out is None else out + term
    o_ref[...] = out.astype(o_ref.dtype)


def _merge(ys, ws, proj, gate_col, b_gate, *, tm, tn):
    t, w = ys[0].shape
    d = ws[0].shape[1]
    gblk = gate_col // tn
    per = d // tn
    y_spec = pl.BlockSpec((tm, w), lambda i, j: (i, 0))
    w_spec = pl.BlockSpec((w, tn), lambda i, j: (0, j))
    gate_specs = [pl.BlockSpec((tm, tn), functools.partial(lambda i, j, n: (i, gblk + n * per + j), n=n))
                  for n in range(N_BRANCH)]
    return pl.pallas_call(
        _merge_kernel,
        out_shape=jax.ShapeDtypeStruct((t, d), BF16),
        grid=(t // tm, d // tn),
        in_specs=[y_spec] * 3 + [w_spec] * 3 + gate_specs
                 + [pl.BlockSpec((N_BRANCH, tn), lambda i, j: (0, j))],
        out_specs=pl.BlockSpec((tm, tn), lambda i, j: (i, j)),
        compiler_params=_cparams(("parallel", "arbitrary")),
        name="merge",
    )(*ys, *ws, proj, proj, proj, b_gate.reshape(N_BRANCH, d))


def _outproj_kernel(m_ref, w_ref, h_ref, g_ref, ho_ref, hn_ref):
    h = h_ref[...] + jnp.dot(m_ref[...], w_ref[...], preferred_element_type=F32)
    ho_ref[...] = h
    hn_ref[...] = _rms_rows(h, g_ref[...]).astype(hn_ref.dtype)


def _outproj(merged, w_out, h, g, *, tm):
    t, d = h.shape
    row = lambda i: (i, 0)
    return pl.pallas_call(
        _outproj_kernel,
        out_shape=(jax.ShapeDtypeStruct((t, d), F32), jax.ShapeDtypeStruct((t, d), BF16)),
        grid=(t // tm,),
        in_specs=[pl.BlockSpec((tm, d), row),
                  pl.BlockSpec((d, d), lambda i: (0, 0)),
                  pl.BlockSpec((tm, d), row),
                  pl.BlockSpec((1, d), lambda i: (0, 0))],
        out_specs=(pl.BlockSpec((tm, d), row), pl.BlockSpec((tm, d), row)),
        compiler_params=_cparams(("parallel",)),
        name="outproj",
    )(merged, w_out, h, g.reshape(1, d))


def _mlp_kernel(x_ref, wu_ref, wd_ref, h_ref, g_ref, o_ref, *, final_norm, parts):
    f = pl.program_id(1)

    @pl.when(f == 0)
    def _():
        o_ref[...] = h_ref[...]

    x = x_ref[...]
    part = wu_ref.shape[1] // parts
    hids = [jnp.maximum(jnp.dot(x, wu_ref[:, n * part:(n + 1) * part], preferred_element_type=F32), 0.0)
            for n in range(parts)]
    hid = jnp.concatenate([(hd * hd).astype(BF16) for hd in hids], axis=1)
    o_ref[...] += jnp.dot(hid, wd_ref[...], preferred_element_type=F32)

    if final_norm:
        @pl.when(f == pl.num_programs(1) - 1)
        def _():
            o_ref[...] = _rms_rows(o_ref[...], g_ref[...])


def _mlp(hn, w_up, w_down, h, g, *, tm, tf, final_norm):
    t, d = h.shape
    ff = w_up.shape[1]
    row = lambda i, f: (i, 0)
    return pl.pallas_call(
        functools.partial(_mlp_kernel, final_norm=final_norm, parts=2),
        out_shape=jax.ShapeDtypeStruct((t, d), F32),
        grid=(t // tm, ff // tf),
        in_specs=[pl.BlockSpec((tm, d), row),
                  pl.BlockSpec((d, tf), lambda i, f: (0, f)),
                  pl.BlockSpec((tf, d), lambda i, f: (f, 0)),
                  pl.BlockSpec((tm, d), row),
                  pl.BlockSpec((1, d), lambda i, f: (0, 0))],
        out_specs=pl.BlockSpec((tm, d), row),
        compiler_params=_cparams(("parallel", "arbitrary")),
        name="mlp",
    )(hn, w_up, w_down, h, g.reshape(1, d))


def _layer(h, bsz, seq, p, *, final_g):
    t, d = h.shape
    s5_w = p["s5_d"].shape[0]
    lru_w = p["lru_lambda"].shape[0]
    attn_w = p["w_br_attn"].shape[0]
    n_heads = attn_w // HEAD_DIM
    off_lru = s5_w
    off_q = off_lru + lru_w
    off_k = off_q + attn_w
    off_v = off_k + attn_w
    off_gate = off_v + attn_w

    col = jnp.arange(p["w_in"].shape[1])
    q_scale = jnp.where((col >= off_q) & (col < off_k), HEAD_DIM ** -0.5, 1.0).astype(F32)
    proj, u_s5 = _inproj(h, p["norm_mix_g"], (p["w_in"] * q_scale[None, :]).astype(BF16), tm=1024, tn=s5_w)
    proj3 = proj.reshape(bsz, seq, proj.shape[1])

    y_ssm = _s5_scan(u_s5, p["s5_tables"], seq=seq)
    y_s5 = _s5_post(y_ssm, u_s5, p["s5_d"], p["s5_w_glu"].astype(BF16), p["s5_b_glu"], tm=1024)

    y_lru = _lru(proj3, off_lru // lru_w, p["lru_conv_w"], p["lru_conv_b"], p["lru_w_r"], p["lru_b_r"],
                 p["lru_w_i"], p["lru_b_i"], p["lru_lambda"], tl=512).reshape(t, lru_w)

    y_attn = _attention(proj3, off_q // HEAD_DIM, off_k // HEAD_DIM, off_v // HEAD_DIM, n_heads,
                        tq=256, hp=8).reshape(t, attn_w)

    merged = _merge((y_s5, y_lru, y_attn),
                    (p["w_br_s5"].astype(BF16), p["w_br_lru"].astype(BF16), p["w_br_attn"].astype(BF16)),
                    proj, off_gate, p["b_gate"], tm=1024, tn=512)
    h, hn = _outproj(merged, p["w_out"].astype(BF16), h, p["norm_mlp_g"], tm=512)
    g_last = p["norm_mlp_g"] if final_g is None else final_g
    return _mlp(hn, p["w_up"].astype(BF16), p["w_down"].astype(BF16), h, g_last,
                tm=512, tf=1024, final_norm=final_g is not None)


_LAYER_PARAMS = ("norm_mix_g", "w_in", "b_gate", "s5_lam_re", "s5_lam_im", "s5_log_dt", "s5_b_re", "s5_b_im",
                 "s5_c_re", "s5_c_im", "s5_d", "s5_w_glu", "s5_b_glu", "lru_conv_w", "lru_conv_b", "lru_w_r",
                 "lru_b_r", "lru_w_i", "lru_b_i", "lru_lambda", "w_br_s5", "w_br_lru", "w_br_attn", "w_out",
                 "norm_mlp_g", "w_up", "w_down")


def kernel(x, norm_mix_g, w_in, b_gate, s5_lam_re, s5_lam_im, s5_log_dt, s5_b_re, s5_b_im, s5_c_re, s5_c_im,
           s5_d, s5_w_glu, s5_b_glu, lru_conv_w, lru_conv_b, lru_w_r, lru_b_r, lru_w_i, lru_b_i, lru_lambda,
           w_br_s5, w_br_lru, w_br_attn, w_out, norm_mlp_g, w_up, w_down, final_norm_g):
    stacked = dict(zip(_LAYER_PARAMS, (norm_mix_g, w_in, b_gate, s5_lam_re, s5_lam_im, s5_log_dt, s5_b_re,
                                       s5_b_im, s5_c_re, s5_c_im, s5_d, s5_w_glu, s5_b_glu, lru_conv_w,
                                       lru_conv_b, lru_w_r, lru_b_r, lru_w_i, lru_b_i, lru_lambda, w_br_s5,
                                       w_br_lru, w_br_attn, w_out, norm_mlp_g, w_up, w_down)))
    bsz, seq, d = x.shape
    depth = w_in.shape[0]
    h = x.reshape(bsz * seq, d).astype(F32)
    s5_tables = _s5_tables(s5_lam_re, s5_lam_im, s5_log_dt, s5_b_re, s5_b_im, s5_c_re, s5_c_im)
    for layer in range(depth):
        p = {k: v[layer] for k, v in stacked.items()}
        p["s5_tables"] = s5_tables[layer]
        h = _layer(h, bsz, seq, p, final_g=final_norm_g if layer == depth - 1 else None)
    return h.reshape(bsz, seq, d).astype(x.dtype)
```

```python
import functools
import math

import jax
import jax.numpy as jnp
from jax import lax
from jax.experimental import pallas as pl
from jax.experimental.pallas import tpu as pltpu

F32 = jnp.float32
BF16 = jnp.bfloat16

EPS = 1e-6
LANES = 128
HEAD_DIM = 128
ATTN_UNDERFLOW = 128.0
S5_GROUP = 16
S5_CHUNK = 16
LRU_BLOCK = 64
LRU_SUPER = 256
LRU_C = 8.0
N_BRANCH = 3
LOG2E = 1.4426950408889634
VMEM_LIMIT = 56 * 1024 * 1024
HIGHEST = lax.Precision.HIGHEST


def _cparams(sem):
    return pltpu.CompilerParams(dimension_semantics=sem, vmem_limit_bytes=VMEM_LIMIT)


def _softplus(z):
    return jnp.maximum(z, 0.0) + jnp.log(1.0 + jnp.exp2(jnp.abs(z) * (-LOG2E)))


def _sigmoid(z):
    return 1.0 / (1.0 + jnp.exp(-z))


def _rms_rows(x, g):
    ms = jnp.mean(x * x, axis=-1, keepdims=True)
    return (x * lax.rsqrt(ms + EPS)) * g


def _inproj_kernel(x_ref, g_ref, w_ref, o_ref, o32_ref, xn_ref, *, row_chunk):
    j = pl.program_id(1)

    @pl.when(j == 0)
    def _():
        g = g_ref[...]
        w = w_ref[...]
        for r in range(0, x_ref.shape[0], row_chunk):
            xn = _rms_rows(x_ref[r:r + row_chunk, :], g).astype(BF16)
            xn_ref[r:r + row_chunk, :] = xn
            acc = jnp.dot(xn, w, preferred_element_type=F32)
            o_ref[r:r + row_chunk, :] = acc.astype(o_ref.dtype)
            o32_ref[r:r + row_chunk, :] = acc

    @pl.when(j > 0)
    def _():
        o_ref[...] = jnp.dot(xn_ref[...], w_ref[...], preferred_element_type=F32).astype(o_ref.dtype)


def _inproj(h, g, w, layer, *, tm, tn):
    t, d = h.shape
    n = w.shape[-1]
    return pl.pallas_call(
        functools.partial(_inproj_kernel, row_chunk=256),
        out_shape=(jax.ShapeDtypeStruct((t, n), BF16), jax.ShapeDtypeStruct((t, tn), F32)),
        grid=(t // tm, n // tn),
        in_specs=[pl.BlockSpec((tm, d), lambda i, j: (i, 0)),
                  pl.BlockSpec((1, d), lambda i, j: (0, 0)),
                  pl.BlockSpec((None, d, tn), lambda i, j: (layer, 0, j))],
        out_specs=(pl.BlockSpec((tm, tn), lambda i, j: (i, j)),
                   pl.BlockSpec((tm, tn), lambda i, j: (i, 0))),
        scratch_shapes=[pltpu.VMEM((tm, d), BF16)],
        compiler_params=_cparams(("parallel", "arbitrary")),
        name="inproj",
    )(h, g.reshape(1, d), w)


def _s5_kernel(u_ref, g2_ref, ws_ref, wrt_ref, lr_ref, li_ref, o_ref,
               xre_ref, xim_ref, hre_ref, him_ref, *, tc):
    rows = u_ref.shape[0] // tc
    lanes = u_ref.shape[1]
    xc = jnp.concatenate([u_ref[pl.ds(t, rows, stride=tc), :].astype(BF16) for t in range(tc)], axis=1)
    lr = lr_ref[...]
    li = li_ref[...]
    sp = lr.shape[1]
    x = jnp.dot(xc, ws_ref[...], preferred_element_type=F32)
    xre_ref[...] = x[:, :sp]
    xim_ref[...] = x[:, sp:]

    def body(k, carry):
        hr, hi = carry
        base = pl.multiple_of(k * 8, 8)
        for r in range(8):
            hre_ref[pl.ds(base + r, 1), :] = hr
            him_ref[pl.ds(base + r, 1), :] = hi
            xr = xre_ref[pl.ds(base + r, 1), :]
            xi = xim_ref[pl.ds(base + r, 1), :]
            hr, hi = lr * hr - li * hi + xr, lr * hi + li * hr + xi
        return hr, hi

    zero = jnp.zeros(lr.shape, F32)
    lax.fori_loop(0, rows // 8, body, (zero, zero))

    h0 = jnp.concatenate([hre_ref[...].astype(BF16), him_ref[...].astype(BF16)], axis=1)
    yr = lax.dot_general(h0, wrt_ref[...], (((1,), (1,)), ((), ())), preferred_element_type=F32)
    nblk = g2_ref.shape[0] // lanes
    for m in range(tc // 2):
        kdim = (2 * m + 2) * lanes
        y = jnp.dot(xc[:, :kdim], g2_ref[(nblk * lanes - kdim):, :], preferred_element_type=F32)
        y = y + yr[:, 2 * m * lanes:(2 * m + 2) * lanes]
        o_ref[pl.ds(2 * m, rows, stride=tc), :] = y[:, :lanes]
        o_ref[pl.ds(2 * m + 1, rows, stride=tc), :] = y[:, lanes:]


def _lane_diag_kernel(a_ref, b_ref, o_ref, *, gt, c):
    rows, sub = a_ref.shape
    width = gt * sub
    src = lax.broadcasted_iota(jnp.int32, (sub, width), 0)
    dst = lax.broadcasted_iota(jnp.int32, (sub, width), 1)
    spread = jnp.where(dst % sub == src, 1.0, 0.0).astype(BF16)
    row_g = (lax.broadcasted_iota(jnp.int32, (rows, width), 0) // c) % gt
    col_g = lax.broadcasted_iota(jnp.int32, (rows, width), 1) // sub
    keep = row_g == col_g
    for n, x_ref in enumerate((a_ref, b_ref)):
        tiled = jnp.dot(x_ref[...].astype(BF16), spread, preferred_element_type=F32)
        o_ref[:, n * width:(n + 1) * width] = jnp.where(keep, tiled, 0.0).astype(o_ref.dtype)


def _lane_diag(a, b, *, gt, c):
    nt, rows, sub = a.shape
    spec = pl.BlockSpec((None, rows, sub), lambda k: (k, 0, 0))
    return pl.pallas_call(
        functools.partial(_lane_diag_kernel, gt=gt, c=c),
        out_shape=jax.ShapeDtypeStruct((nt, rows, 2 * gt * sub), BF16),
        grid=(nt,),
        in_specs=[spec, spec],
        out_specs=pl.BlockSpec((None, rows, 2 * gt * sub), lambda k: (k, 0, 0)),
        compiler_params=_cparams(("parallel",)),
        name="s5_lane_diag",
    )(a, b)


def _s5_factors(lam_re, lam_im, log_dt, b_re, b_im, c_re, c_im):
    tc = S5_CHUNK
    g, p = lam_re.shape
    c = b_re.shape[-1]
    lr = lam_re.astype(F32)
    li = lam_im.astype(F32)
    dt = jnp.exp(log_dt.astype(F32))[:, None]
    ar, ai = lr * dt, li * dt
    ks = jnp.arange(tc + 1, dtype=F32)[None, :, None]
    mag = jnp.exp(ks * ar[:, None, :])
    pr = mag * jnp.cos(ks * ai[:, None, :])
    pi = mag * jnp.sin(ks * ai[:, None, :])
    nr = jnp.expm1(ar) * jnp.cos(ai) - 2.0 * jnp.sin(0.5 * ai) ** 2
    ni = jnp.exp(ar) * jnp.sin(ai)
    den = lr * lr + li * li
    fr = (nr * lr + ni * li) / den
    fi = (ni * lr - nr * li) / den
    bbr = fr[..., None] * b_re - fi[..., None] * b_im
    bbi = fr[..., None] * b_im + fi[..., None] * b_re
    er = pr[..., None] * bbr[:, None] - pi[..., None] * bbi[:, None]
    ei = pr[..., None] * bbi[:, None] + pi[..., None] * bbr[:, None]
    kk = (jnp.einsum('gop,gkpi->gkoi', c_re, er[:, :tc], precision=HIGHEST)
          - jnp.einsum('gop,gkpi->gkoi', c_im, ei[:, :tc], precision=HIGHEST))
    gt = LANES // c
    nt = g // gt

    def tile_rows(x):
        a, sub = x.shape[1], x.shape[-1]
        return x.reshape(nt, gt, a, c, sub).transpose(0, 2, 1, 3, 4).reshape(nt, a * gt * c, sub)

    kt = kk.transpose(0, 1, 3, 2)[:, ::-1]
    zblk = jnp.zeros_like(kt[:, :1])
    g2_ab = (tile_rows(jnp.concatenate([kt, zblk], axis=1)), tile_rows(jnp.concatenate([zblk, kt], axis=1)))
    ws_ab = (tile_rows(er[:, tc - 1::-1].transpose(0, 1, 3, 2)), tile_rows(ei[:, tc - 1::-1].transpose(0, 1, 3, 2)))
    pr1, pi1 = pr[:, 1:, None, :], pi[:, 1:, None, :]
    qr = c_re[:, None] * pr1 - c_im[:, None] * pi1
    qi = c_re[:, None] * pi1 + c_im[:, None] * pr1
    wrt_ab = (tile_rows(qr), tile_rows(-qi))
    return g2_ab, ws_ab, wrt_ab, pr[:, tc].reshape(nt, 1, gt * p), pi[:, tc].reshape(nt, 1, gt * p)


def _s5_tables(lam_re, lam_im, log_dt, b_re, b_im, c_re, c_im):
    depth, groups = lam_re.shape[:2]
    c = b_re.shape[-1]
    gt = LANES // c
    g2_ab, ws_ab, wrt_ab, lr, li = jax.vmap(_s5_factors)(lam_re, lam_im, log_dt, b_re, b_im, c_re, c_im)

    def expand(ab):
        a, b = (x.reshape((depth * x.shape[1],) + x.shape[2:]) for x in ab)
        out = _lane_diag(a, b, gt=gt, c=c)
        return out.reshape((depth, out.shape[0] // depth) + out.shape[1:])

    return expand(g2_ab), expand(ws_ab), expand(wrt_ab), lr, li


def _s5_scan(u, tables, layer, *, seq):
    t, w = u.shape
    g2, ws, wrt, lr, li = tables
    nt, sp = lr.shape[1], lr.shape[-1]
    rows = seq // S5_CHUNK
    tile = lambda k, b: (layer, k, 0, 0)
    return pl.pallas_call(
        functools.partial(_s5_kernel, tc=S5_CHUNK),
        out_shape=jax.ShapeDtypeStruct((t, w), F32),
        grid=(nt, t // seq),
        in_specs=[pl.BlockSpec((seq, LANES), lambda k, b: (b, k)),
                  pl.BlockSpec((None, None) + g2.shape[2:], tile),
                  pl.BlockSpec((None, None) + ws.shape[2:], tile),
                  pl.BlockSpec((None, None) + wrt.shape[2:], tile),
                  pl.BlockSpec((None, None, 1, sp), tile),
                  pl.BlockSpec((None, None, 1, sp), tile)],
        out_specs=pl.BlockSpec((seq, LANES), lambda k, b: (b, k)),
        scratch_shapes=[pltpu.VMEM((rows, sp), F32)] * 4,
        compiler_params=_cparams(("parallel", "parallel")),
        name="s5_scan",
    )(u, g2, ws, wrt, lr, li)


def _s5_post_kernel(y_ref, u_ref, d_ref, w_ref, b_ref, o_ref):
    y = y_ref[...] + d_ref[...] * u_ref[...].astype(F32)
    c0 = math.sqrt(2.0 / math.pi)
    y = 0.5 * y * (1.0 + jnp.tanh(c0 * (y + 0.044715 * (y * y * y))))
    gate = jnp.dot(y.astype(BF16), w_ref[...], preferred_element_type=F32) + b_ref[...]
    o_ref[...] = (y * _sigmoid(gate)).astype(o_ref.dtype)


def _s5_post(y, u, d_skip, w_glu, b_glu, layer, *, tm):
    t, w = y.shape
    return pl.pallas_call(
        _s5_post_kernel,
        out_shape=jax.ShapeDtypeStruct((t, w), BF16),
        grid=(t // tm,),
        in_specs=[pl.BlockSpec((tm, w), lambda i: (i, 0)),
                  pl.BlockSpec((tm, w), lambda i: (i, 0)),
                  pl.BlockSpec((1, w), lambda i: (0, 0)),
                  pl.BlockSpec((None, w, w), lambda i: (layer, 0, 0)),
                  pl.BlockSpec((1, w), lambda i: (0, 0))],
        out_specs=pl.BlockSpec((tm, w), lambda i: (i, 0)),
        compiler_params=_cparams(("parallel",)),
        name="s5_post",
    )(y, u, d_skip.reshape(1, w), w_glu, b_glu.reshape(1, w))


def _lru_kernel(x_ref, cw_ref, cb_ref, wr_ref, br_ref, wi_ref, bi_ref, lam_ref, o_ref,
                xbuf_ref, a_ref, g_ref, h_ref, *, tl, kw):
    pad = 8
    w = x_ref.shape[1]

    @pl.when(pl.program_id(1) == 0)
    def _():
        xbuf_ref[pl.ds(0, pad), :] = jnp.zeros((pad, w), F32)
        h_ref[...] = jnp.zeros_like(h_ref)

    @pl.when(pl.program_id(1) > 0)
    def _():
        xbuf_ref[pl.ds(0, pad), :] = xbuf_ref[pl.ds(tl, pad), :]

    xbuf_ref[pl.ds(pad, tl), :] = x_ref[...].astype(F32)
    cw = cw_ref[...]
    xc = cb_ref[...] + xbuf_ref[pl.ds(pad - (kw - 1), tl), :] * cw[0:1, :]
    for k in range(1, kw):
        xc = xc + xbuf_ref[pl.ds(pad - (kw - 1) + k, tl), :] * cw[k:k + 1, :]
    xcb = xc.astype(BF16)
    nsup = w // LRU_SUPER
    rs, is_ = [], []
    for s in range(nsup):
        xs = xcb[:, s * LRU_SUPER:(s + 1) * LRU_SUPER]
        rs.append(jnp.dot(xs, wr_ref[s], preferred_element_type=F32))
        is_.append(jnp.dot(xs, wi_ref[s], preferred_element_type=F32))
    r = _sigmoid(jnp.concatenate(rs, axis=1) + br_ref[...])
    i = _sigmoid(jnp.concatenate(is_, axis=1) + bi_ref[...])
    log_a = (-LRU_C) * r * _softplus(-lam_ref[...])
    a_ref[...] = jnp.exp(log_a)
    th = jnp.tanh(log_a)
    g_ref[...] = jnp.sqrt((-2.0 * th) / (1.0 - th)) * (i * xc)

    row8 = lax.broadcasted_iota(jnp.int32, (8, w), 0)

    def body(t8, h):
        base = pl.multiple_of(t8 * 8, 8)
        a = a_ref[pl.ds(base, 8), :]
        b = g_ref[pl.ds(base, 8), :]
        for sh in (1, 2, 4):
            ok = row8 >= sh
            a_prev = jnp.where(ok, pltpu.roll(a, sh, axis=0), 1.0)
            b_prev = jnp.where(ok, pltpu.roll(b, sh, axis=0), 0.0)
            b = a * b_prev + b
            a = a * a_prev
        hs = a * h + b
        g_ref[pl.ds(base, 8), :] = hs
        return hs[7:8, :]

    h_ref[...] = lax.fori_loop(0, tl // 8, body, h_ref[...])
    o_ref[...] = g_ref[...].astype(o_ref.dtype)


def _block_diag_super(wblk):
    n, k, _ = wblk.shape
    per = LRU_SUPER // k
    wb = wblk.reshape(n // per, per, k, k)
    eye = jnp.eye(per, dtype=wblk.dtype)
    sup = wb[:, :, :, None, :] * eye[None, :, None, :, None]
    return sup.reshape(n // per, LRU_SUPER, LRU_SUPER)


def _lru(proj3, col_block, conv_w, conv_b, w_r, b_r, w_i, b_i, lam, *, tl):
    b, l, _ = proj3.shape
    w = conv_w.shape[1]
    kw = conv_w.shape[0]
    wr = _block_diag_super(w_r).astype(BF16)
    wi = _block_diag_super(w_i).astype(BF16)
    vec = lambda bi, ti: (0, 0)
    full3 = lambda bi, ti: (0, 0, 0)
    return pl.pallas_call(
        functools.partial(_lru_kernel, tl=tl, kw=kw),
        out_shape=jax.ShapeDtypeStruct((b, l, w), BF16),
        grid=(b, l // tl),
        in_specs=[pl.BlockSpec((None, tl, w), lambda bi, ti: (bi, ti, col_block)),
                  pl.BlockSpec((kw, w), vec),
                  pl.BlockSpec((1, w), vec),
                  pl.BlockSpec(wr.shape, full3),
                  pl.BlockSpec((1, w), vec),
                  pl.BlockSpec(wi.shape, full3),
                  pl.BlockSpec((1, w), vec),
                  pl.BlockSpec((1, w), vec)],
        out_specs=pl.BlockSpec((None, tl, w), lambda bi, ti: (bi, ti, 0)),
        scratch_shapes=[pltpu.VMEM((tl + 8, w), F32), pltpu.VMEM((tl, w), F32),
                        pltpu.VMEM((tl, w), F32), pltpu.VMEM((1, w), F32)],
        compiler_params=_cparams(("parallel", "arbitrary")),
        name="rglru",
    )(proj3, conv_w, conv_b.reshape(1, w), wr, b_r.reshape(1, w), wi, b_i.reshape(1, w), lam.reshape(1, w))


def _attn_kernel(q_ref, k_ref, v_ref, o_ref, *, tq, hp):
    qi = pl.program_id(2)
    d = HEAD_DIM
    row = lax.broadcasted_iota(jnp.int32, (tq, tq), 0)
    col = lax.broadcasted_iota(jnp.int32, (tq, tq), 1)
    suffix = jnp.where(row > col, 1.0, 0.0).astype(BF16)
    causal = col < row
    qs = [q_ref[:, h * d:(h + 1) * d] for h in range(hp)]

    def blocks(j_lo, nb, state, masked):
        ks = pl.multiple_of(j_lo * tq, tq)
        heads = range(hp)
        order = range(nb - 1, -1, -1)
        zs = [lax.dot_general(qs[h], k_ref[pl.ds(ks, nb * tq), h * d:(h + 1) * d], (((1,), (1,)), ((), ())),
                              preferred_element_type=F32) for h in heads]
        sps = []
        for h in heads:
            sp = _softplus(zs[h])
            sps.append(jnp.where(causal, sp, 0.0) if masked else sp)
        sss = [[jnp.dot(sps[h][:, n * tq:(n + 1) * tq].astype(BF16), suffix, preferred_element_type=F32)
                for n in range(nb)] for h in heads]
        out = []
        for h in heads:
            carry, acc = state[2 * h], state[2 * h + 1]
            ws = [None] * nb
            for n in order:
                cols = slice(n * tq, (n + 1) * tq)
                w = jnp.exp(zs[h][:, cols] - sps[h][:, cols] - sss[h][n] - carry)
                ws[n] = (jnp.where(causal, w, 0.0) if masked else w).astype(BF16)
                carry = carry + jnp.sum(sps[h][:, cols], axis=-1, keepdims=True)
            acc = acc + jnp.dot(jnp.concatenate(ws, axis=1), v_ref[pl.ds(ks, nb * tq), h * d:(h + 1) * d],
                                preferred_element_type=F32)
            out.extend((carry, acc))
        return tuple(out)

    state = blocks(qi, 1, (jnp.zeros((tq, 1), F32), jnp.zeros((tq, d), F32)) * hp, True)

    def keep_going(c):
        return jnp.logical_and(c[0] < qi, c[1] == 0)

    def sweep(c):
        jj = c[0]
        new = blocks(qi - 1 - jj, 1, c[2:], False)
        low = new[0]
        for h in range(1, hp):
            low = jnp.minimum(low, new[2 * h])
        done = (jnp.min(low) > ATTN_UNDERFLOW).astype(jnp.int32)
        return (jj + 1, done) + new

    state = lax.while_loop(keep_going, sweep, (jnp.int32(0), jnp.int32(0)) + state)[2:]
    for h in range(hp):
        o_ref[:, h * d:(h + 1) * d] = state[2 * h + 1].astype(o_ref.dtype)


def _attention(proj3, q_blk, k_blk, v_blk, n_heads, *, tq, hp):
    b, l, _ = proj3.shape
    d = HEAD_DIM
    return pl.pallas_call(
        functools.partial(_attn_kernel, tq=tq, hp=hp),
        out_shape=jax.ShapeDtypeStruct((b, l, n_heads * d), BF16),
        grid=(b, n_heads // hp, l // tq),
        in_specs=[pl.BlockSpec((None, tq, hp * d), lambda bi, hi, qi: (bi, qi, q_blk // hp + hi)),
                  pl.BlockSpec((None, l, hp * d), lambda bi, hi, qi: (bi, 0, k_blk // hp + hi)),
                  pl.BlockSpec((None, l, hp * d), lambda bi, hi, qi: (bi, 0, v_blk // hp + hi))],
        out_specs=pl.BlockSpec((None, tq, hp * d), lambda bi, hi, qi: (bi, qi, hi)),
        compiler_params=_cparams(("parallel", "parallel", "arbitrary")),
        name="stick_attn",
    )(proj3, proj3, proj3)


def _merge_kernel(ya_ref, yb_ref, yc_ref, wa_ref, wb_ref, wc_ref, ga_ref, gb_ref, gc_ref, bg_ref, o_ref):
    bg = bg_ref[...]
    out = None
    for n, (y_ref, w_ref, gl_ref) in enumerate(((ya_ref, wa_ref, ga_ref), (yb_ref, wb_ref, gb_ref),
                                                (yc_ref, wc_ref, gc_ref))):
        gate = _sigmoid(gl_ref[...].astype(F32) + bg[n:n + 1, :])
        term = gate * jnp.dot(y_ref[...], w_ref[...], preferred_element_type=F32)
        out = term if out is None else out + term
    o_ref[...] = out.astype(o_ref.dtype)


def _merge(ys, ws, proj, gate_col, b_gate, layer, *, tm, tn):
    t, w = ys[0].shape
    d = ws[0].shape[-1]
    gblk = gate_col // tn
    per = d // tn
    y_spec = pl.BlockSpec((tm, w), lambda i, j: (i, 0))
    w_spec = pl.BlockSpec((None, w, tn), lambda i, j: (layer, 0, j))
    gate_specs = [pl.BlockSpec((tm, tn), functools.partial(lambda i, j, n: (i, gblk + n * per + j), n=n))
                  for n in range(N_BRANCH)]
    return pl.pallas_call(
        _merge_kernel,
        out_shape=jax.ShapeDtypeStruct((t, d), BF16),
        grid=(t // tm, d // tn),
        in_specs=[y_spec] * 3 + [w_spec] * 3 + gate_specs
                 + [pl.BlockSpec((N_BRANCH, tn), lambda i, j: (0, j))],
        out_specs=pl.BlockSpec((tm, tn), lambda i, j: (i, j)),
        compiler_params=_cparams(("parallel", "arbitrary")),
        name="merge",
    )(*ys, *ws, proj, proj, proj, b_gate.reshape(N_BRANCH, d))


def _outproj_kernel(m_ref, w_ref, h_ref, g_ref, ho_ref, hn_ref):
    h = h_ref[...] + jnp.dot(m_ref[...], w_ref[...], preferred_element_type=F32)
    ho_ref[...] = h
    hn_ref[...] = _rms_rows(h, g_ref[...]).astype(hn_ref.dtype)


def _outproj(merged, w_out, h, g, layer, *, tm):
    t, d = h.shape
    row = lambda i: (i, 0)
    return pl.pallas_call(
        _outproj_kernel,
        out_shape=(jax.ShapeDtypeStruct((t, d), F32), jax.ShapeDtypeStruct((t, d), BF16)),
        grid=(t // tm,),
        in_specs=[pl.BlockSpec((tm, d), row),
                  pl.BlockSpec((None, d, d), lambda i: (layer, 0, 0)),
                  pl.BlockSpec((tm, d), row),
                  pl.BlockSpec((1, d), lambda i: (0, 0))],
        out_specs=(pl.BlockSpec((tm, d), row), pl.BlockSpec((tm, d), row)),
        compiler_params=_cparams(("parallel",)),
        name="outproj",
    )(merged, w_out, h, g.reshape(1, d))


def _mlp_kernel(x_ref, wu_ref, wd_ref, h_ref, g_ref, o_ref, *, final_norm, parts):
    f = pl.program_id(1)

    @pl.when(f == 0)
    def _():
        o_ref[...] = h_ref[...]

    x = x_ref[...]
    part = wu_ref.shape[1] // parts
    hids = [jnp.maximum(jnp.dot(x, wu_ref[:, n * part:(n + 1) * part], preferred_element_type=F32), 0.0)
            for n in range(parts)]
    hid = jnp.concatenate([(hd * hd).astype(BF16) for hd in hids], axis=1)
    o_ref[...] += jnp.dot(hid, wd_ref[...], preferred_element_type=F32)

    if final_norm:
        @pl.when(f == pl.num_programs(1) - 1)
        def _():
            o_ref[...] = _rms_rows(o_ref[...], g_ref[...])


def _mlp(hn, w_up, w_down, h, g, layer, *, tm, tf, final_norm):
    t, d = h.shape
    ff = w_up.shape[-1]
    row = lambda i, f: (i, 0)
    return pl.pallas_call(
        functools.partial(_mlp_kernel, final_norm=final_norm, parts=2),
        out_shape=jax.ShapeDtypeStruct((t, d), F32),
        grid=(t // tm, ff // tf),
        in_specs=[pl.BlockSpec((tm, d), row),
                  pl.BlockSpec((None, d, tf), lambda i, f: (layer, 0, f)),
                  pl.BlockSpec((None, tf, d), lambda i, f: (layer, f, 0)),
                  pl.BlockSpec((tm, d), row),
                  pl.BlockSpec((1, d), lambda i, f: (0, 0))],
        out_specs=pl.BlockSpec((tm, d), row),
        compiler_params=_cparams(("parallel", "arbitrary")),
        name="mlp",
    )(hn, w_up, w_down, h, g.reshape(1, d))


def _layer(h, bsz, seq, p, big, layer, *, final_g):
    t, d = h.shape
    s5_w = p["s5_d"].shape[0]
    lru_w = p["lru_lambda"].shape[0]
    attn_w = big["w_br_attn"].shape[1]
    n_heads = attn_w // HEAD_DIM
    off_lru = s5_w
    off_q = off_lru + lru_w
    off_k = off_q + attn_w
    off_v = off_k + attn_w
    off_gate = off_v + attn_w

    proj, u_s5 = _inproj(h, p["norm_mix_g"], big["w_in"], layer, tm=1024, tn=s5_w)
    proj3 = proj.reshape(bsz, seq, proj.shape[1])

    y_ssm = _s5_scan(u_s5, big["s5_tables"], layer, seq=seq)
    y_s5 = _s5_post(y_ssm, u_s5, p["s5_d"], big["s5_w_glu"], p["s5_b_glu"], layer, tm=1024)

    y_lru = _lru(proj3, off_lru // lru_w, p["lru_conv_w"], p["lru_conv_b"], p["lru_w_r"], p["lru_b_r"],
                 p["lru_w_i"], p["lru_b_i"], p["lru_lambda"], tl=512).reshape(t, lru_w)

    y_attn = _attention(proj3, off_q // HEAD_DIM, off_k // HEAD_DIM, off_v // HEAD_DIM, n_heads,
                        tq=256, hp=8).reshape(t, attn_w)

    merged = _merge((y_s5, y_lru, y_attn), (big["w_br_s5"], big["w_br_lru"], big["w_br_attn"]),
                    proj, off_gate, p["b_gate"], layer, tm=1024, tn=512)
    h, hn = _outproj(merged, big["w_out"], h, p["norm_mlp_g"], layer, tm=512)
    g_last = p["norm_mlp_g"] if final_g is None else final_g
    return _mlp(hn, big["w_up"], big["w_down"], h, g_last, layer,
                tm=512, tf=1024, final_norm=final_g is not None)


_SMALL_PARAMS = ("norm_mix_g", "b_gate", "s5_d", "s5_b_glu", "lru_conv_w", "lru_conv_b", "lru_w_r", "lru_b_r",
                 "lru_w_i", "lru_b_i", "lru_lambda", "norm_mlp_g")


def kernel(x, norm_mix_g, w_in, b_gate, s5_lam_re, s5_lam_im, s5_log_dt, s5_b_re, s5_b_im, s5_c_re, s5_c_im,
           s5_d, s5_w_glu, s5_b_glu, lru_conv_w, lru_conv_b, lru_w_r, lru_b_r, lru_w_i, lru_b_i, lru_lambda,
           w_br_s5, w_br_lru, w_br_attn, w_out, norm_mlp_g, w_up, w_down, final_norm_g):
    small = dict(zip(_SMALL_PARAMS, (norm_mix_g, b_gate, s5_d, s5_b_glu, lru_conv_w, lru_conv_b, lru_w_r, lru_b_r,
                                     lru_w_i, lru_b_i, lru_lambda, norm_mlp_g)))
    bsz, seq, d = x.shape
    depth = w_in.shape[0]
    off_q = s5_d.shape[1] + lru_lambda.shape[1]
    col = jnp.arange(w_in.shape[-1])
    q_scale = jnp.where((col >= off_q) & (col < off_q + w_br_attn.shape[1]), HEAD_DIM ** -0.5, 1.0).astype(F32)
    big = {"w_in": (w_in * q_scale).astype(BF16), "s5_w_glu": s5_w_glu.astype(BF16),
           "w_br_s5": w_br_s5.astype(BF16), "w_br_lru": w_br_lru.astype(BF16), "w_br_attn": w_br_attn.astype(BF16),
           "w_out": w_out.astype(BF16), "w_up": w_up.astype(BF16), "w_down": w_down.astype(BF16),
           "s5_tables": _s5_tables(s5_lam_re, s5_lam_im, s5_log_dt, s5_b_re, s5_b_im, s5_c_re, s5_c_im)}
    h = x.reshape(bsz * seq, d).astype(F32)
    for layer in range(depth):
        p = {k: v[layer] for k, v in small.items()}
        h = _layer(h, bsz, seq, p, big, layer, final_g=final_norm_g if layer == depth - 1 else None)
    return h.reshape(bsz, seq, d).astype(x.dtype)
```

```python
import functools
import math

import jax
import jax.numpy as jnp
from jax import lax
from jax.experimental import pallas as pl
from jax.experimental.pallas import tpu as pltpu

F32 = jnp.float32
BF16 = jnp.bfloat16

EPS = 1e-6
LANES = 128
HEAD_DIM = 128
ATTN_UNDERFLOW = 128.0
S5_GROUP = 16
S5_CHUNK = 16
LRU_BLOCK = 64
LRU_SUPER = 256
LRU_C = 8.0
N_BRANCH = 3
LOG2E = 1.4426950408889634
VMEM_LIMIT = 56 * 1024 * 1024
HIGHEST = lax.Precision.HIGHEST


def _cparams(sem):
    return pltpu.CompilerParams(dimension_semantics=sem, vmem_limit_bytes=VMEM_LIMIT)


def _softplus(z):
    return jnp.maximum(z, 0.0) + jnp.log(1.0 + jnp.exp2(jnp.abs(z) * (-LOG2E)))


def _sigmoid(z):
    return 1.0 / (1.0 + jnp.exp(-z))


def _rms_rows(x, g):
    ms = jnp.mean(x * x, axis=-1, keepdims=True)
    return (x * lax.rsqrt(ms + EPS)) * g


def _inproj_kernel(x_ref, g_ref, w_ref, o_ref, o32_ref, xn_ref, *, row_chunk):
    j = pl.program_id(1)

    @pl.when(j == 0)
    def _():
        g = g_ref[...]
        w = w_ref[...]
        for r in range(0, x_ref.shape[0], row_chunk):
            xn = _rms_rows(x_ref[r:r + row_chunk, :], g).astype(BF16)
            xn_ref[r:r + row_chunk, :] = xn
            acc = jnp.dot(xn, w, preferred_element_type=F32)
            o_ref[r:r + row_chunk, :] = acc.astype(o_ref.dtype)
            o32_ref[r:r + row_chunk, :] = acc

    @pl.when(j > 0)
    def _():
        o_ref[...] = jnp.dot(xn_ref[...], w_ref[...], preferred_element_type=F32).astype(o_ref.dtype)


def _inproj(h, g, w, layer, *, tm, tn):
    t, d = h.shape
    n = w.shape[-1]
    return pl.pallas_call(
        functools.partial(_inproj_kernel, row_chunk=256),
        out_shape=(jax.ShapeDtypeStruct((t, n), BF16), jax.ShapeDtypeStruct((t, tn), F32)),
        grid=(t // tm, n // tn),
        in_specs=[pl.BlockSpec((tm, d), lambda i, j: (i, 0)),
                  pl.BlockSpec((1, d), lambda i, j: (0, 0)),
                  pl.BlockSpec((None, d, tn), lambda i, j: (layer, 0, j))],
        out_specs=(pl.BlockSpec((tm, tn), lambda i, j: (i, j)),
                   pl.BlockSpec((tm, tn), lambda i, j: (i, 0))),
        scratch_shapes=[pltpu.VMEM((tm, d), BF16)],
        compiler_params=_cparams(("parallel", "arbitrary")),
        name="inproj",
    )(h, g.reshape(1, d), w)


def _lane_diag(a_ref, b_ref, o_ref, *, gt, c):
    rows, sub = a_ref.shape
    width = gt * sub
    src = lax.broadcasted_iota(jnp.int32, (sub, width), 0)
    dst = lax.broadcasted_iota(jnp.int32, (sub, width), 1)
    spread = jnp.where(dst % sub == src, 1.0, 0.0).astype(BF16)
    row_g = (lax.broadcasted_iota(jnp.int32, (rows, width), 0) // c) % gt
    col_g = lax.broadcasted_iota(jnp.int32, (rows, width), 1) // sub
    keep = row_g == col_g
    for n, x_ref in enumerate((a_ref, b_ref)):
        tiled = jnp.dot(x_ref[...].astype(BF16), spread, preferred_element_type=F32)
        o_ref[:, n * width:(n + 1) * width] = jnp.where(keep, tiled, 0.0).astype(o_ref.dtype)


def _s5_kernel(u_ref, g2a_ref, g2b_ref, wsa_ref, wsb_ref, wra_ref, wrb_ref, lr_ref, li_ref, o_ref,
               g2_ref, ws_ref, wrt_ref, xre_ref, xim_ref, hre_ref, him_ref, *, tc, gt, c):
    @pl.when(pl.program_id(1) == 0)
    def _():
        _lane_diag(g2a_ref, g2b_ref, g2_ref, gt=gt, c=c)
        _lane_diag(wsa_ref, wsb_ref, ws_ref, gt=gt, c=c)
        _lane_diag(wra_ref, wrb_ref, wrt_ref, gt=gt, c=c)

    rows = u_ref.shape[0] // tc
    lanes = u_ref.shape[1]
    xc = jnp.concatenate([u_ref[pl.ds(t, rows, stride=tc), :].astype(BF16) for t in range(tc)], axis=1)
    lr = lr_ref[...]
    li = li_ref[...]
    sp = lr.shape[1]
    x = jnp.dot(xc, ws_ref[...], preferred_element_type=F32)
    xre_ref[...] = x[:, :sp]
    xim_ref[...] = x[:, sp:]

    def body(k, carry):
        hr, hi = carry
        base = pl.multiple_of(k * 8, 8)
        for r in range(8):
            hre_ref[pl.ds(base + r, 1), :] = hr
            him_ref[pl.ds(base + r, 1), :] = hi
            xr = xre_ref[pl.ds(base + r, 1), :]
            xi = xim_ref[pl.ds(base + r, 1), :]
            hr, hi = lr * hr - li * hi + xr, lr * hi + li * hr + xi
        return hr, hi

    zero = jnp.zeros(lr.shape, F32)
    lax.fori_loop(0, rows // 8, body, (zero, zero))

    h0 = jnp.concatenate([hre_ref[...].astype(BF16), him_ref[...].astype(BF16)], axis=1)
    yr = lax.dot_general(h0, wrt_ref[...], (((1,), (1,)), ((), ())), preferred_element_type=F32)
    nblk = g2_ref.shape[0] // lanes
    for m in range(tc // 2):
        kdim = (2 * m + 2) * lanes
        y = jnp.dot(xc[:, :kdim], g2_ref[(nblk * lanes - kdim):, :], preferred_element_type=F32)
        y = y + yr[:, 2 * m * lanes:(2 * m + 2) * lanes]
        o_ref[pl.ds(2 * m, rows, stride=tc), :] = y[:, :lanes]
        o_ref[pl.ds(2 * m + 1, rows, stride=tc), :] = y[:, lanes:]


def _s5_factors(lam_re, lam_im, log_dt, b_re, b_im, c_re, c_im):
    tc = S5_CHUNK
    g, p = lam_re.shape
    c = b_re.shape[-1]
    lr = lam_re.astype(F32)
    li = lam_im.astype(F32)
    dt = jnp.exp(log_dt.astype(F32))[:, None]
    ar, ai = lr * dt, li * dt
    ks = jnp.arange(tc + 1, dtype=F32)[None, :, None]
    mag = jnp.exp(ks * ar[:, None, :])
    pr = mag * jnp.cos(ks * ai[:, None, :])
    pi = mag * jnp.sin(ks * ai[:, None, :])
    nr = jnp.expm1(ar) * jnp.cos(ai) - 2.0 * jnp.sin(0.5 * ai) ** 2
    ni = jnp.exp(ar) * jnp.sin(ai)
    den = lr * lr + li * li
    fr = (nr * lr + ni * li) / den
    fi = (ni * lr - nr * li) / den
    bbr = fr[..., None] * b_re - fi[..., None] * b_im
    bbi = fr[..., None] * b_im + fi[..., None] * b_re
    er = pr[..., None] * bbr[:, None] - pi[..., None] * bbi[:, None]
    ei = pr[..., None] * bbi[:, None] + pi[..., None] * bbr[:, None]
    kk = (jnp.einsum('gop,gkpi->gkoi', c_re, er[:, :tc], precision=HIGHEST)
          - jnp.einsum('gop,gkpi->gkoi', c_im, ei[:, :tc], precision=HIGHEST))
    gt = LANES // c
    nt = g // gt

    def tile_rows(x):
        a, sub = x.shape[1], x.shape[-1]
        return x.reshape(nt, gt, a, c, sub).transpose(0, 2, 1, 3, 4).reshape(nt, a * gt * c, sub)

    kt = kk.transpose(0, 1, 3, 2)[:, ::-1]
    zblk = jnp.zeros_like(kt[:, :1])
    g2_ab = (tile_rows(jnp.concatenate([kt, zblk], axis=1)), tile_rows(jnp.concatenate([zblk, kt], axis=1)))
    ws_ab = (tile_rows(er[:, tc - 1::-1].transpose(0, 1, 3, 2)), tile_rows(ei[:, tc - 1::-1].transpose(0, 1, 3, 2)))
    pr1, pi1 = pr[:, 1:, None, :], pi[:, 1:, None, :]
    qr = c_re[:, None] * pr1 - c_im[:, None] * pi1
    qi = c_re[:, None] * pi1 + c_im[:, None] * pr1
    wrt_ab = (tile_rows(qr), tile_rows(-qi))
    return g2_ab, ws_ab, wrt_ab, pr[:, tc].reshape(nt, 1, gt * p), pi[:, tc].reshape(nt, 1, gt * p)


def _s5_tables(lam_re, lam_im, log_dt, b_re, b_im, c_re, c_im):
    return jax.vmap(_s5_factors)(lam_re, lam_im, log_dt, b_re, b_im, c_re, c_im)


def _s5_scan(u, tables, layer, *, seq):
    t, w = u.shape
    (g2a, g2b), (wsa, wsb), (wra, wrb), lr, li = tables
    nt, sp = lr.shape[1], lr.shape[-1]
    c = g2a.shape[-1]
    gt = LANES // c
    rows = seq // S5_CHUNK
    tile = lambda k, b: (layer, k, 0, 0)
    factor = lambda x: pl.BlockSpec((None, None) + x.shape[2:], tile)
    table = lambda x: pltpu.VMEM((x.shape[2], 2 * gt * x.shape[3]), BF16)
    return pl.pallas_call(
        functools.partial(_s5_kernel, tc=S5_CHUNK, gt=gt, c=c),
        out_shape=jax.ShapeDtypeStruct((t, w), F32),
        grid=(nt, t // seq),
        in_specs=[pl.BlockSpec((seq, LANES), lambda k, b: (b, k)),
                  factor(g2a), factor(g2b), factor(wsa), factor(wsb), factor(wra), factor(wrb),
                  factor(lr), factor(li)],
        out_specs=pl.BlockSpec((seq, LANES), lambda k, b: (b, k)),
        scratch_shapes=[table(g2a), table(wsa), table(wra)] + [pltpu.VMEM((rows, sp), F32)] * 4,
        compiler_params=_cparams(("parallel", "arbitrary")),
        name="s5_scan",
    )(u, g2a, g2b, wsa, wsb, wra, wrb, lr, li)


def _s5_post_kernel(y_ref, u_ref, d_ref, w_ref, b_ref, o_ref):
    y = y_ref[...] + d_ref[...] * u_ref[...].astype(F32)
    c0 = math.sqrt(2.0 / math.pi)
    y = 0.5 * y * (1.0 + jnp.tanh(c0 * (y + 0.044715 * (y * y * y))))
    gate = jnp.dot(y.astype(BF16), w_ref[...], preferred_element_type=F32) + b_ref[...]
    o_ref[...] = (y * _sigmoid(gate)).astype(o_ref.dtype)


def _s5_post(y, u, d_skip, w_glu, b_glu, layer, *, tm):
    t, w = y.shape
    return pl.pallas_call(
        _s5_post_kernel,
        out_shape=jax.ShapeDtypeStruct((t, w), BF16),
        grid=(t // tm,),
        in_specs=[pl.BlockSpec((tm, w), lambda i: (i, 0)),
                  pl.BlockSpec((tm, w), lambda i: (i, 0)),
                  pl.BlockSpec((1, w), lambda i: (0, 0)),
                  pl.BlockSpec((None, w, w), lambda i: (layer, 0, 0)),
                  pl.BlockSpec((1, w), lambda i: (0, 0))],
        out_specs=pl.BlockSpec((tm, w), lambda i: (i, 0)),
        compiler_params=_cparams(("parallel",)),
        name="s5_post",
    )(y, u, d_skip.reshape(1, w), w_glu, b_glu.reshape(1, w))


def _lru_kernel(x_ref, cw_ref, cb_ref, wr_ref, br_ref, wi_ref, bi_ref, lam_ref, o_ref,
                xbuf_ref, a_ref, g_ref, h_ref, *, tl, kw):
    pad = 8
    w = x_ref.shape[1]

    @pl.when(pl.program_id(1) == 0)
    def _():
        xbuf_ref[pl.ds(0, pad), :] = jnp.zeros((pad, w), F32)
        h_ref[...] = jnp.zeros_like(h_ref)

    @pl.when(pl.program_id(1) > 0)
    def _():
        xbuf_ref[pl.ds(0, pad), :] = xbuf_ref[pl.ds(tl, pad), :]

    xbuf_ref[pl.ds(pad, tl), :] = x_ref[...].astype(F32)
    cw = cw_ref[...]
    xc = cb_ref[...] + xbuf_ref[pl.ds(pad - (kw - 1), tl), :] * cw[0:1, :]
    for k in range(1, kw):
        xc = xc + xbuf_ref[pl.ds(pad - (kw - 1) + k, tl), :] * cw[k:k + 1, :]
    xcb = xc.astype(BF16)
    nsup = w // LRU_SUPER
    rs, is_ = [], []
    for s in range(nsup):
        xs = xcb[:, s * LRU_SUPER:(s + 1) * LRU_SUPER]
        rs.append(jnp.dot(xs, wr_ref[s], preferred_element_type=F32))
        is_.append(jnp.dot(xs, wi_ref[s], preferred_element_type=F32))
    r = _sigmoid(jnp.concatenate(rs, axis=1) + br_ref[...])
    i = _sigmoid(jnp.concatenate(is_, axis=1) + bi_ref[...])
    log_a = (-LRU_C) * r * _softplus(-lam_ref[...])
    a_ref[...] = jnp.exp(log_a)
    th = jnp.tanh(log_a)
    g_ref[...] = jnp.sqrt((-2.0 * th) / (1.0 - th)) * (i * xc)

    row8 = lax.broadcasted_iota(jnp.int32, (8, w), 0)

    def body(t8, h):
        base = pl.multiple_of(t8 * 8, 8)
        a = a_ref[pl.ds(base, 8), :]
        b = g_ref[pl.ds(base, 8), :]
        for sh in (1, 2, 4):
            ok = row8 >= sh
            a_prev = jnp.where(ok, pltpu.roll(a, sh, axis=0), 1.0)
            b_prev = jnp.where(ok, pltpu.roll(b, sh, axis=0), 0.0)
            b = a * b_prev + b
            a = a * a_prev
        hs = a * h + b
        g_ref[pl.ds(base, 8), :] = hs
        return hs[7:8, :]

    h_ref[...] = lax.fori_loop(0, tl // 8, body, h_ref[...])
    o_ref[...] = g_ref[...].astype(o_ref.dtype)


def _block_diag_super(wblk):
    n, k, _ = wblk.shape
    per = LRU_SUPER // k
    wb = wblk.reshape(n // per, per, k, k)
    eye = jnp.eye(per, dtype=wblk.dtype)
    sup = wb[:, :, :, None, :] * eye[None, :, None, :, None]
    return sup.reshape(n // per, LRU_SUPER, LRU_SUPER)


def _lru(proj3, col_block, conv_w, conv_b, w_r, b_r, w_i, b_i, lam, *, tl):
    b, l, _ = proj3.shape
    w = conv_w.shape[1]
    kw = conv_w.shape[0]
    wr = _block_diag_super(w_r).astype(BF16)
    wi = _block_diag_super(w_i).astype(BF16)
    vec = lambda bi, ti: (0, 0)
    full3 = lambda bi, ti: (0, 0, 0)
    return pl.pallas_call(
        functools.partial(_lru_kernel, tl=tl, kw=kw),
        out_shape=jax.ShapeDtypeStruct((b, l, w), BF16),
        grid=(b, l // tl),
        in_specs=[pl.BlockSpec((None, tl, w), lambda bi, ti: (bi, ti, col_block)),
                  pl.BlockSpec((kw, w), vec),
                  pl.BlockSpec((1, w), vec),
                  pl.BlockSpec(wr.shape, full3),
                  pl.BlockSpec((1, w), vec),
                  pl.BlockSpec(wi.shape, full3),
                  pl.BlockSpec((1, w), vec),
                  pl.BlockSpec((1, w), vec)],
        out_specs=pl.BlockSpec((None, tl, w), lambda bi, ti: (bi, ti, 0)),
        scratch_shapes=[pltpu.VMEM((tl + 8, w), F32), pltpu.VMEM((tl, w), F32),
                        pltpu.VMEM((tl, w), F32), pltpu.VMEM((1, w), F32)],
        compiler_params=_cparams(("parallel", "arbitrary")),
        name="rglru",
    )(proj3, conv_w, conv_b.reshape(1, w), wr, b_r.reshape(1, w), wi, b_i.reshape(1, w), lam.reshape(1, w))


def _attn_kernel(q_ref, k_ref, v_ref, o_ref, *, tq, hp):
    qi = pl.program_id(2)
    d = HEAD_DIM
    row = lax.broadcasted_iota(jnp.int32, (tq, tq), 0)
    col = lax.broadcasted_iota(jnp.int32, (tq, tq), 1)
    suffix = jnp.where(row > col, 1.0, 0.0).astype(BF16)
    causal = col < row
    qs = [q_ref[:, h * d:(h + 1) * d] for h in range(hp)]

    def blocks(j_lo, nb, state, masked):
        ks = pl.multiple_of(j_lo * tq, tq)
        heads = range(hp)
        order = range(nb - 1, -1, -1)
        zs = [lax.dot_general(qs[h], k_ref[pl.ds(ks, nb * tq), h * d:(h + 1) * d], (((1,), (1,)), ((), ())),
                              preferred_element_type=F32) for h in heads]
        sps = []
        for h in heads:
            sp = _softplus(zs[h])
            sps.append(jnp.where(causal, sp, 0.0) if masked else sp)
        sss = [[jnp.dot(sps[h][:, n * tq:(n + 1) * tq].astype(BF16), suffix, preferred_element_type=F32)
                for n in range(nb)] for h in heads]
        out = []
        for h in heads:
            carry, acc = state[2 * h], state[2 * h + 1]
            ws = [None] * nb
            for n in order:
                cols = slice(n * tq, (n + 1) * tq)
                w = jnp.exp(zs[h][:, cols] - sps[h][:, cols] - sss[h][n] - carry)
                ws[n] = (jnp.where(causal, w, 0.0) if masked else w).astype(BF16)
                carry = carry + jnp.sum(sps[h][:, cols], axis=-1, keepdims=True)
            acc = acc + jnp.dot(jnp.concatenate(ws, axis=1), v_ref[pl.ds(ks, nb * tq), h * d:(h + 1) * d],
                                preferred_element_type=F32)
            out.extend((carry, acc))
        return tuple(out)

    state = blocks(qi, 1, (jnp.zeros((tq, 1), F32), jnp.zeros((tq, d), F32)) * hp, True)

    def keep_going(c):
        return jnp.logical_and(c[0] < qi, c[1] == 0)

    def sweep(c):
        jj = c[0]
        new = blocks(qi - 1 - jj, 1, c[2:], False)
        low = new[0]
        for h in range(1, hp):
            low = jnp.minimum(low, new[2 * h])
        done = (jnp.min(low) > ATTN_UNDERFLOW).astype(jnp.int32)
        return (jj + 1, done) + new

    state = lax.while_loop(keep_going, sweep, (jnp.int32(0), jnp.int32(0)) + state)[2:]
    for h in range(hp):
        o_ref[:, h * d:(h + 1) * d] = state[2 * h + 1].astype(o_ref.dtype)


def _attention(proj3, q_blk, k_blk, v_blk, n_heads, *, tq, hp):
    b, l, _ = proj3.shape
    d = HEAD_DIM
    return pl.pallas_call(
        functools.partial(_attn_kernel, tq=tq, hp=hp),
        out_shape=jax.ShapeDtypeStruct((b, l, n_heads * d), BF16),
        grid=(b, n_heads // hp, l // tq),
        in_specs=[pl.BlockSpec((None, tq, hp * d), lambda bi, hi, qi: (bi, qi, q_blk // hp + hi)),
                  pl.BlockSpec((None, l, hp * d), lambda bi, hi, qi: (bi, 0, k_blk // hp + hi)),
                  pl.BlockSpec((None, l, hp * d), lambda bi, hi, qi: (bi, 0, v_blk // hp + hi))],
        out_specs=pl.BlockSpec((None, tq, hp * d), lambda bi, hi, qi: (bi, qi, hi)),
        compiler_params=_cparams(("parallel", "parallel", "arbitrary")),
        name="stick_attn",
    )(proj3, proj3, proj3)


def _merge_kernel(ya_ref, yb_ref, yc_ref, wa_ref, wb_ref, wc_ref, ga_ref, gb_ref, gc_ref, bg_ref, o_ref):
    bg = bg_ref[...]
    out = None
    for n, (y_ref, w_ref, gl_ref) in enumerate(((ya_ref, wa_ref, ga_ref), (yb_ref, wb_ref, gb_ref),
                                                (yc_ref, wc_ref, gc_ref))):
        gate = _sigmoid(gl_ref[...].astype(F32) + bg[n:n + 1, :])
        term = gate * jnp.dot(y_ref[...], w_ref[...], preferred_element_type=F32)
        out = term if out is None else out + term
    o_ref[...] = out.astype(o_ref.dtype)


def _merge(ys, ws, proj, gate_col, b_gate, layer, *, tm, tn):
    t, w = ys[0].shape
    d = ws[0].shape[-1]
    gblk = gate_col // tn
    per = d // tn
    y_spec = pl.BlockSpec((tm, w), lambda i, j: (i, 0))
    w_spec = pl.BlockSpec((None, w, tn), lambda i, j: (layer, 0, j))
    gate_specs = [pl.BlockSpec((tm, tn), functools.partial(lambda i, j, n: (i, gblk + n * per + j), n=n))
                  for n in range(N_BRANCH)]
    return pl.pallas_call(
        _merge_kernel,
        out_shape=jax.ShapeDtypeStruct((t, d), BF16),
        grid=(t // tm, d // tn),
        in_specs=[y_spec] * 3 + [w_spec] * 3 + gate_specs
                 + [pl.BlockSpec((N_BRANCH, tn), lambda i, j: (0, j))],
        out_specs=pl.BlockSpec((tm, tn), lambda i, j: (i, j)),
        compiler_params=_cparams(("parallel", "arbitrary")),
        name="merge",
    )(*ys, *ws, proj, proj, proj, b_gate.reshape(N_BRANCH, d))


def _outproj_kernel(m_ref, w_ref, h_ref, g_ref, ho_ref, hn_ref):
    h = h_ref[...] + jnp.dot(m_ref[...], w_ref[...], preferred_element_type=F32)
    ho_ref[...] = h
    hn_ref[...] = _rms_rows(h, g_ref[...]).astype(hn_ref.dtype)


def _outproj(merged, w_out, h, g, layer, *, tm):
    t, d = h.shape
    row = lambda i: (i, 0)
    return pl.pallas_call(
        _outproj_kernel,
        out_shape=(jax.ShapeDtypeStruct((t, d), F32), jax.ShapeDtypeStruct((t, d), BF16)),
        grid=(t // tm,),
        in_specs=[pl.BlockSpec((tm, d), row),
                  pl.BlockSpec((None, d, d), lambda i: (layer, 0, 0)),
                  pl.BlockSpec((tm, d), row),
                  pl.BlockSpec((1, d), lambda i: (0, 0))],
        out_specs=(pl.BlockSpec((tm, d), row), pl.BlockSpec((tm, d), row)),
        compiler_params=_cparams(("parallel",)),
        name="outproj",
    )(merged, w_out, h, g.reshape(1, d))


def _mlp_kernel(x_ref, wu_ref, wd_ref, h_ref, g_ref, o_ref, *, final_norm, parts):
    f = pl.program_id(1)

    @pl.when(f == 0)
    def _():
        o_ref[...] = h_ref[...]

    x = x_ref[...]
    part = wu_ref.shape[1] // parts
    hids = [jnp.maximum(jnp.dot(x, wu_ref[:, n * part:(n + 1) * part], preferred_element_type=F32), 0.0)
            for n in range(parts)]
    hid = jnp.concatenate([(hd * hd).astype(BF16) for hd in hids], axis=1)
    o_ref[...] += jnp.dot(hid, wd_ref[...], preferred_element_type=F32)

    if final_norm:
        @pl.when(f == pl.num_programs(1) - 1)
        def _():
            o_ref[...] = _rms_rows(o_ref[...], g_ref[...])


def _mlp(hn, w_up, w_down, h, g, layer, *, tm, tf, final_norm):
    t, d = h.shape
    ff = w_up.shape[-1]
    row = lambda i, f: (i, 0)
    return pl.pallas_call(
        functools.partial(_mlp_kernel, final_norm=final_norm, parts=2),
        out_shape=jax.ShapeDtypeStruct((t, d), F32),
        grid=(t // tm, ff // tf),
        in_specs=[pl.BlockSpec((tm, d), row),
                  pl.BlockSpec((None, d, tf), lambda i, f: (layer, 0, f)),
                  pl.BlockSpec((None, tf, d), lambda i, f: (layer, f, 0)),
                  pl.BlockSpec((tm, d), row),
                  pl.BlockSpec((1, d), lambda i, f: (0, 0))],
        out_specs=pl.BlockSpec((tm, d), row),
        compiler_params=_cparams(("parallel", "arbitrary")),
        name="mlp",
    )(hn, w_up, w_down, h, g.reshape(1, d))


def _layer(h, bsz, seq, p, big, layer, *, final_g):
    t, d = h.shape
    s5_w = p["s5_d"].shape[0]
    lru_w = p["lru_lambda"].shape[0]
    attn_w = big["w_br_attn"].shape[1]
    n_heads = attn_w // HEAD_DIM
    off_lru = s5_w
    off_q = off_lru + lru_w
    off_k = off_q + attn_w
    off_v = off_k + attn_w
    off_gate = off_v + attn_w

    proj, u_s5 = _inproj(h, p["norm_mix_g"], big["w_in"], layer, tm=1024, tn=s5_w)
    proj3 = proj.reshape(bsz, seq, proj.shape[1])

    y_ssm = _s5_scan(u_s5, big["s5_tables"], layer, seq=seq)
    y_s5 = _s5_post(y_ssm, u_s5, p["s5_d"], big["s5_w_glu"], p["s5_b_glu"], layer, tm=1024)

    y_lru = _lru(proj3, off_lru // lru_w, p["lru_conv_w"], p["lru_conv_b"], p["lru_w_r"], p["lru_b_r"],
                 p["lru_w_i"], p["lru_b_i"], p["lru_lambda"], tl=512).reshape(t, lru_w)

    y_attn = _attention(proj3, off_q // HEAD_DIM, off_k // HEAD_DIM, off_v // HEAD_DIM, n_heads,
                        tq=256, hp=8).reshape(t, attn_w)

    merged = _merge((y_s5, y_lru, y_attn), (big["w_br_s5"], big["w_br_lru"], big["w_br_attn"]),
                    proj, off_gate, p["b_gate"], layer, tm=1024, tn=512)
    h, hn = _outproj(merged, big["w_out"], h, p["norm_mlp_g"], layer, tm=512)
    g_last = p["norm_mlp_g"] if final_g is None else final_g
    return _mlp(hn, big["w_up"], big["w_down"], h, g_last, layer,
                tm=512, tf=1024, final_norm=final_g is not None)


_SMALL_PARAMS = ("norm_mix_g", "b_gate", "s5_d", "s5_b_glu", "lru_conv_w", "lru_conv_b", "lru_w_r", "lru_b_r",
                 "lru_w_i", "lru_b_i", "lru_lambda", "norm_mlp_g")


def kernel(x, norm_mix_g, w_in, b_gate, s5_lam_re, s5_lam_im, s5_log_dt, s5_b_re, s5_b_im, s5_c_re, s5_c_im,
           s5_d, s5_w_glu, s5_b_glu, lru_conv_w, lru_conv_b, lru_w_r, lru_b_r, lru_w_i, lru_b_i, lru_lambda,
           w_br_s5, w_br_lru, w_br_attn, w_out, norm_mlp_g, w_up, w_down, final_norm_g):
    small = dict(zip(_SMALL_PARAMS, (norm_mix_g, b_gate, s5_d, s5_b_glu, lru_conv_w, lru_conv_b, lru_w_r, lru_b_r,
                                     lru_w_i, lru_b_i, lru_lambda, norm_mlp_g)))
    bsz, seq, d = x.shape
    depth = w_in.shape[0]
    off_q = s5_d.shape[1] + lru_lambda.shape[1]
    col = jnp.arange(w_in.shape[-1])
    q_scale = jnp.where((col >= off_q) & (col < off_q + w_br_attn.shape[1]), HEAD_DIM ** -0.5, 1.0).astype(F32)
    big = {"w_in": (w_in * q_scale).astype(BF16), "s5_w_glu": s5_w_glu.astype(BF16),
           "w_br_s5": w_br_s5.astype(BF16), "w_br_lru": w_br_lru.astype(BF16), "w_br_attn": w_br_attn.astype(BF16),
           "w_out": w_out.astype(BF16), "w_up": w_up.astype(BF16), "w_down": w_down.astype(BF16),
           "s5_tables": _s5_tables(s5_lam_re, s5_lam_im, s5_log_dt, s5_b_re, s5_b_im, s5_c_re, s5_c_im)}
    h = x.reshape(bsz * seq, d).astype(F32)
    for layer in range(depth):
        p = {k: v[layer] for k, v in small.items()}
        h = _layer(h, bsz, seq, p, big, layer, final_g=final_norm_g if layer == depth - 1 else None)
    return h.reshape(bsz, seq, d).astype(x.dtype)
```

```python
import functools
import math

import jax
import jax.numpy as jnp
from jax import lax
from jax.experimental import pallas as pl
from jax.experimental.pallas import tpu as pltpu

F32 = jnp.float32
BF16 = jnp.bfloat16

EPS = 1e-6
LANES = 128
HEAD_DIM = 128
ATTN_UNDERFLOW = 128.0
S5_GROUP = 16
S5_CHUNK = 16
LRU_BLOCK = 64
LRU_SUPER = 256
LRU_C = 8.0
N_BRANCH = 3
LOG2E = 1.4426950408889634
VMEM_LIMIT = 56 * 1024 * 1024
HIGHEST = lax.Precision.HIGHEST


def _cparams(sem):
    return pltpu.CompilerParams(dimension_semantics=sem, vmem_limit_bytes=VMEM_LIMIT)


def _softplus(z):
    return jnp.maximum(z, 0.0) + jnp.log(1.0 + jnp.exp2(jnp.abs(z) * (-LOG2E)))


def _sigmoid(z):
    return 1.0 / (1.0 + jnp.exp(-z))


def _rms_rows(x, g):
    ms = jnp.mean(x * x, axis=-1, keepdims=True)
    return (x * lax.rsqrt(ms + EPS)) * g


def _inproj_kernel(x_ref, g_ref, w_ref, o_ref, o32_ref, xn_ref, *, row_chunk):
    j = pl.program_id(1)

    @pl.when(j == 0)
    def _():
        g = g_ref[...]
        w = w_ref[...]
        for r in range(0, x_ref.shape[0], row_chunk):
            xn = _rms_rows(x_ref[r:r + row_chunk, :], g).astype(BF16)
            xn_ref[r:r + row_chunk, :] = xn
            acc = jnp.dot(xn, w, preferred_element_type=F32)
            o_ref[r:r + row_chunk, :] = acc.astype(o_ref.dtype)
            o32_ref[r:r + row_chunk, :] = acc

    @pl.when(j > 0)
    def _():
        o_ref[...] = jnp.dot(xn_ref[...], w_ref[...], preferred_element_type=F32).astype(o_ref.dtype)


def _inproj(h, g, w, layer, *, tm, tn):
    t, d = h.shape
    n = w.shape[-1]
    return pl.pallas_call(
        functools.partial(_inproj_kernel, row_chunk=256),
        out_shape=(jax.ShapeDtypeStruct((t, n), BF16), jax.ShapeDtypeStruct((t, tn), F32)),
        grid=(t // tm, n // tn),
        in_specs=[pl.BlockSpec((tm, d), lambda i, j: (i, 0)),
                  pl.BlockSpec((1, d), lambda i, j: (0, 0)),
                  pl.BlockSpec((None, d, tn), lambda i, j: (layer, 0, j))],
        out_specs=(pl.BlockSpec((tm, tn), lambda i, j: (i, j)),
                   pl.BlockSpec((tm, tn), lambda i, j: (i, 0))),
        scratch_shapes=[pltpu.VMEM((tm, d), BF16)],
        compiler_params=_cparams(("parallel", "arbitrary")),
        name="inproj",
    )(h, g.reshape(1, d), w)


def _lane_diag(a_ref, b_ref, o_ref, *, gt, c):
    rows, sub = a_ref.shape
    width = gt * sub
    src = lax.broadcasted_iota(jnp.int32, (sub, width), 0)
    dst = lax.broadcasted_iota(jnp.int32, (sub, width), 1)
    spread = jnp.where(dst % sub == src, 1.0, 0.0).astype(BF16)
    row_g = (lax.broadcasted_iota(jnp.int32, (rows, width), 0) // c) % gt
    col_g = lax.broadcasted_iota(jnp.int32, (rows, width), 1) // sub
    keep = row_g == col_g
    for n, x_ref in enumerate((a_ref, b_ref)):
        tiled = jnp.dot(x_ref[...].astype(BF16), spread, preferred_element_type=F32)
        o_ref[:, n * width:(n + 1) * width] = jnp.where(keep, tiled, 0.0).astype(o_ref.dtype)


def _s5_kernel(u_ref, g2a_ref, g2b_ref, wsa_ref, wsb_ref, wra_ref, wrb_ref, lr_ref, li_ref, o_ref,
               g2_ref, ws_ref, wrt_ref, xre_ref, xim_ref, hre_ref, him_ref, *, tc, gt, c):
    @pl.when(pl.program_id(1) == 0)
    def _():
        _lane_diag(g2a_ref, g2b_ref, g2_ref, gt=gt, c=c)
        _lane_diag(wsa_ref, wsb_ref, ws_ref, gt=gt, c=c)
        _lane_diag(wra_ref, wrb_ref, wrt_ref, gt=gt, c=c)

    rows = u_ref.shape[0] // tc
    lanes = u_ref.shape[1]
    xc = jnp.concatenate([u_ref[pl.ds(t, rows, stride=tc), :].astype(BF16) for t in range(tc)], axis=1)
    lr = lr_ref[...]
    li = li_ref[...]
    sp = lr.shape[1]
    x = jnp.dot(xc, ws_ref[...], preferred_element_type=F32)
    xre_ref[...] = x[:, :sp]
    xim_ref[...] = x[:, sp:]

    def body(k, carry):
        hr, hi = carry
        base = pl.multiple_of(k * 8, 8)
        for r in range(8):
            hre_ref[pl.ds(base + r, 1), :] = hr
            him_ref[pl.ds(base + r, 1), :] = hi
            xr = xre_ref[pl.ds(base + r, 1), :]
            xi = xim_ref[pl.ds(base + r, 1), :]
            hr, hi = lr * hr - li * hi + xr, lr * hi + li * hr + xi
        return hr, hi

    zero = jnp.zeros(lr.shape, F32)
    lax.fori_loop(0, rows // 8, body, (zero, zero))

    h0 = jnp.concatenate([hre_ref[...].astype(BF16), him_ref[...].astype(BF16)], axis=1)
    yr = lax.dot_general(h0, wrt_ref[...], (((1,), (1,)), ((), ())), preferred_element_type=F32)
    nblk = g2_ref.shape[0] // lanes
    for m in range(tc // 2):
        kdim = (2 * m + 2) * lanes
        y = jnp.dot(xc[:, :kdim], g2_ref[(nblk * lanes - kdim):, :], preferred_element_type=F32)
        y = y + yr[:, 2 * m * lanes:(2 * m + 2) * lanes]
        o_ref[pl.ds(2 * m, rows, stride=tc), :] = y[:, :lanes]
        o_ref[pl.ds(2 * m + 1, rows, stride=tc), :] = y[:, lanes:]


def _s5_factors(lam_re, lam_im, log_dt, b_re, b_im, c_re, c_im):
    tc = S5_CHUNK
    g, p = lam_re.shape
    c = b_re.shape[-1]
    lr = lam_re.astype(F32)
    li = lam_im.astype(F32)
    dt = jnp.exp(log_dt.astype(F32))[:, None]
    ar, ai = lr * dt, li * dt

    def powers(ks):
        k3 = ks.astype(F32)[None, :, None]
        mag = jnp.exp(k3 * ar[:, None, :])
        return mag * jnp.cos(k3 * ai[:, None, :]), mag * jnp.sin(k3 * ai[:, None, :])

    pdr, pdi = powers((tc - 1) - jnp.arange(tc))
    par, pai = powers(1 + jnp.arange(tc))
    nr = jnp.expm1(ar) * jnp.cos(ai) - 2.0 * jnp.sin(0.5 * ai) ** 2
    ni = jnp.exp(ar) * jnp.sin(ai)
    den = lr * lr + li * li
    fr = ((nr * lr + ni * li) / den)[:, None, :]
    fi = ((ni * lr - nr * li) / den)[:, None, :]
    bt_re, bt_im = b_re.transpose(0, 2, 1), b_im.transpose(0, 2, 1)
    bbr = fr * bt_re - fi * bt_im
    bbi = fr * bt_im + fi * bt_re
    er = pdr[:, :, None, :] * bbr[:, None] - pdi[:, :, None, :] * bbi[:, None]
    ei = pdr[:, :, None, :] * bbi[:, None] + pdi[:, :, None, :] * bbr[:, None]
    kt = (jnp.einsum('grip,gop->grio', er, c_re, precision=HIGHEST)
          - jnp.einsum('grip,gop->grio', ei, c_im, precision=HIGHEST))
    gt = LANES // c
    nt = g // gt

    def tile_rows(x):
        a, sub = x.shape[1], x.shape[-1]
        return x.reshape(nt, gt, a, c, sub).transpose(0, 2, 1, 3, 4).reshape(nt, a * gt * c, sub)

    zblk = jnp.zeros_like(kt[:, :1])
    g2_ab = (tile_rows(jnp.concatenate([kt, zblk], axis=1)), tile_rows(jnp.concatenate([zblk, kt], axis=1)))
    ws_ab = (tile_rows(er), tile_rows(ei))
    qr = c_re[:, None] * par[:, :, None, :] - c_im[:, None] * pai[:, :, None, :]
    qi = c_re[:, None] * pai[:, :, None, :] + c_im[:, None] * par[:, :, None, :]
    wrt_ab = (tile_rows(qr), tile_rows(-qi))
    return g2_ab, ws_ab, wrt_ab, par[:, tc - 1].reshape(nt, 1, gt * p), pai[:, tc - 1].reshape(nt, 1, gt * p)


def _s5_tables(lam_re, lam_im, log_dt, b_re, b_im, c_re, c_im):
    return jax.vmap(_s5_factors)(lam_re, lam_im, log_dt, b_re, b_im, c_re, c_im)


def _s5_scan(u, tables, layer, *, seq):
    t, w = u.shape
    (g2a, g2b), (wsa, wsb), (wra, wrb), lr, li = tables
    nt, sp = lr.shape[1], lr.shape[-1]
    c = g2a.shape[-1]
    gt = LANES // c
    rows = seq // S5_CHUNK
    tile = lambda k, b: (layer, k, 0, 0)
    factor = lambda x: pl.BlockSpec((None, None) + x.shape[2:], tile)
    table = lambda x: pltpu.VMEM((x.shape[2], 2 * gt * x.shape[3]), BF16)
    return pl.pallas_call(
        functools.partial(_s5_kernel, tc=S5_CHUNK, gt=gt, c=c),
        out_shape=jax.ShapeDtypeStruct((t, w), F32),
        grid=(nt, t // seq),
        in_specs=[pl.BlockSpec((seq, LANES), lambda k, b: (b, k)),
                  factor(g2a), factor(g2b), factor(wsa), factor(wsb), factor(wra), factor(wrb),
                  factor(lr), factor(li)],
        out_specs=pl.BlockSpec((seq, LANES), lambda k, b: (b, k)),
        scratch_shapes=[table(g2a), table(wsa), table(wra)] + [pltpu.VMEM((rows, sp), F32)] * 4,
        compiler_params=_cparams(("parallel", "arbitrary")),
        name="s5_scan",
    )(u, g2a, g2b, wsa, wsb, wra, wrb, lr, li)


def _s5_post_kernel(y_ref, u_ref, d_ref, w_ref, b_ref, o_ref):
    y = y_ref[...] + d_ref[...] * u_ref[...].astype(F32)
    c0 = math.sqrt(2.0 / math.pi)
    y = 0.5 * y * (1.0 + jnp.tanh(c0 * (y + 0.044715 * (y * y * y))))
    gate = jnp.dot(y.astype(BF16), w_ref[...], preferred_element_type=F32) + b_ref[...]
    o_ref[...] = (y * _sigmoid(gate)).astype(o_ref.dtype)


def _s5_post(y, u, d_skip, w_glu, b_glu, layer, *, tm):
    t, w = y.shape
    return pl.pallas_call(
        _s5_post_kernel,
        out_shape=jax.ShapeDtypeStruct((t, w), BF16),
        grid=(t // tm,),
        in_specs=[pl.BlockSpec((tm, w), lambda i: (i, 0)),
                  pl.BlockSpec((tm, w), lambda i: (i, 0)),
                  pl.BlockSpec((1, w), lambda i: (0, 0)),
                  pl.BlockSpec((None, w, w), lambda i: (layer, 0, 0)),
                  pl.BlockSpec((1, w), lambda i: (0, 0))],
        out_specs=pl.BlockSpec((tm, w), lambda i: (i, 0)),
        compiler_params=_cparams(("parallel",)),
        name="s5_post",
    )(y, u, d_skip.reshape(1, w), w_glu, b_glu.reshape(1, w))


def _lru_kernel(x_ref, cw_ref, cb_ref, wr_ref, br_ref, wi_ref, bi_ref, lam_ref, o_ref,
                xbuf_ref, a_ref, g_ref, h_ref, *, tl, kw):
    pad = 8
    w = x_ref.shape[1]

    @pl.when(pl.program_id(1) == 0)
    def _():
        xbuf_ref[pl.ds(0, pad), :] = jnp.zeros((pad, w), F32)
        h_ref[...] = jnp.zeros_like(h_ref)

    @pl.when(pl.program_id(1) > 0)
    def _():
        xbuf_ref[pl.ds(0, pad), :] = xbuf_ref[pl.ds(tl, pad), :]

    xbuf_ref[pl.ds(pad, tl), :] = x_ref[...].astype(F32)
    cw = cw_ref[...]
    xc = cb_ref[...] + xbuf_ref[pl.ds(pad - (kw - 1), tl), :] * cw[0:1, :]
    for k in range(1, kw):
        xc = xc + xbuf_ref[pl.ds(pad - (kw - 1) + k, tl), :] * cw[k:k + 1, :]
    xcb = xc.astype(BF16)
    nsup = w // LRU_SUPER
    rs, is_ = [], []
    for s in range(nsup):
        xs = xcb[:, s * LRU_SUPER:(s + 1) * LRU_SUPER]
        rs.append(jnp.dot(xs, wr_ref[s], preferred_element_type=F32))
        is_.append(jnp.dot(xs, wi_ref[s], preferred_element_type=F32))
    r = _sigmoid(jnp.concatenate(rs, axis=1) + br_ref[...])
    i = _sigmoid(jnp.concatenate(is_, axis=1) + bi_ref[...])
    log_a = (-LRU_C) * r * _softplus(-lam_ref[...])
    a_ref[...] = jnp.exp(log_a)
    th = jnp.tanh(log_a)
    g_ref[...] = jnp.sqrt((-2.0 * th) / (1.0 - th)) * (i * xc)

    row8 = lax.broadcasted_iota(jnp.int32, (8, w), 0)

    def body(t8, h):
        base = pl.multiple_of(t8 * 8, 8)
        a = a_ref[pl.ds(base, 8), :]
        b = g_ref[pl.ds(base, 8), :]
        for sh in (1, 2, 4):
            ok = row8 >= sh
            a_prev = jnp.where(ok, pltpu.roll(a, sh, axis=0), 1.0)
            b_prev = jnp.where(ok, pltpu.roll(b, sh, axis=0), 0.0)
            b = a * b_prev + b
            a = a * a_prev
        hs = a * h + b
        g_ref[pl.ds(base, 8), :] = hs
        return hs[7:8, :]

    h_ref[...] = lax.fori_loop(0, tl // 8, body, h_ref[...])
    o_ref[...] = g_ref[...].astype(o_ref.dtype)


def _block_diag_super(wblk):
    n, k, _ = wblk.shape
    per = LRU_SUPER // k
    wb = wblk.reshape(n // per, per, k, k)
    eye = jnp.eye(per, dtype=wblk.dtype)
    sup = wb[:, :, :, None, :] * eye[None, :, None, :, None]
    return sup.reshape(n // per, LRU_SUPER, LRU_SUPER)


def _lru(proj3, col_block, conv_w, conv_b, w_r, b_r, w_i, b_i, lam, *, tl):
    b, l, _ = proj3.shape
    w = conv_w.shape[1]
    kw = conv_w.shape[0]
    wr = _block_diag_super(w_r).astype(BF16)
    wi = _block_diag_super(w_i).astype(BF16)
    vec = lambda bi, ti: (0, 0)
    full3 = lambda bi, ti: (0, 0, 0)
    return pl.pallas_call(
        functools.partial(_lru_kernel, tl=tl, kw=kw),
        out_shape=jax.ShapeDtypeStruct((b, l, w), BF16),
        grid=(b, l // tl),
        in_specs=[pl.BlockSpec((None, tl, w), lambda bi, ti: (bi, ti, col_block)),
                  pl.BlockSpec((kw, w), vec),
                  pl.BlockSpec((1, w), vec),
                  pl.BlockSpec(wr.shape, full3),
                  pl.BlockSpec((1, w), vec),
                  pl.BlockSpec(wi.shape, full3),
                  pl.BlockSpec((1, w), vec),
                  pl.BlockSpec((1, w), vec)],
        out_specs=pl.BlockSpec((None, tl, w), lambda bi, ti: (bi, ti, 0)),
        scratch_shapes=[pltpu.VMEM((tl + 8, w), F32), pltpu.VMEM((tl, w), F32),
                        pltpu.VMEM((tl, w), F32), pltpu.VMEM((1, w), F32)],
        compiler_params=_cparams(("parallel", "arbitrary")),
        name="rglru",
    )(proj3, conv_w, conv_b.reshape(1, w), wr, b_r.reshape(1, w), wi, b_i.reshape(1, w), lam.reshape(1, w))


def _attn_kernel(q_ref, k_ref, v_ref, o_ref, *, tq, hp):
    qi = pl.program_id(2)
    d = HEAD_DIM
    row = lax.broadcasted_iota(jnp.int32, (tq, tq), 0)
    col = lax.broadcasted_iota(jnp.int32, (tq, tq), 1)
    suffix = jnp.where(row > col, 1.0, 0.0).astype(BF16)
    causal = col < row
    qs = [q_ref[:, h * d:(h + 1) * d] for h in range(hp)]

    def blocks(j_lo, nb, state, masked):
        ks = pl.multiple_of(j_lo * tq, tq)
        heads = range(hp)
        order = range(nb - 1, -1, -1)
        zs = [lax.dot_general(qs[h], k_ref[pl.ds(ks, nb * tq), h * d:(h + 1) * d], (((1,), (1,)), ((), ())),
                              preferred_element_type=F32) for h in heads]
        sps = []
        for h in heads:
            sp = _softplus(zs[h])
            sps.append(jnp.where(causal, sp, 0.0) if masked else sp)
        sss = [[jnp.dot(sps[h][:, n * tq:(n + 1) * tq].astype(BF16), suffix, preferred_element_type=F32)
                for n in range(nb)] for h in heads]
        out = []
        for h in heads:
            carry, acc = state[2 * h], state[2 * h + 1]
            ws = [None] * nb
            for n in order:
                cols = slice(n * tq, (n + 1) * tq)
                w = jnp.exp(zs[h][:, cols] - sps[h][:, cols] - sss[h][n] - carry)
                ws[n] = (jnp.where(causal, w, 0.0) if masked else w).astype(BF16)
                carry = carry + jnp.sum(sps[h][:, cols], axis=-1, keepdims=True)
            acc = acc + jnp.dot(jnp.concatenate(ws, axis=1), v_ref[pl.ds(ks, nb * tq), h * d:(h + 1) * d],
                                preferred_element_type=F32)
            out.extend((carry, acc))
        return tuple(out)

    state = blocks(qi, 1, (jnp.zeros((tq, 1), F32), jnp.zeros((tq, d), F32)) * hp, True)

    def keep_going(c):
        return jnp.logical_and(c[0] < qi, c[1] == 0)

    def sweep(c):
        jj = c[0]
        new = blocks(qi - 1 - jj, 1, c[2:], False)
        low = new[0]
        for h in range(1, hp):
            low = jnp.minimum(low, new[2 * h])
        done = (jnp.min(low) > ATTN_UNDERFLOW).astype(jnp.int32)
        return (jj + 1, done) + new

    state = lax.while_loop(keep_going, sweep, (jnp.int32(0), jnp.int32(0)) + state)[2:]
    for h in range(hp):
        o_ref[:, h * d:(h + 1) * d] = state[2 * h + 1].astype(o_ref.dtype)


def _attention(proj3, q_blk, k_blk, v_blk, n_heads, *, tq, hp):
    b, l, _ = proj3.shape
    d = HEAD_DIM
    return pl.pallas_call(
        functools.partial(_attn_kernel, tq=tq, hp=hp),
        out_shape=jax.ShapeDtypeStruct((b, l, n_heads * d), BF16),
        grid=(b, n_heads // hp, l // tq),
        in_specs=[pl.BlockSpec((None, tq, hp * d), lambda bi, hi, qi: (bi, qi, q_blk // hp + hi)),
                  pl.BlockSpec((None, l, hp * d), lambda bi, hi, qi: (bi, 0, k_blk // hp + hi)),
                  pl.BlockSpec((None, l, hp * d), lambda bi, hi, qi: (bi, 0, v_blk // hp + hi))],
        out_specs=pl.BlockSpec((None, tq, hp * d), lambda bi, hi, qi: (bi, qi, hi)),
        compiler_params=_cparams(("parallel", "parallel", "arbitrary")),
        name="stick_attn",
    )(proj3, proj3, proj3)


def _merge_kernel(ya_ref, yb_ref, yc_ref, wa_ref, wb_ref, wc_ref, ga_ref, gb_ref, gc_ref, bg_ref, o_ref):
    bg = bg_ref[...]
    out = None
    for n, (y_ref, w_ref, gl_ref) in enumerate(((ya_ref, wa_ref, ga_ref), (yb_ref, wb_ref, gb_ref),
                                                (yc_ref, wc_ref, gc_ref))):
        gate = _sigmoid(gl_ref[...].astype(F32) + bg[n:n + 1, :])
        term = gate * jnp.dot(y_ref[...], w_ref[...], preferred_element_type=F32)
        out = term if out is None else out + term
    o_ref[...] = out.astype(o_ref.dtype)


def _merge(ys, ws, proj, gate_col, b_gate, layer, *, tm, tn):
    t, w = ys[0].shape
    d = ws[0].shape[-1]
    gblk = gate_col // tn
    per = d // tn
    y_spec = pl.BlockSpec((tm, w), lambda i, j: (i, 0))
    w_spec = pl.BlockSpec((None, w, tn), lambda i, j: (layer, 0, j))
    gate_specs = [pl.BlockSpec((tm, tn), functools.partial(lambda i, j, n: (i, gblk + n * per + j), n=n))
                  for n in range(N_BRANCH)]
    return pl.pallas_call(
        _merge_kernel,
        out_shape=jax.ShapeDtypeStruct((t, d), BF16),
        grid=(t // tm, d // tn),
        in_specs=[y_spec] * 3 + [w_spec] * 3 + gate_specs
                 + [pl.BlockSpec((N_BRANCH, tn), lambda i, j: (0, j))],
        out_specs=pl.BlockSpec((tm, tn), lambda i, j: (i, j)),
        compiler_params=_cparams(("parallel", "arbitrary")),
        name="merge",
    )(*ys, *ws, proj, proj, proj, b_gate.reshape(N_BRANCH, d))


def _outproj_kernel(m_ref, w_ref, h_ref, g_ref, ho_ref, hn_ref):
    h = h_ref[...] + jnp.dot(m_ref[...], w_ref[...], preferred_element_type=F32)
    ho_ref[...] = h
    hn_ref[...] = _rms_rows(h, g_ref[...]).astype(hn_ref.dtype)


def _outproj(merged, w_out, h, g, layer, *, tm):
    t, d = h.shape
    row = lambda i: (i, 0)
    return pl.pallas_call(
        _outproj_kernel,
        out_shape=(jax.ShapeDtypeStruct((t, d), F32), jax.ShapeDtypeStruct((t, d), BF16)),
        grid=(t // tm,),
        in_specs=[pl.BlockSpec((tm, d), row),
                  pl.BlockSpec((None, d, d), lambda i: (layer, 0, 0)),
                  pl.BlockSpec((tm, d), row),
                  pl.BlockSpec((1, d), lambda i: (0, 0))],
        out_specs=(pl.BlockSpec((tm, d), row), pl.BlockSpec((tm, d), row)),
        compiler_params=_cparams(("parallel",)),
        name="outproj",
    )(merged, w_out, h, g.reshape(1, d))


def _mlp_kernel(x_ref, wu_ref, wd_ref, h_ref, g_ref, o_ref, *, final_norm, parts):
    f = pl.program_id(1)

    @pl.when(f == 0)
    def _():
        o_ref[...] = h_ref[...]

    x = x_ref[...]
    part = wu_ref.shape[1] // parts
    hids = [jnp.maximum(jnp.dot(x, wu_ref[:, n * part:(n + 1) * part], preferred_element_type=F32), 0.0)
            for n in range(parts)]
    hid = jnp.concatenate([(hd * hd).astype(BF16) for hd in hids], axis=1)
    o_ref[...] += jnp.dot(hid, wd_ref[...], preferred_element_type=F32)

    if final_norm:
        @pl.when(f == pl.num_programs(1) - 1)
        def _():
            o_ref[...] = _rms_rows(o_ref[...], g_ref[...])


def _mlp(hn, w_up, w_down, h, g, layer, *, tm, tf, final_norm):
    t, d = h.shape
    ff = w_up.shape[-1]
    row = lambda i, f: (i, 0)
    return pl.pallas_call(
        functools.partial(_mlp_kernel, final_norm=final_norm, parts=2),
        out_shape=jax.ShapeDtypeStruct((t, d), F32),
        grid=(t // tm, ff // tf),
        in_specs=[pl.BlockSpec((tm, d), row),
                  pl.BlockSpec((None, d, tf), lambda i, f: (layer, 0, f)),
                  pl.BlockSpec((None, tf, d), lambda i, f: (layer, f, 0)),
                  pl.BlockSpec((tm, d), row),
                  pl.BlockSpec((1, d), lambda i, f: (0, 0))],
        out_specs=pl.BlockSpec((tm, d), row),
        compiler_params=_cparams(("parallel", "arbitrary")),
        name="mlp",
    )(hn, w_up, w_down, h, g.reshape(1, d))


def _layer(h, bsz, seq, p, big, layer, *, final_g):
    t, d = h.shape
    s5_w = p["s5_d"].shape[0]
    lru_w = p["lru_lambda"].shape[0]
    attn_w = big["w_br_attn"].shape[1]
    n_heads = attn_w // HEAD_DIM
    off_lru = s5_w
    off_q = off_lru + lru_w
    off_k = off_q + attn_w
    off_v = off_k + attn_w
    off_gate = off_v + attn_w

    proj, u_s5 = _inproj(h, p["norm_mix_g"], big["w_in"], layer, tm=1024, tn=s5_w)
    proj3 = proj.reshape(bsz, seq, proj.shape[1])

    y_ssm = _s5_scan(u_s5, big["s5_tables"], layer, seq=seq)
    y_s5 = _s5_post(y_ssm, u_s5, p["s5_d"], big["s5_w_glu"], p["s5_b_glu"], layer, tm=1024)

    y_lru = _lru(proj3, off_lru // lru_w, p["lru_conv_w"], p["lru_conv_b"], p["lru_w_r"], p["lru_b_r"],
                 p["lru_w_i"], p["lru_b_i"], p["lru_lambda"], tl=512).reshape(t, lru_w)

    y_attn = _attention(proj3, off_q // HEAD_DIM, off_k // HEAD_DIM, off_v // HEAD_DIM, n_heads,
                        tq=256, hp=8).reshape(t, attn_w)

    merged = _merge((y_s5, y_lru, y_attn), (big["w_br_s5"], big["w_br_lru"], big["w_br_attn"]),
                    proj, off_gate, p["b_gate"], layer, tm=1024, tn=512)
    h, hn = _outproj(merged, big["w_out"], h, p["norm_mlp_g"], layer, tm=512)
    g_last = p["norm_mlp_g"] if final_g is None else final_g
    return _mlp(hn, big["w_up"], big["w_down"], h, g_last, layer,
                tm=512, tf=1024, final_norm=final_g is not None)


_SMALL_PARAMS = ("norm_mix_g", "b_gate", "s5_d", "s5_b_glu", "lru_conv_w", "lru_conv_b", "lru_w_r", "lru_b_r",
                 "lru_w_i", "lru_b_i", "lru_lambda", "norm_mlp_g")


def kernel(x, norm_mix_g, w_in, b_gate, s5_lam_re, s5_lam_im, s5_log_dt, s5_b_re, s5_b_im, s5_c_re, s5_c_im,
           s5_d, s5_w_glu, s5_b_glu, lru_conv_w, lru_conv_b, lru_w_r, lru_b_r, lru_w_i, lru_b_i, lru_lambda,
           w_br_s5, w_br_lru, w_br_attn, w_out, norm_mlp_g, w_up, w_down, final_norm_g):
    small = dict(zip(_SMALL_PARAMS, (norm_mix_g, b_gate, s5_d, s5_b_glu, lru_conv_w, lru_conv_b, lru_w_r, lru_b_r,
                                     lru_w_i, lru_b_i, lru_lambda, norm_mlp_g)))
    bsz, seq, d = x.shape
    depth = w_in.shape[0]
    off_q = s5_d.shape[1] + lru_lambda.shape[1]
    col = jnp.arange(w_in.shape[-1])
    q_scale = jnp.where((col >= off_q) & (col < off_q + w_br_attn.shape[1]), HEAD_DIM ** -0.5, 1.0).astype(F32)
    big = {"w_in": (w_in * q_scale).astype(BF16), "s5_w_glu": s5_w_glu.astype(BF16),
           "w_br_s5": w_br_s5.astype(BF16), "w_br_lru": w_br_lru.astype(BF16), "w_br_attn": w_br_attn.astype(BF16),
           "w_out": w_out.astype(BF16), "w_up": w_up.astype(BF16), "w_down": w_down.astype(BF16),
           "s5_tables": _s5_tables(s5_lam_re, s5_lam_im, s5_log_dt, s5_b_re, s5_b_im, s5_c_re, s5_c_im)}
    h = x.reshape(bsz * seq, d).astype(F32)
    for layer in range(depth):
        p = {k: v[layer] for k, v in small.items()}
        h = _layer(h, bsz, seq, p, big, layer, final_g=final_norm_g if layer == depth - 1 else None)
    return h.reshape(bsz, seq, d).astype(x.dtype)
```

```python
import functools
import math

import jax
import jax.numpy as jnp
from jax import lax
from jax.experimental import pallas as pl
from jax.experimental.pallas import tpu as pltpu

F32 = jnp.float32
BF16 = jnp.bfloat16

EPS = 1e-6
LANES = 128
HEAD_DIM = 128
ATTN_UNDERFLOW = 128.0
S5_CHUNK = 16
LRU_SUPER = 256
LRU_C = 8.0
N_BRANCH = 3
LOG2E = 1.4426950408889634
VMEM_LIMIT = 56 * 1024 * 1024
HIGHEST = lax.Precision.HIGHEST


def _cparams(sem):
    return pltpu.CompilerParams(dimension_semantics=sem, vmem_limit_bytes=VMEM_LIMIT)


def _softplus(z):
    return jnp.maximum(z, 0.0) + jnp.log(1.0 + jnp.exp2(jnp.abs(z) * (-LOG2E)))


def _sigmoid(z):
    return 1.0 / (1.0 + jnp.exp(-z))


def _rms_rows(x, g):
    ms = jnp.mean(x * x, axis=-1, keepdims=True)
    return (x * lax.rsqrt(ms + EPS)) * g


def _inproj_kernel(x_ref, g_ref, w_ref, o_ref, o32_ref, xn_ref, *, row_chunk):
    j = pl.program_id(1)

    @pl.when(j == 0)
    def _():
        g = g_ref[...]
        w = w_ref[...]
        for r in range(0, x_ref.shape[0], row_chunk):
            xn = _rms_rows(x_ref[r:r + row_chunk, :], g).astype(BF16)
            xn_ref[r:r + row_chunk, :] = xn
            acc = jnp.dot(xn, w, preferred_element_type=F32)
            o_ref[r:r + row_chunk, :] = acc.astype(o_ref.dtype)
            o32_ref[r:r + row_chunk, :] = acc

    @pl.when(j > 0)
    def _():
        o_ref[...] = jnp.dot(xn_ref[...], w_ref[...], preferred_element_type=F32).astype(o_ref.dtype)


def _inproj(h, g, w, layer, *, tm, tn):
    t, d = h.shape
    n = w.shape[-1]
    return pl.pallas_call(
        functools.partial(_inproj_kernel, row_chunk=256),
        out_shape=(jax.ShapeDtypeStruct((t, n), BF16), jax.ShapeDtypeStruct((t, tn), F32)),
        grid=(t // tm, n // tn),
        in_specs=[pl.BlockSpec((tm, d), lambda i, j: (i, 0)),
                  pl.BlockSpec((1, d), lambda i, j: (0, 0)),
                  pl.BlockSpec((None, d, tn), lambda i, j: (layer, 0, j))],
        out_specs=(pl.BlockSpec((tm, tn), lambda i, j: (i, j)),
                   pl.BlockSpec((tm, tn), lambda i, j: (i, 0))),
        scratch_shapes=[pltpu.VMEM((tm, d), BF16)],
        compiler_params=_cparams(("parallel", "arbitrary")),
        name="inproj",
    )(h, g.reshape(1, d), w)


def _lane_diag(a_ref, b_ref, o_ref, *, gt, c):
    rows, sub = a_ref.shape
    width = gt * sub
    src = lax.broadcasted_iota(jnp.int32, (sub, width), 0)
    dst = lax.broadcasted_iota(jnp.int32, (sub, width), 1)
    spread = jnp.where(dst % sub == src, 1.0, 0.0).astype(BF16)
    row_g = (lax.broadcasted_iota(jnp.int32, (rows, width), 0) // c) % gt
    col_g = lax.broadcasted_iota(jnp.int32, (rows, width), 1) // sub
    keep = row_g == col_g
    for n, x_ref in enumerate((a_ref, b_ref)):
        tiled = jnp.dot(x_ref[...].astype(BF16), spread, preferred_element_type=F32)
        o_ref[:, n * width:(n + 1) * width] = jnp.where(keep, tiled, 0.0).astype(o_ref.dtype)


def _s5_kernel(u_ref, g2a_ref, g2b_ref, wsa_ref, wsb_ref, wra_ref, wrb_ref, lr_ref, li_ref, o_ref,
               g2_ref, ws_ref, wrt_ref, xre_ref, xim_ref, hre_ref, him_ref, *, tc, gt, c):
    @pl.when(pl.program_id(1) == 0)
    def _():
        _lane_diag(g2a_ref, g2b_ref, g2_ref, gt=gt, c=c)
        _lane_diag(wsa_ref, wsb_ref, ws_ref, gt=gt, c=c)
        _lane_diag(wra_ref, wrb_ref, wrt_ref, gt=gt, c=c)

    rows = u_ref.shape[0] // tc
    lanes = u_ref.shape[1]
    xc = jnp.concatenate([u_ref[pl.ds(t, rows, stride=tc), :].astype(BF16) for t in range(tc)], axis=1)
    lr = lr_ref[...]
    li = li_ref[...]
    sp = lr.shape[1]
    x = jnp.dot(xc, ws_ref[...], preferred_element_type=F32)
    xre_ref[...] = x[:, :sp]
    xim_ref[...] = x[:, sp:]

    def body(k, carry):
        hr, hi = carry
        base = pl.multiple_of(k * 8, 8)
        for r in range(8):
            hre_ref[pl.ds(base + r, 1), :] = hr
            him_ref[pl.ds(base + r, 1), :] = hi
            xr = xre_ref[pl.ds(base + r, 1), :]
            xi = xim_ref[pl.ds(base + r, 1), :]
            hr, hi = lr * hr - li * hi + xr, lr * hi + li * hr + xi
        return hr, hi

    zero = jnp.zeros(lr.shape, F32)
    lax.fori_loop(0, rows // 8, body, (zero, zero))

    h0 = jnp.concatenate([hre_ref[...].astype(BF16), him_ref[...].astype(BF16)], axis=1)
    yr = lax.dot_general(h0, wrt_ref[...], (((1,), (1,)), ((), ())), preferred_element_type=F32)
    nblk = g2_ref.shape[0] // lanes
    for m in range(tc // 2):
        kdim = (2 * m + 2) * lanes
        y = jnp.dot(xc[:, :kdim], g2_ref[(nblk * lanes - kdim):, :], preferred_element_type=F32)
        y = y + yr[:, 2 * m * lanes:(2 * m + 2) * lanes]
        o_ref[pl.ds(2 * m, rows, stride=tc), :] = y[:, :lanes]
        o_ref[pl.ds(2 * m + 1, rows, stride=tc), :] = y[:, lanes:]


def _s5_factors(lam_re, lam_im, log_dt, b_re, b_im, c_re, c_im):
    tc = S5_CHUNK
    g, p = lam_re.shape
    c = b_re.shape[-1]
    lr = lam_re.astype(F32)
    li = lam_im.astype(F32)
    dt = jnp.exp(log_dt.astype(F32))[:, None]
    ar, ai = lr * dt, li * dt

    def powers(ks):
        k3 = ks.astype(F32)[None, :, None]
        mag = jnp.exp(k3 * ar[:, None, :])
        return mag * jnp.cos(k3 * ai[:, None, :]), mag * jnp.sin(k3 * ai[:, None, :])

    pdr, pdi = powers((tc - 1) - jnp.arange(tc))
    par, pai = powers(1 + jnp.arange(tc))
    nr = jnp.expm1(ar) * jnp.cos(ai) - 2.0 * jnp.sin(0.5 * ai) ** 2
    ni = jnp.exp(ar) * jnp.sin(ai)
    den = lr * lr + li * li
    fr = ((nr * lr + ni * li) / den)[:, None, :]
    fi = ((ni * lr - nr * li) / den)[:, None, :]
    bt_re, bt_im = b_re.transpose(0, 2, 1), b_im.transpose(0, 2, 1)
    bbr = fr * bt_re - fi * bt_im
    bbi = fr * bt_im + fi * bt_re
    er = pdr[:, :, None, :] * bbr[:, None] - pdi[:, :, None, :] * bbi[:, None]
    ei = pdr[:, :, None, :] * bbi[:, None] + pdi[:, :, None, :] * bbr[:, None]
    kt = (jnp.einsum('grip,gop->grio', er, c_re, precision=HIGHEST)
          - jnp.einsum('grip,gop->grio', ei, c_im, precision=HIGHEST))
    gt = LANES // c
    nt = g // gt

    def tile_rows(x):
        a, sub = x.shape[1], x.shape[-1]
        return x.reshape(nt, gt, a, c, sub).transpose(0, 2, 1, 3, 4).reshape(nt, a * gt * c, sub)

    zblk = jnp.zeros_like(kt[:, :1])
    g2_ab = (tile_rows(jnp.concatenate([kt, zblk], axis=1)), tile_rows(jnp.concatenate([zblk, kt], axis=1)))
    ws_ab = (tile_rows(er), tile_rows(ei))
    qr = c_re[:, None] * par[:, :, None, :] - c_im[:, None] * pai[:, :, None, :]
    qi = c_re[:, None] * pai[:, :, None, :] + c_im[:, None] * par[:, :, None, :]
    wrt_ab = (tile_rows(qr), tile_rows(-qi))
    return g2_ab, ws_ab, wrt_ab, par[:, tc - 1].reshape(nt, 1, gt * p), pai[:, tc - 1].reshape(nt, 1, gt * p)


def _s5_tables(lam_re, lam_im, log_dt, b_re, b_im, c_re, c_im):
    return jax.vmap(_s5_factors)(lam_re, lam_im, log_dt, b_re, b_im, c_re, c_im)


def _s5_scan(u, tables, layer, *, seq):
    t, w = u.shape
    (g2a, g2b), (wsa, wsb), (wra, wrb), lr, li = tables
    nt, sp = lr.shape[1], lr.shape[-1]
    c = g2a.shape[-1]
    gt = LANES // c
    rows = seq // S5_CHUNK
    tile = lambda k, b: (layer, k, 0, 0)
    factor = lambda x: pl.BlockSpec((None, None) + x.shape[2:], tile)
    table = lambda x: pltpu.VMEM((x.shape[2], 2 * gt * x.shape[3]), BF16)
    return pl.pallas_call(
        functools.partial(_s5_kernel, tc=S5_CHUNK, gt=gt, c=c),
        out_shape=jax.ShapeDtypeStruct((t, w), F32),
        grid=(nt, t // seq),
        in_specs=[pl.BlockSpec((seq, LANES), lambda k, b: (b, k)),
                  factor(g2a), factor(g2b), factor(wsa), factor(wsb), factor(wra), factor(wrb),
                  factor(lr), factor(li)],
        out_specs=pl.BlockSpec((seq, LANES), lambda k, b: (b, k)),
        scratch_shapes=[table(g2a), table(wsa), table(wra)] + [pltpu.VMEM((rows, sp), F32)] * 4,
        compiler_params=_cparams(("parallel", "arbitrary")),
        name="s5_scan",
    )(u, g2a, g2b, wsa, wsb, wra, wrb, lr, li)


def _s5_post_kernel(y_ref, u_ref, d_ref, w_ref, b_ref, o_ref):
    y = y_ref[...] + d_ref[...] * u_ref[...].astype(F32)
    c0 = math.sqrt(2.0 / math.pi)
    y = 0.5 * y * (1.0 + jnp.tanh(c0 * (y + 0.044715 * (y * y * y))))
    gate = jnp.dot(y.astype(BF16), w_ref[...], preferred_element_type=F32) + b_ref[...]
    o_ref[...] = (y * _sigmoid(gate)).astype(o_ref.dtype)


def _s5_post(y, u, d_skip, w_glu, b_glu, layer, *, tm):
    t, w = y.shape
    return pl.pallas_call(
        _s5_post_kernel,
        out_shape=jax.ShapeDtypeStruct((t, w), BF16),
        grid=(t // tm,),
        in_specs=[pl.BlockSpec((tm, w), lambda i: (i, 0)),
                  pl.BlockSpec((tm, w), lambda i: (i, 0)),
                  pl.BlockSpec((1, w), lambda i: (0, 0)),
                  pl.BlockSpec((None, w, w), lambda i: (layer, 0, 0)),
                  pl.BlockSpec((1, w), lambda i: (0, 0))],
        out_specs=pl.BlockSpec((tm, w), lambda i: (i, 0)),
        compiler_params=_cparams(("parallel",)),
        name="s5_post",
    )(y, u, d_skip.reshape(1, w), w_glu, b_glu.reshape(1, w))


def _lru_kernel(x_ref, cw_ref, cb_ref, wr_ref, br_ref, wi_ref, bi_ref, lam_ref, o_ref,
                xbuf_ref, a_ref, g_ref, h_ref, *, tl, kw):
    pad = 8
    w = x_ref.shape[1]

    @pl.when(pl.program_id(1) == 0)
    def _():
        xbuf_ref[pl.ds(0, pad), :] = jnp.zeros((pad, w), F32)
        h_ref[...] = jnp.zeros_like(h_ref)

    @pl.when(pl.program_id(1) > 0)
    def _():
        xbuf_ref[pl.ds(0, pad), :] = xbuf_ref[pl.ds(tl, pad), :]

    xbuf_ref[pl.ds(pad, tl), :] = x_ref[...].astype(F32)
    cw = cw_ref[...]
    xc = cb_ref[...] + xbuf_ref[pl.ds(pad - (kw - 1), tl), :] * cw[0:1, :]
    for k in range(1, kw):
        xc = xc + xbuf_ref[pl.ds(pad - (kw - 1) + k, tl), :] * cw[k:k + 1, :]
    xcb = xc.astype(BF16)
    nsup = w // LRU_SUPER
    rs, is_ = [], []
    for s in range(nsup):
        xs = xcb[:, s * LRU_SUPER:(s + 1) * LRU_SUPER]
        rs.append(jnp.dot(xs, wr_ref[s], preferred_element_type=F32))
        is_.append(jnp.dot(xs, wi_ref[s], preferred_element_type=F32))
    r = _sigmoid(jnp.concatenate(rs, axis=1) + br_ref[...])
    i = _sigmoid(jnp.concatenate(is_, axis=1) + bi_ref[...])
    log_a = (-LRU_C) * r * _softplus(-lam_ref[...])
    a_ref[...] = jnp.exp(log_a)
    th = jnp.tanh(log_a)
    g_ref[...] = jnp.sqrt((-2.0 * th) / (1.0 - th)) * (i * xc)

    row8 = lax.broadcasted_iota(jnp.int32, (8, w), 0)

    def body(t8, h):
        base = pl.multiple_of(t8 * 8, 8)
        a = a_ref[pl.ds(base, 8), :]
        b = g_ref[pl.ds(base, 8), :]
        for sh in (1, 2, 4):
            ok = row8 >= sh
            a_prev = jnp.where(ok, pltpu.roll(a, sh, axis=0), 1.0)
            b_prev = jnp.where(ok, pltpu.roll(b, sh, axis=0), 0.0)
            b = a * b_prev + b
            a = a * a_prev
        hs = a * h + b
        g_ref[pl.ds(base, 8), :] = hs
        return hs[7:8, :]

    h_ref[...] = lax.fori_loop(0, tl // 8, body, h_ref[...])
    o_ref[...] = g_ref[...].astype(o_ref.dtype)


def _block_diag_super(wblk):
    n, k, _ = wblk.shape
    per = LRU_SUPER // k
    wb = wblk.reshape(n // per, per, k, k)
    eye = jnp.eye(per, dtype=wblk.dtype)
    sup = wb[:, :, :, None, :] * eye[None, :, None, :, None]
    return sup.reshape(n // per, LRU_SUPER, LRU_SUPER)


def _lru(proj3, col_block, conv_w, conv_b, w_r, b_r, w_i, b_i, lam, *, tl):
    b, l, _ = proj3.shape
    w = conv_w.shape[1]
    kw = conv_w.shape[0]
    wr = _block_diag_super(w_r).astype(BF16)
    wi = _block_diag_super(w_i).astype(BF16)
    vec = lambda bi, ti: (0, 0)
    full3 = lambda bi, ti: (0, 0, 0)
    return pl.pallas_call(
        functools.partial(_lru_kernel, tl=tl, kw=kw),
        out_shape=jax.ShapeDtypeStruct((b, l, w), BF16),
        grid=(b, l // tl),
        in_specs=[pl.BlockSpec((None, tl, w), lambda bi, ti: (bi, ti, col_block)),
                  pl.BlockSpec((kw, w), vec),
                  pl.BlockSpec((1, w), vec),
                  pl.BlockSpec(wr.shape, full3),
                  pl.BlockSpec((1, w), vec),
                  pl.BlockSpec(wi.shape, full3),
                  pl.BlockSpec((1, w), vec),
                  pl.BlockSpec((1, w), vec)],
        out_specs=pl.BlockSpec((None, tl, w), lambda bi, ti: (bi, ti, 0)),
        scratch_shapes=[pltpu.VMEM((tl + 8, w), F32), pltpu.VMEM((tl, w), F32),
                        pltpu.VMEM((tl, w), F32), pltpu.VMEM((1, w), F32)],
        compiler_params=_cparams(("parallel", "arbitrary")),
        name="rglru",
    )(proj3, conv_w, conv_b.reshape(1, w), wr, b_r.reshape(1, w), wi, b_i.reshape(1, w), lam.reshape(1, w))


def _attn_kernel(q_ref, k_ref, v_ref, o_ref, *, tq, hp):
    qi = pl.program_id(2)
    d = HEAD_DIM
    row = lax.broadcasted_iota(jnp.int32, (tq, tq), 0)
    col = lax.broadcasted_iota(jnp.int32, (tq, tq), 1)
    suffix = jnp.where(row > col, 1.0, 0.0).astype(BF16)
    causal = col < row
    qs = [q_ref[:, h * d:(h + 1) * d] for h in range(hp)]

    def blocks(j_lo, nb, state, diag):
        ks = pl.multiple_of(j_lo * tq, tq)
        heads = range(hp)
        masked = [diag and n == nb - 1 for n in range(nb)]
        zs = [lax.dot_general(qs[h], k_ref[pl.ds(ks, nb * tq), h * d:(h + 1) * d], (((1,), (1,)), ((), ())),
                              preferred_element_type=F32) for h in heads]
        sps = []
        for h in heads:
            sp = _softplus(zs[h])
            parts = [sp[:, n * tq:(n + 1) * tq] for n in range(nb)]
            sps.append([jnp.where(causal, s, 0.0) if m else s for s, m in zip(parts, masked)])
        sss = [[jnp.dot(sps[h][n].astype(BF16), suffix, preferred_element_type=F32) for n in range(nb)]
               for h in heads]
        out = []
        for h in heads:
            carry, acc = state[2 * h], state[2 * h + 1]
            ws = [None] * nb
            for n in range(nb - 1, -1, -1):
                w = jnp.exp(zs[h][:, n * tq:(n + 1) * tq] - sps[h][n] - sss[h][n] - carry)
                ws[n] = (jnp.where(causal, w, 0.0) if masked[n] else w).astype(BF16)
                carry = carry + jnp.sum(sps[h][n], axis=-1, keepdims=True)
            acc = acc + jnp.dot(jnp.concatenate(ws, axis=1), v_ref[pl.ds(ks, nb * tq), h * d:(h + 1) * d],
                                preferred_element_type=F32)
            out.extend((carry, acc))
        return tuple(out)

    def underflowed(state):
        low = state[0]
        for h in range(1, hp):
            low = jnp.minimum(low, state[2 * h])
        return (jnp.min(low) > ATTN_UNDERFLOW).astype(jnp.int32)

    def keep_going(c):
        return jnp.logical_and(c[0] < qi, c[1] == 0)

    def sweep(c):
        new = blocks(qi - 1 - c[0], 1, c[2:], False)
        return (c[0] + 1, underflowed(new)) + new

    init = (jnp.zeros((tq, 1), F32), jnp.zeros((tq, d), F32)) * hp
    state = lax.cond(qi > 0, lambda s: blocks(qi - 1, 2, s, True), lambda s: blocks(qi, 1, s, True), init)
    state = lax.while_loop(keep_going, sweep, (jnp.int32(1), underflowed(state)) + state)[2:]
    for h in range(hp):
        o_ref[:, h * d:(h + 1) * d] = state[2 * h + 1].astype(o_ref.dtype)


def _attention(proj3, q_blk, k_blk, v_blk, n_heads, *, tq, hp):
    b, l, _ = proj3.shape
    d = HEAD_DIM
    return pl.pallas_call(
        functools.partial(_attn_kernel, tq=tq, hp=hp),
        out_shape=jax.ShapeDtypeStruct((b, l, n_heads * d), BF16),
        grid=(b, n_heads // hp, l // tq),
        in_specs=[pl.BlockSpec((None, tq, hp * d), lambda bi, hi, qi: (bi, qi, q_blk // hp + hi)),
                  pl.BlockSpec((None, l, hp * d), lambda bi, hi, qi: (bi, 0, k_blk // hp + hi)),
                  pl.BlockSpec((None, l, hp * d), lambda bi, hi, qi: (bi, 0, v_blk // hp + hi))],
        out_specs=pl.BlockSpec((None, tq, hp * d), lambda bi, hi, qi: (bi, qi, hi)),
        compiler_params=_cparams(("parallel", "parallel", "arbitrary")),
        name="stick_attn",
    )(proj3, proj3, proj3)


def _merge_kernel(ya_ref, yb_ref, yc_ref, wa_ref, wb_ref, wc_ref, ga_ref, gb_ref, gc_ref, bg_ref, o_ref):
    bg = bg_ref[...]
    out = None
    for n, (y_ref, w_ref, gl_ref) in enumerate(((ya_ref, wa_ref, ga_ref), (yb_ref, wb_ref, gb_ref),
                                                (yc_ref, wc_ref, gc_ref))):
        gate = _sigmoid(gl_ref[...].astype(F32) + bg[n:n + 1, :])
        term = gate * jnp.dot(y_ref[...], w_ref[...], preferred_element_type=F32)
        out = term if out is None else out + term
    o_ref[...] = out.astype(o_ref.dtype)


def _merge(ys, ws, proj, gate_col, b_gate, layer, *, tm, tn):
    t, w = ys[0].shape
    d = ws[0].shape[-1]
    gblk = gate_col // tn
    per = d // tn
    y_spec = pl.BlockSpec((tm, w), lambda i, j: (i, 0))
    w_spec = pl.BlockSpec((None, w, tn), lambda i, j: (layer, 0, j))
    gate_specs = [pl.BlockSpec((tm, tn), functools.partial(lambda i, j, n: (i, gblk + n * per + j), n=n))
                  for n in range(N_BRANCH)]
    return pl.pallas_call(
        _merge_kernel,
        out_shape=jax.ShapeDtypeStruct((t, d), BF16),
        grid=(t // tm, d // tn),
        in_specs=[y_spec] * 3 + [w_spec] * 3 + gate_specs
                 + [pl.BlockSpec((N_BRANCH, tn), lambda i, j: (0, j))],
        out_specs=pl.BlockSpec((tm, tn), lambda i, j: (i, j)),
        compiler_params=_cparams(("parallel", "arbitrary")),
        name="merge",
    )(*ys, *ws, proj, proj, proj, b_gate.reshape(N_BRANCH, d))


def _outproj_kernel(m_ref, w_ref, h_ref, g_ref, ho_ref, hn_ref):
    h = h_ref[...] + jnp.dot(m_ref[...], w_ref[...], preferred_element_type=F32)
    ho_ref[...] = h
    hn_ref[...] = _rms_rows(h, g_ref[...]).astype(hn_ref.dtype)


def _outproj(merged, w_out, h, g, layer, *, tm):
    t, d = h.shape
    row = lambda i: (i, 0)
    return pl.pallas_call(
        _outproj_kernel,
        out_shape=(jax.ShapeDtypeStruct((t, d), F32), jax.ShapeDtypeStruct((t, d), BF16)),
        grid=(t // tm,),
        in_specs=[pl.BlockSpec((tm, d), row),
                  pl.BlockSpec((None, d, d), lambda i: (layer, 0, 0)),
                  pl.BlockSpec((tm, d), row),
                  pl.BlockSpec((1, d), lambda i: (0, 0))],
        out_specs=(pl.BlockSpec((tm, d), row), pl.BlockSpec((tm, d), row)),
        compiler_params=_cparams(("parallel",)),
        name="outproj",
    )(merged, w_out, h, g.reshape(1, d))


def _mlp_kernel(x_ref, wu_ref, wd_ref, h_ref, g_ref, o_ref, *, final_norm, parts):
    f = pl.program_id(1)

    @pl.when(f == 0)
    def _():
        o_ref[...] = h_ref[...]

    x = x_ref[...]
    part = wu_ref.shape[1] // parts
    hids = [jnp.maximum(jnp.dot(x, wu_ref[:, n * part:(n + 1) * part], preferred_element_type=F32), 0.0)
            for n in range(parts)]
    hid = jnp.concatenate([(hd * hd).astype(BF16) for hd in hids], axis=1)
    o_ref[...] += jnp.dot(hid, wd_ref[...], preferred_element_type=F32)

    if final_norm:
        @pl.when(f == pl.num_programs(1) - 1)
        def _():
            o_ref[...] = _rms_rows(o_ref[...], g_ref[...])


def _mlp(hn, w_up, w_down, h, g, layer, *, tm, tf, final_norm):
    t, d = h.shape
    ff = w_up.shape[-1]
    row = lambda i, f: (i, 0)
    return pl.pallas_call(
        functools.partial(_mlp_kernel, final_norm=final_norm, parts=2),
        out_shape=jax.ShapeDtypeStruct((t, d), F32),
        grid=(t // tm, ff // tf),
        in_specs=[pl.BlockSpec((tm, d), row),
                  pl.BlockSpec((None, d, tf), lambda i, f: (layer, 0, f)),
                  pl.BlockSpec((None, tf, d), lambda i, f: (layer, f, 0)),
                  pl.BlockSpec((tm, d), row),
                  pl.BlockSpec((1, d), lambda i, f: (0, 0))],
        out_specs=pl.BlockSpec((tm, d), row),
        compiler_params=_cparams(("parallel", "arbitrary")),
        name="mlp",
    )(hn, w_up, w_down, h, g.reshape(1, d))


def _layer(h, bsz, seq, p, big, layer, *, final_g):
    t, d = h.shape
    s5_w = p["s5_d"].shape[0]
    lru_w = p["lru_lambda"].shape[0]
    attn_w = big["w_br_attn"].shape[1]
    n_heads = attn_w // HEAD_DIM
    off_lru = s5_w
    off_q = off_lru + lru_w
    off_k = off_q + attn_w
    off_v = off_k + attn_w
    off_gate = off_v + attn_w

    proj, u_s5 = _inproj(h, p["norm_mix_g"], big["w_in"], layer, tm=1024, tn=s5_w)
    proj3 = proj.reshape(bsz, seq, proj.shape[1])

    y_ssm = _s5_scan(u_s5, big["s5_tables"], layer, seq=seq)
    y_s5 = _s5_post(y_ssm, u_s5, p["s5_d"], big["s5_w_glu"], p["s5_b_glu"], layer, tm=1024)

    y_lru = _lru(proj3, off_lru // lru_w, p["lru_conv_w"], p["lru_conv_b"], p["lru_w_r"], p["lru_b_r"],
                 p["lru_w_i"], p["lru_b_i"], p["lru_lambda"], tl=512).reshape(t, lru_w)

    y_attn = _attention(proj3, off_q // HEAD_DIM, off_k // HEAD_DIM, off_v // HEAD_DIM, n_heads,
                        tq=256, hp=8).reshape(t, attn_w)

    merged = _merge((y_s5, y_lru, y_attn), (big["w_br_s5"], big["w_br_lru"], big["w_br_attn"]),
                    proj, off_gate, p["b_gate"], layer, tm=1024, tn=512)
    h, hn = _outproj(merged, big["w_out"], h, p["norm_mlp_g"], layer, tm=512)
    g_last = p["norm_mlp_g"] if final_g is None else final_g
    return _mlp(hn, big["w_up"], big["w_down"], h, g_last, layer,
                tm=512, tf=1024, final_norm=final_g is not None)


_SMALL_PARAMS = ("norm_mix_g", "b_gate", "s5_d", "s5_b_glu", "lru_conv_w", "lru_conv_b", "lru_w_r", "lru_b_r",
                 "lru_w_i", "lru_b_i", "lru_lambda", "norm_mlp_g")


def kernel(x, norm_mix_g, w_in, b_gate, s5_lam_re, s5_lam_im, s5_log_dt, s5_b_re, s5_b_im, s5_c_re, s5_c_im,
           s5_d, s5_w_glu, s5_b_glu, lru_conv_w, lru_conv_b, lru_w_r, lru_b_r, lru_w_i, lru_b_i, lru_lambda,
           w_br_s5, w_br_lru, w_br_attn, w_out, norm_mlp_g, w_up, w_down, final_norm_g):
    small = dict(zip(_SMALL_PARAMS, (norm_mix_g, b_gate, s5_d, s5_b_glu, lru_conv_w, lru_conv_b, lru_w_r, lru_b_r,
                                     lru_w_i, lru_b_i, lru_lambda, norm_mlp_g)))
    bsz, seq, d = x.shape
    depth = w_in.shape[0]
    off_q = s5_d.shape[1] + lru_lambda.shape[1]
    col = jnp.arange(w_in.shape[-1])
    q_scale = jnp.where((col >= off_q) & (col < off_q + w_br_attn.shape[1]), HEAD_DIM ** -0.5, 1.0).astype(F32)
    big = {"w_in": (w_in * q_scale).astype(BF16), "s5_w_glu": s5_w_glu.astype(BF16),
           "w_br_s5": w_br_s5.astype(BF16), "w_br_lru": w_br_lru.astype(BF16), "w_br_attn": w_br_attn.astype(BF16),
           "w_out": w_out.astype(BF16), "w_up": w_up.astype(BF16), "w_down": w_down.astype(BF16),
           "s5_tables": _s5_tables(s5_lam_re, s5_lam_im, s5_log_dt, s5_b_re, s5_b_im, s5_c_re, s5_c_im)}
    h = x.reshape(bsz * seq, d).astype(F32)
    for layer in range(depth):
        p = {k: v[layer] for k, v in small.items()}
        h = _layer(h, bsz, seq, p, big, layer, final_g=final_norm_g if layer == depth - 1 else None)
    return h.reshape(bsz, seq, d).astype(x.dtype)
```

```python
import functools
import math

import jax
import jax.numpy as jnp
from jax import lax
from jax.experimental import pallas as pl
from jax.experimental.pallas import tpu as pltpu

F32 = jnp.float32
BF16 = jnp.bfloat16

EPS = 1e-6
LANES = 128
HEAD_DIM = 128
ATTN_UNDERFLOW = 128.0
S5_GROUP = 16
S5_CHUNK = 16
LRU_BLOCK = 64
LRU_SUPER = 256
LRU_C = 8.0
N_BRANCH = 3
LOG2E = 1.4426950408889634
VMEM_LIMIT = 56 * 1024 * 1024
HIGHEST = lax.Precision.HIGHEST


def _cparams(sem):
    return pltpu.CompilerParams(dimension_semantics=sem, vmem_limit_bytes=VMEM_LIMIT)


def _softplus(z):
    return jnp.maximum(z, 0.0) + jnp.log(1.0 + jnp.exp2(jnp.abs(z) * (-LOG2E)))


def _sigmoid(z):
    return 1.0 / (1.0 + jnp.exp(-z))


def _rms_rows(x, g):
    ms = jnp.mean(x * x, axis=-1, keepdims=True)
    return (x * lax.rsqrt(ms + EPS)) * g


def _inproj_kernel(x_ref, g_ref, w_ref, o_ref, o32_ref, xn_ref, *, row_chunk):
    j = pl.program_id(1)

    @pl.when(j == 0)
    def _():
        g = g_ref[...]
        w = w_ref[...]
        for r in range(0, x_ref.shape[0], row_chunk):
            xn = _rms_rows(x_ref[r:r + row_chunk, :], g).astype(BF16)
            xn_ref[r:r + row_chunk, :] = xn
            acc = jnp.dot(xn, w, preferred_element_type=F32)
            o_ref[r:r + row_chunk, :] = acc.astype(o_ref.dtype)
            o32_ref[r:r + row_chunk, :] = acc

    @pl.when(j > 0)
    def _():
        o_ref[...] = jnp.dot(xn_ref[...], w_ref[...], preferred_element_type=F32).astype(o_ref.dtype)


def _inproj(h, g, w, layer, *, tm, tn):
    t, d = h.shape
    n = w.shape[-1]
    return pl.pallas_call(
        functools.partial(_inproj_kernel, row_chunk=256),
        out_shape=(jax.ShapeDtypeStruct((t, n), BF16), jax.ShapeDtypeStruct((t, tn), F32)),
        grid=(t // tm, n // tn),
        in_specs=[pl.BlockSpec((tm, d), lambda i, j: (i, 0)),
                  pl.BlockSpec((1, d), lambda i, j: (0, 0)),
                  pl.BlockSpec((None, d, tn), lambda i, j: (layer, 0, j))],
        out_specs=(pl.BlockSpec((tm, tn), lambda i, j: (i, j)),
                   pl.BlockSpec((tm, tn), lambda i, j: (i, 0))),
        scratch_shapes=[pltpu.VMEM((tm, d), BF16)],
        compiler_params=_cparams(("parallel", "arbitrary")),
        name="inproj",
    )(h, g.reshape(1, d), w)


def _lane_diag(a_ref, b_ref, o_ref, *, gt, c):
    rows, sub = a_ref.shape
    width = gt * sub
    src = lax.broadcasted_iota(jnp.int32, (sub, width), 0)
    dst = lax.broadcasted_iota(jnp.int32, (sub, width), 1)
    spread = jnp.where(dst % sub == src, 1.0, 0.0).astype(BF16)
    row_g = (lax.broadcasted_iota(jnp.int32, (rows, width), 0) // c) % gt
    col_g = lax.broadcasted_iota(jnp.int32, (rows, width), 1) // sub
    keep = row_g == col_g
    for n, x_ref in enumerate((a_ref, b_ref)):
        tiled = jnp.dot(x_ref[...].astype(BF16), spread, preferred_element_type=F32)
        o_ref[:, n * width:(n + 1) * width] = jnp.where(keep, tiled, 0.0).astype(o_ref.dtype)


def _s5_kernel(u_ref, g2a_ref, g2b_ref, wsa_ref, wsb_ref, wra_ref, wrb_ref, lr_ref, li_ref, o_ref,
               g2_ref, ws_ref, wrt_ref, xre_ref, xim_ref, hre_ref, him_ref, *, tc, gt, c):
    @pl.when(pl.program_id(1) == 0)
    def _():
        _lane_diag(g2a_ref, g2b_ref, g2_ref, gt=gt, c=c)
        _lane_diag(wsa_ref, wsb_ref, ws_ref, gt=gt, c=c)
        _lane_diag(wra_ref, wrb_ref, wrt_ref, gt=gt, c=c)

    rows = u_ref.shape[0] // tc
    lanes = u_ref.shape[1]
    xc = jnp.concatenate([u_ref[pl.ds(t, rows, stride=tc), :].astype(BF16) for t in range(tc)], axis=1)
    lr = lr_ref[...]
    li = li_ref[...]
    sp = lr.shape[1]
    x = jnp.dot(xc, ws_ref[...], preferred_element_type=F32)
    xre_ref[...] = x[:, :sp]
    xim_ref[...] = x[:, sp:]

    def body(k, carry):
        hr, hi = carry
        base = pl.multiple_of(k * 8, 8)
        for r in range(8):
            hre_ref[pl.ds(base + r, 1), :] = hr
            him_ref[pl.ds(base + r, 1), :] = hi
            xr = xre_ref[pl.ds(base + r, 1), :]
            xi = xim_ref[pl.ds(base + r, 1), :]
            hr, hi = lr * hr - li * hi + xr, lr * hi + li * hr + xi
        return hr, hi

    zero = jnp.zeros(lr.shape, F32)
    lax.fori_loop(0, rows // 8, body, (zero, zero))

    h0 = jnp.concatenate([hre_ref[...].astype(BF16), him_ref[...].astype(BF16)], axis=1)
    yr = lax.dot_general(h0, wrt_ref[...], (((1,), (1,)), ((), ())), preferred_element_type=F32)
    nblk = g2_ref.shape[0] // lanes
    for m in range(tc // 2):
        kdim = (2 * m + 2) * lanes
        y = jnp.dot(xc[:, :kdim], g2_ref[(nblk * lanes - kdim):, :], preferred_element_type=F32)
        y = y + yr[:, 2 * m * lanes:(2 * m + 2) * lanes]
        o_ref[pl.ds(2 * m, rows, stride=tc), :] = y[:, :lanes]
        o_ref[pl.ds(2 * m + 1, rows, stride=tc), :] = y[:, lanes:]


def _s5_factors(lam_re, lam_im, log_dt, b_re, b_im, c_re, c_im):
    tc = S5_CHUNK
    g, p = lam_re.shape
    c = b_re.shape[-1]
    lr = lam_re.astype(F32)
    li = lam_im.astype(F32)
    dt = jnp.exp(log_dt.astype(F32))[:, None]
    ar, ai = lr * dt, li * dt

    def powers(ks):
        k3 = ks.astype(F32)[None, :, None]
        mag = jnp.exp(k3 * ar[:, None, :])
        return mag * jnp.cos(k3 * ai[:, None, :]), mag * jnp.sin(k3 * ai[:, None, :])

    pdr, pdi = powers((tc - 1) - jnp.arange(tc))
    par, pai = powers(1 + jnp.arange(tc))
    nr = jnp.expm1(ar) * jnp.cos(ai) - 2.0 * jnp.sin(0.5 * ai) ** 2
    ni = jnp.exp(ar) * jnp.sin(ai)
    den = lr * lr + li * li
    fr = ((nr * lr + ni * li) / den)[:, None, :]
    fi = ((ni * lr - nr * li) / den)[:, None, :]
    bt_re, bt_im = b_re.transpose(0, 2, 1), b_im.transpose(0, 2, 1)
    bbr = fr * bt_re - fi * bt_im
    bbi = fr * bt_im + fi * bt_re
    er = pdr[:, :, None, :] * bbr[:, None] - pdi[:, :, None, :] * bbi[:, None]
    ei = pdr[:, :, None, :] * bbi[:, None] + pdi[:, :, None, :] * bbr[:, None]
    kt = (jnp.einsum('grip,gop->grio', er, c_re, precision=HIGHEST)
          - jnp.einsum('grip,gop->grio', ei, c_im, precision=HIGHEST))
    gt = LANES // c
    nt = g // gt

    def tile_rows(x):
        a, sub = x.shape[1], x.shape[-1]
        return x.reshape(nt, gt, a, c, sub).transpose(0, 2, 1, 3, 4).reshape(nt, a * gt * c, sub)

    zblk = jnp.zeros_like(kt[:, :1])
    g2_ab = (tile_rows(jnp.concatenate([kt, zblk], axis=1)), tile_rows(jnp.concatenate([zblk, kt], axis=1)))
    ws_ab = (tile_rows(er), tile_rows(ei))
    qr = c_re[:, None] * par[:, :, None, :] - c_im[:, None] * pai[:, :, None, :]
    qi = c_re[:, None] * pai[:, :, None, :] + c_im[:, None] * par[:, :, None, :]
    wrt_ab = (tile_rows(qr), tile_rows(-qi))
    return g2_ab, ws_ab, wrt_ab, par[:, tc - 1].reshape(nt, 1, gt * p), pai[:, tc - 1].reshape(nt, 1, gt * p)


def _s5_tables(lam_re, lam_im, log_dt, b_re, b_im, c_re, c_im):
    return jax.vmap(_s5_factors)(lam_re, lam_im, log_dt, b_re, b_im, c_re, c_im)


def _s5_scan(u, tables, layer, *, seq):
    t, w = u.shape
    (g2a, g2b), (wsa, wsb), (wra, wrb), lr, li = tables
    nt, sp = lr.shape[1], lr.shape[-1]
    c = g2a.shape[-1]
    gt = LANES // c
    rows = seq // S5_CHUNK
    tile = lambda k, b: (layer, k, 0, 0)
    factor = lambda x: pl.BlockSpec((None, None) + x.shape[2:], tile)
    table = lambda x: pltpu.VMEM((x.shape[2], 2 * gt * x.shape[3]), BF16)
    return pl.pallas_call(
        functools.partial(_s5_kernel, tc=S5_CHUNK, gt=gt, c=c),
        out_shape=jax.ShapeDtypeStruct((t, w), F32),
        grid=(nt, t // seq),
        in_specs=[pl.BlockSpec((seq, LANES), lambda k, b: (b, k)),
                  factor(g2a), factor(g2b), factor(wsa), factor(wsb), factor(wra), factor(wrb),
                  factor(lr), factor(li)],
        out_specs=pl.BlockSpec((seq, LANES), lambda k, b: (b, k)),
        scratch_shapes=[table(g2a), table(wsa), table(wra)] + [pltpu.VMEM((rows, sp), F32)] * 4,
        compiler_params=_cparams(("parallel", "arbitrary")),
        name="s5_scan",
    )(u, g2a, g2b, wsa, wsb, wra, wrb, lr, li)


def _s5_post_kernel(y_ref, u_ref, d_ref, w_ref, b_ref, o_ref):
    y = y_ref[...] + d_ref[...] * u_ref[...].astype(F32)
    c0 = math.sqrt(2.0 / math.pi)
    y = 0.5 * y * (1.0 + jnp.tanh(c0 * (y + 0.044715 * (y * y * y))))
    gate = jnp.dot(y.astype(BF16), w_ref[...], preferred_element_type=F32) + b_ref[...]
    o_ref[...] = (y * _sigmoid(gate)).astype(o_ref.dtype)


def _s5_post(y, u, d_skip, w_glu, b_glu, layer, *, tm):
    t, w = y.shape
    return pl.pallas_call(
        _s5_post_kernel,
        out_shape=jax.ShapeDtypeStruct((t, w), BF16),
        grid=(t // tm,),
        in_specs=[pl.BlockSpec((tm, w), lambda i: (i, 0)),
                  pl.BlockSpec((tm, w), lambda i: (i, 0)),
                  pl.BlockSpec((1, w), lambda i: (0, 0)),
                  pl.BlockSpec((None, w, w), lambda i: (layer, 0, 0)),
                  pl.BlockSpec((1, w), lambda i: (0, 0))],
        out_specs=pl.BlockSpec((tm, w), lambda i: (i, 0)),
        compiler_params=_cparams(("parallel",)),
        name="s5_post",
    )(y, u, d_skip.reshape(1, w), w_glu, b_glu.reshape(1, w))


def _lru_kernel(x_ref, cw_ref, cb_ref, wr_ref, br_ref, wi_ref, bi_ref, lam_ref, o_ref,
                xbuf_ref, a_ref, g_ref, h_ref, *, tl, kw):
    pad = 8
    w = x_ref.shape[1]

    @pl.when(pl.program_id(1) == 0)
    def _():
        xbuf_ref[pl.ds(0, pad), :] = jnp.zeros((pad, w), F32)
        h_ref[...] = jnp.zeros_like(h_ref)

    @pl.when(pl.program_id(1) > 0)
    def _():
        xbuf_ref[pl.ds(0, pad), :] = xbuf_ref[pl.ds(tl, pad), :]

    xbuf_ref[pl.ds(pad, tl), :] = x_ref[...].astype(F32)
    cw = cw_ref[...]
    xc = cb_ref[...] + xbuf_ref[pl.ds(pad - (kw - 1), tl), :] * cw[0:1, :]
    for k in range(1, kw):
        xc = xc + xbuf_ref[pl.ds(pad - (kw - 1) + k, tl), :] * cw[k:k + 1, :]
    xcb = xc.astype(BF16)
    nsup = w // LRU_SUPER
    rs, is_ = [], []
    for s in range(nsup):
        xs = xcb[:, s * LRU_SUPER:(s + 1) * LRU_SUPER]
        rs.append(jnp.dot(xs, wr_ref[s], preferred_element_type=F32))
        is_.append(jnp.dot(xs, wi_ref[s], preferred_element_type=F32))
    r = _sigmoid(jnp.concatenate(rs, axis=1) + br_ref[...])
    i = _sigmoid(jnp.concatenate(is_, axis=1) + bi_ref[...])
    log_a = (-LRU_C) * r * _softplus(-lam_ref[...])
    a_ref[...] = jnp.exp(log_a)
    th = jnp.tanh(log_a)
    g_ref[...] = jnp.sqrt((-2.0 * th) / (1.0 - th)) * (i * xc)

    row8 = lax.broadcasted_iota(jnp.int32, (8, w), 0)

    def body(t8, h):
        base = pl.multiple_of(t8 * 8, 8)
        a = a_ref[pl.ds(base, 8), :]
        b = g_ref[pl.ds(base, 8), :]
        for sh in (1, 2, 4):
            ok = row8 >= sh
            a_prev = jnp.where(ok, pltpu.roll(a, sh, axis=0), 1.0)
            b_prev = jnp.where(ok, pltpu.roll(b, sh, axis=0), 0.0)
            b = a * b_prev + b
            a = a * a_prev
        hs = a * h + b
        g_ref[pl.ds(base, 8), :] = hs
        return hs[7:8, :]

    h_ref[...] = lax.fori_loop(0, tl // 8, body, h_ref[...])
    o_ref[...] = g_ref[...].astype(o_ref.dtype)


def _block_diag_super(wblk):
    n, k, _ = wblk.shape
    per = LRU_SUPER // k
    wb = wblk.reshape(n // per, per, k, k)
    eye = jnp.eye(per, dtype=wblk.dtype)
    sup = wb[:, :, :, None, :] * eye[None, :, None, :, None]
    return sup.reshape(n // per, LRU_SUPER, LRU_SUPER)


def _lru(proj3, col_block, conv_w, conv_b, w_r, b_r, w_i, b_i, lam, *, tl):
    b, l, _ = proj3.shape
    w = conv_w.shape[1]
    kw = conv_w.shape[0]
    wr = _block_diag_super(w_r).astype(BF16)
    wi = _block_diag_super(w_i).astype(BF16)
    vec = lambda bi, ti: (0, 0)
    full3 = lambda bi, ti: (0, 0, 0)
    return pl.pallas_call(
        functools.partial(_lru_kernel, tl=tl, kw=kw),
        out_shape=jax.ShapeDtypeStruct((b, l, w), BF16),
        grid=(b, l // tl),
        in_specs=[pl.BlockSpec((None, tl, w), lambda bi, ti: (bi, ti, col_block)),
                  pl.BlockSpec((kw, w), vec),
                  pl.BlockSpec((1, w), vec),
                  pl.BlockSpec(wr.shape, full3),
                  pl.BlockSpec((1, w), vec),
                  pl.BlockSpec(wi.shape, full3),
                  pl.BlockSpec((1, w), vec),
                  pl.BlockSpec((1, w), vec)],
        out_specs=pl.BlockSpec((None, tl, w), lambda bi, ti: (bi, ti, 0)),
        scratch_shapes=[pltpu.VMEM((tl + 8, w), F32), pltpu.VMEM((tl, w), F32),
                        pltpu.VMEM((tl, w), F32), pltpu.VMEM((1, w), F32)],
        compiler_params=_cparams(("parallel", "arbitrary")),
        name="rglru",
    )(proj3, conv_w, conv_b.reshape(1, w), wr, b_r.reshape(1, w), wi, b_i.reshape(1, w), lam.reshape(1, w))


def _attn_kernel(q_ref, k_ref, v_ref, o_ref, *, tq, hp):
    qi = pl.program_id(2)
    d = HEAD_DIM
    row = lax.broadcasted_iota(jnp.int32, (tq, tq), 0)
    col = lax.broadcasted_iota(jnp.int32, (tq, tq), 1)
    suffix = jnp.where(row > col, 1.0, 0.0).astype(BF16)
    causal = col < row
    qs = [q_ref[:, h * d:(h + 1) * d] for h in range(hp)]

    def blocks(j_lo, nb, state, masked):
        ks = pl.multiple_of(j_lo * tq, tq)
        heads = range(hp)
        order = range(nb - 1, -1, -1)
        zs = [lax.dot_general(qs[h], k_ref[pl.ds(ks, nb * tq), h * d:(h + 1) * d], (((1,), (1,)), ((), ())),
                              preferred_element_type=F32) for h in heads]
        sps = []
        for h in heads:
            sp = _softplus(zs[h])
            sps.append(jnp.where(causal, sp, 0.0) if masked else sp)
        sss = [[jnp.dot(sps[h][:, n * tq:(n + 1) * tq].astype(BF16), suffix, preferred_element_type=F32)
                for n in range(nb)] for h in heads]
        out = []
        for h in heads:
            carry, acc = state[2 * h], state[2 * h + 1]
            ws = [None] * nb
            for n in order:
                cols = slice(n * tq, (n + 1) * tq)
                w = jnp.exp(zs[h][:, cols] - sps[h][:, cols] - sss[h][n] - carry)
                ws[n] = (jnp.where(causal, w, 0.0) if masked else w).astype(BF16)
                carry = carry + jnp.sum(sps[h][:, cols], axis=-1, keepdims=True)
            acc = acc + jnp.dot(jnp.concatenate(ws, axis=1), v_ref[pl.ds(ks, nb * tq), h * d:(h + 1) * d],
                                preferred_element_type=F32)
            out.extend((carry, acc))
        return tuple(out)

    state = blocks(qi, 1, (jnp.zeros((tq, 1), F32), jnp.zeros((tq, d), F32)) * hp, True)

    def keep_going(c):
        return jnp.logical_and(c[0] < qi, c[1] == 0)

    def sweep(c):
        jj = c[0]
        new = blocks(qi - 1 - jj, 1, c[2:], False)
        low = new[0]
        for h in range(1, hp):
            low = jnp.minimum(low, new[2 * h])
        done = (jnp.min(low) > ATTN_UNDERFLOW).astype(jnp.int32)
        return (jj + 1, done) + new

    state = lax.while_loop(keep_going, sweep, (jnp.int32(0), jnp.int32(0)) + state)[2:]
    for h in range(hp):
        o_ref[:, h * d:(h + 1) * d] = state[2 * h + 1].astype(o_ref.dtype)


def _attention(proj3, q_blk, k_blk, v_blk, n_heads, *, tq, hp):
    b, l, _ = proj3.shape
    d = HEAD_DIM
    return pl.pallas_call(
        functools.partial(_attn_kernel, tq=tq, hp=hp),
        out_shape=jax.ShapeDtypeStruct((b, l, n_heads * d), BF16),
        grid=(b, n_heads // hp, l // tq),
        in_specs=[pl.BlockSpec((None, tq, hp * d), lambda bi, hi, qi: (bi, qi, q_blk // hp + hi)),
                  pl.BlockSpec((None, l, hp * d), lambda bi, hi, qi: (bi, 0, k_blk // hp + hi)),
                  pl.BlockSpec((None, l, hp * d), lambda bi, hi, qi: (bi, 0, v_blk // hp + hi))],
        out_specs=pl.BlockSpec((None, tq, hp * d), lambda bi, hi, qi: (bi, qi, hi)),
        compiler_params=_cparams(("parallel", "parallel", "arbitrary")),
        name="stick_attn",
    )(proj3, proj3, proj3)


def _merge_kernel(ya_ref, yb_ref, yc_ref, wa_ref, wb_ref, wc_ref, ga_ref, gb_ref, gc_ref, bg_ref, o_ref):
    bg = bg_ref[...]
    out = None
    for n, (y_ref, w_ref, gl_ref) in enumerate(((ya_ref, wa_ref, ga_ref), (yb_ref, wb_ref, gb_ref),
                                                (yc_ref, wc_ref, gc_ref))):
        gate = _sigmoid(gl_ref[...].astype(F32) + bg[n:n + 1, :])
        term = gate * jnp.dot(y_ref[...], w_ref[...], preferred_element_type=F32)
        out = term if out is None else out + term
    o_ref[...] = out.astype(o_ref.dtype)


def _merge(ys, ws, proj, gate_col, b_gate, layer, *, tm, tn):
    t, w = ys[0].shape
    d = ws[0].shape[-1]
    gblk = gate_col // tn
    per = d // tn
    y_spec = pl.BlockSpec((tm, w), lambda i, j: (i, 0))
    w_spec = pl.BlockSpec((None, w, tn), lambda i, j: (layer, 0, j))
    gate_specs = [pl.BlockSpec((tm, tn), functools.partial(lambda i, j, n: (i, gblk + n * per + j), n=n))
                  for n in range(N_BRANCH)]
    return pl.pallas_call(
        _merge_kernel,
        out_shape=jax.ShapeDtypeStruct((t, d), BF16),
        grid=(t // tm, d // tn),
        in_specs=[y_spec] * 3 + [w_spec] * 3 + gate_specs
                 + [pl.BlockSpec((N_BRANCH, tn), lambda i, j: (0, j))],
        out_specs=pl.BlockSpec((tm, tn), lambda i, j: (i, j)),
        compiler_params=_cparams(("parallel", "arbitrary")),
        name="merge",
    )(*ys, *ws, proj, proj, proj, b_gate.reshape(N_BRANCH, d))


def _outproj_kernel(m_ref, w_ref, h_ref, g_ref, ho_ref, hn_ref):
    h = h_ref[...] + jnp.dot(m_ref[...], w_ref[...], preferred_element_type=F32)
    ho_ref[...] = h
    hn_ref[...] = _rms_rows(h, g_ref[...]).astype(hn_ref.dtype)


def _outproj(merged, w_out, h, g, layer, *, tm):
    t, d = h.shape
    row = lambda i: (i, 0)
    return pl.pallas_call(
        _outproj_kernel,
        out_shape=(jax.ShapeDtypeStruct((t, d), F32), jax.ShapeDtypeStruct((t, d), BF16)),
        grid=(t // tm,),
        in_specs=[pl.BlockSpec((tm, d), row),
                  pl.BlockSpec((None, d, d), lambda i: (layer, 0, 0)),
                  pl.BlockSpec((tm, d), row),
                  pl.BlockSpec((1, d), lambda i: (0, 0))],
        out_specs=(pl.BlockSpec((tm, d), row), pl.BlockSpec((tm, d), row)),
        compiler_params=_cparams(("parallel",)),
        name="outproj",
    )(merged, w_out, h, g.reshape(1, d))


def _mlp_kernel(x_ref, wu_ref, wd_ref, h_ref, g_ref, o_ref, *, final_norm, parts):
    f = pl.program_id(1)

    @pl.when(f == 0)
    def _():
        o_ref[...] = h_ref[...]

    x = x_ref[...]
    part = wu_ref.shape[1] // parts
    hids = [jnp.maximum(jnp.dot(x, wu_ref[:, n * part:(n + 1) * part], preferred_element_type=F32), 0.0)
            for n in range(parts)]
    hid = jnp.concatenate([(hd * hd).astype(BF16) for hd in hids], axis=1)
    o_ref[...] += jnp.dot(hid, wd_ref[...], preferred_element_type=F32)

    if final_norm:
        @pl.when(f == pl.num_programs(1) - 1)
        def _():
            o_ref[...] = _rms_rows(o_ref[...], g_ref[...])


def _mlp(hn, w_up, w_down, h, g, layer, *, tm, tf, final_norm):
    t, d = h.shape
    ff = w_up.shape[-1]
    row = lambda i, f: (i, 0)
    return pl.pallas_call(
        functools.partial(_mlp_kernel, final_norm=final_norm, parts=2),
        out_shape=jax.ShapeDtypeStruct((t, d), F32),
        grid=(t // tm, ff // tf),
        in_specs=[pl.BlockSpec((tm, d), row),
                  pl.BlockSpec((None, d, tf), lambda i, f: (layer, 0, f)),
                  pl.BlockSpec((None, tf, d), lambda i, f: (layer, f, 0)),
                  pl.BlockSpec((tm, d), row),
                  pl.BlockSpec((1, d), lambda i, f: (0, 0))],
        out_specs=pl.BlockSpec((tm, d), row),
        compiler_params=_cparams(("parallel", "arbitrary")),
        name="mlp",
    )(hn, w_up, w_down, h, g.reshape(1, d))


def _layer(h, bsz, seq, p, big, layer, *, final_g):
    t, d = h.shape
    s5_w = p["s5_d"].shape[0]
    lru_w = p["lru_lambda"].shape[0]
    attn_w = big["w_br_attn"].shape[1]
    n_heads = attn_w // HEAD_DIM
    off_lru = s5_w
    off_q = off_lru + lru_w
    off_k = off_q + attn_w
    off_v = off_k + attn_w
    off_gate = off_v + attn_w

    proj, u_s5 = _inproj(h, p["norm_mix_g"], big["w_in"], layer, tm=1024, tn=s5_w)
    proj3 = proj.reshape(bsz, seq, proj.shape[1])

    y_ssm = _s5_scan(u_s5, big["s5_tables"], layer, seq=seq)
    y_s5 = _s5_post(y_ssm, u_s5, p["s5_d"], big["s5_w_glu"], p["s5_b_glu"], layer, tm=1024)

    y_lru = _lru(proj3, off_lru // lru_w, p["lru_conv_w"], p["lru_conv_b"], p["lru_w_r"], p["lru_b_r"],
                 p["lru_w_i"], p["lru_b_i"], p["lru_lambda"], tl=512).reshape(t, lru_w)

    y_attn = _attention(proj3, off_q // HEAD_DIM, off_k // HEAD_DIM, off_v // HEAD_DIM, n_heads,
                        tq=256, hp=8).reshape(t, attn_w)

    merged = _merge((y_s5, y_lru, y_attn), (big["w_br_s5"], big["w_br_lru"], big["w_br_attn"]),
                    proj, off_gate, p["b_gate"], layer, tm=512, tn=1024)
    h, hn = _outproj(merged, big["w_out"], h, p["norm_mlp_g"], layer, tm=512)
    g_last = p["norm_mlp_g"] if final_g is None else final_g
    return _mlp(hn, big["w_up"], big["w_down"], h, g_last, layer,
                tm=512, tf=1024, final_norm=final_g is not None)


_SMALL_PARAMS = ("norm_mix_g", "b_gate", "s5_d", "s5_b_glu", "lru_conv_w", "lru_conv_b", "lru_w_r", "lru_b_r",
                 "lru_w_i", "lru_b_i", "lru_lambda", "norm_mlp_g")


def kernel(x, norm_mix_g, w_in, b_gate, s5_lam_re, s5_lam_im, s5_log_dt, s5_b_re, s5_b_im, s5_c_re, s5_c_im,
           s5_d, s5_w_glu, s5_b_glu, lru_conv_w, lru_conv_b, lru_w_r, lru_b_r, lru_w_i, lru_b_i, lru_lambda,
           w_br_s5, w_br_lru, w_br_attn, w_out, norm_mlp_g, w_up, w_down, final_norm_g):
    small = dict(zip(_SMALL_PARAMS, (norm_mix_g, b_gate, s5_d, s5_b_glu, lru_conv_w, lru_conv_b, lru_w_r, lru_b_r,
                                     lru_w_i, lru_b_i, lru_lambda, norm_mlp_g)))
    bsz, seq, d = x.shape
    depth = w_in.shape[0]
    off_q = s5_d.shape[1] + lru_lambda.shape[1]
    col = jnp.arange(w_in.shape[-1])
    q_scale = jnp.where((col >= off_q) & (col < off_q + w_br_attn.shape[1]), HEAD_DIM ** -0.5, 1.0).astype(F32)
    big = {"w_in": (w_in * q_scale).astype(BF16), "s5_w_glu": s5_w_glu.astype(BF16),
           "w_br_s5": w_br_s5.astype(BF16), "w_br_lru": w_br_lru.astype(BF16), "w_br_attn": w_br_attn.astype(BF16),
           "w_out": w_out.astype(BF16), "w_up": w_up.astype(BF16), "w_down": w_down.astype(BF16),
           "s5_tables": _s5_tables(s5_lam_re, s5_lam_im, s5_log_dt, s5_b_re, s5_b_im, s5_c_re, s5_c_im)}
    h = x.reshape(bsz * seq, d).astype(F32)
    for layer in range(depth):
        p = {k: v[layer] for k, v in small.items()}
        h = _layer(h, bsz, seq, p, big, layer, final_g=final_norm_g if layer == depth - 1 else None)
    return h.reshape(bsz, seq, d).astype(x.dtype)
```
